```python
import jax, jax.numpy as jnp
from jax import lax
import numpy as np

D_MODEL = 1024
BATCH = 8
SEQ = 8192
DEPTH = 4

CHUNK = 64
Q_BLOCK = 128
MEM_LEN = 256
FFN_HIDDEN = 2048
CONV_CH = 512
CONV_WIDTH = 31
SB_HEADS = 4
SB_HEAD_DIM = 128
MLA_HEADS = 4
MLA_NOPE = 128
MLA_ROPE = 64
MLA_V = 128
MLA_Q_LORA = 256
MLA_KV_LORA = 256
MEM_HEADS = 4
MEM_HEAD_DIM = 128
N_BRANCH = 4
BRANCH_WIDTH = 512
ROPE_BASE = 10000.0
EPS = 1e-6
NEG_INF = -1e30
IN_SPLITS = (2 * CONV_CH, 3 * SB_HEADS * SB_HEAD_DIM, MLA_Q_LORA, MLA_KV_LORA, MLA_ROPE, MEM_HEADS * MEM_HEAD_DIM, N_BRANCH * D_MODEL)
IN_WIDTH = 2 * CONV_CH + 3 * SB_HEADS * SB_HEAD_DIM + MLA_Q_LORA + MLA_KV_LORA + MLA_ROPE + MEM_HEADS * MEM_HEAD_DIM + N_BRANCH * D_MODEL

kernel_name = "hybrid_chunk_causal_encoder_trunk"


def rms_norm(x, g):
    xf = x.astype(jnp.float32)
    y = xf * lax.rsqrt(jnp.mean(xf * xf, axis=-1, keepdims=True) + EPS)
    return (y * g.astype(jnp.float32)).astype(x.dtype)


def layer_norm(x, g, b):
    xf = x.astype(jnp.float32)
    mu = jnp.mean(xf, axis=-1, keepdims=True)
    var = jnp.mean(jnp.square(xf - mu), axis=-1, keepdims=True)
    y = (xf - mu) * lax.rsqrt(var + EPS)
    return (y * g.astype(jnp.float32) + b.astype(jnp.float32)).astype(x.dtype)


def swiglu_ffn(h, w_in, w_out):
    gate, up = jnp.split(h @ w_in, 2, axis=-1)
    return (jax.nn.silu(gate) * up) @ w_out


def rope_tables(positions):
    inv_freq = ROPE_BASE ** (-jnp.arange(0, MLA_ROPE, 2, dtype=jnp.float32) / MLA_ROPE)
    ang = positions.astype(jnp.float32)[..., None] * inv_freq
    return jnp.cos(ang)[:, :, None, :], jnp.sin(ang)[:, :, None, :]


def rope_tail(x, cos, sin):
    x_pass, x_rot = x[..., :-MLA_ROPE], x[..., -MLA_ROPE:]
    x1, x2 = jnp.split(x_rot, 2, axis=-1)
    c, s = cos.astype(x.dtype), sin.astype(x.dtype)
    return jnp.concatenate([x_pass, x1 * c - x2 * s, x2 * c + x1 * s], axis=-1)


def causal_block_sweep(block_fn, q, k, v):
    nb = q.shape[1] // Q_BLOCK
    outs = []
    for i in range(nb):
        lo, hi = i * Q_BLOCK, (i + 1) * Q_BLOCK
        outs.append(block_fn(q[:, lo:hi], k[:, :hi], v[:, :hi], i))
    return jnp.concatenate(outs, axis=1)


def stick_breaking_attention(q, k, v):
    scale = SB_HEAD_DIM ** -0.5
    r = jnp.arange(Q_BLOCK, dtype=jnp.int32)
    later_in_block = (r[:, None] > r[None, :]).astype(jnp.float32)

    def block(qb, kk, vv, i):
        B, Q, H, _ = qb.shape
        L = kk.shape[1]
        nk = L // Q_BLOCK
        q_pos = i * Q_BLOCK + r
        strict = jnp.arange(L, dtype=jnp.int32)[None, :] < q_pos[:, None]
        z = jnp.einsum('bqhd,bkhd->bhqk', qb, kk).astype(jnp.float32) * scale
        log_keep = jnp.where(strict, -jax.nn.softplus(z), 0.0)
        lk = log_keep.reshape(B, H, Q, nk, Q_BLOCK)
        within = jnp.einsum('bhqnk,kj->bhqnj', lk, later_in_block)
        blk = jnp.sum(lk, axis=-1)
        later_blocks = lax.cumsum(blk, axis=3, reverse=True) - blk
        later = (within + later_blocks[..., None]).reshape(B, H, Q, L)
        w = jnp.exp(jnp.where(strict, z + log_keep + later, NEG_INF))
        return jnp.einsum('bhqk,bkhd->bqhd', w.astype(vv.dtype), vv)

    return causal_block_sweep(block, q, k, v)


def chunk_causal_softmax_attention(q, k, v, scale):
    r = jnp.arange(Q_BLOCK, dtype=jnp.int32)

    def block(qb, kk, vv, i):
        L = kk.shape[1]
        q_chunk = (i * Q_BLOCK + r) // CHUNK
        allowed = (jnp.arange(L, dtype=jnp.int32) // CHUNK)[None, :] <= q_chunk[:, None]
        s = jnp.einsum('bqhd,bkhd->bhqk', qb, kk).astype(jnp.float32) * scale
        p = jax.nn.softmax(jnp.where(allowed, s, NEG_INF), axis=-1)
        return jnp.einsum('bhqk,bkhd->bqhd', p.astype(vv.dtype), vv)

    return causal_block_sweep(block, q, k, v)


def conformer_conv_branch(u, dw, b, ln_g, ln_b):
    a, g = jnp.split(u, 2, axis=-1)
    y = a * jax.nn.sigmoid(g)
    y = lax.conv_general_dilated(
        y, dw[:, None, :].astype(y.dtype), window_strides=(1,),
        padding=[(CONV_WIDTH - 1, 0)], dimension_numbers=('NWC', 'WIO', 'NWC'),
        feature_group_count=CONV_CH) + b
    return jax.nn.silu(layer_norm(y, ln_g, ln_b))


def stick_breaking_branch(qkv, q_hnorm, k_hnorm):
    B, S = qkv.shape[:2]
    qkv = qkv.reshape(B, S, 3, SB_HEADS, SB_HEAD_DIM)
    q = rms_norm(qkv[:, :, 0], q_hnorm)
    k = rms_norm(qkv[:, :, 1], k_hnorm)
    o = stick_breaking_attention(q, k, qkv[:, :, 2])
    return o.reshape(B, S, SB_HEADS * SB_HEAD_DIM)


def mla_branch(q_lat, kv_lat, k_rope, cos, sin, q_norm, w_uq, kv_norm, w_ukv, q_hnorm, k_hnorm):
    B, S = q_lat.shape[:2]
    q = (rms_norm(q_lat, q_norm) @ w_uq).reshape(B, S, MLA_HEADS, MLA_NOPE + MLA_ROPE)
    kv = (rms_norm(kv_lat, kv_norm) @ w_ukv).reshape(B, S, MLA_HEADS, MLA_NOPE + MLA_V)
    k_nope, v = kv[..., :MLA_NOPE], kv[..., MLA_NOPE:]
    k_r = jnp.broadcast_to(k_rope[:, :, None, :], (B, S, MLA_HEADS, MLA_ROPE))
    k = jnp.concatenate([k_nope, k_r], axis=-1)
    q = rope_tail(rms_norm(q, q_hnorm), cos, sin)
    k = rope_tail(rms_norm(k, k_hnorm), cos, sin)
    o = chunk_causal_softmax_attention(q, k, v, (MLA_NOPE + MLA_ROPE) ** -0.5)
    return o.reshape(B, S, MLA_HEADS * MLA_V)


def memory_branch(q_raw, mem, mem_norm, w_kv, q_hnorm, k_hnorm):
    B, S = q_raw.shape[:2]
    M = mem.shape[1]
    q = rms_norm(q_raw.reshape(B, S, MEM_HEADS, MEM_HEAD_DIM), q_hnorm)
    kv = (rms_norm(mem, mem_norm) @ w_kv).reshape(B, M, 2, MEM_HEADS, MEM_HEAD_DIM)
    k = rms_norm(kv[:, :, 0], k_hnorm)
    v = kv[:, :, 1]
    s = jnp.einsum('bshd,bmhd->bhsm', q, k).astype(jnp.float32) * (MEM_HEAD_DIM ** -0.5)
    p = jax.nn.softmax(s, axis=-1).astype(v.dtype)
    o = jnp.einsum('bhsm,bmhd->bshd', p, v)
    return o.reshape(B, S, MEM_HEADS * MEM_HEAD_DIM)


def _fwd_setup_inputs(seed: int = 0) -> dict:
    key = jax.random.key(seed)
    ks = jax.random.split(key, 32)
    L, D, F = DEPTH, D_MODEL, FFN_HIDDEN

    def w(k, shape, fan_in):
        return jax.random.normal(k, shape, jnp.float32) * (fan_in ** -0.5)

    def gain(k, shape):
        return 1.0 + 0.02 * jax.random.normal(k, shape, jnp.float32)

    offsets = jax.random.randint(ks[2], (BATCH, 1), 0, 4096, dtype=jnp.int32)
    positions = offsets + jnp.arange(SEQ, dtype=jnp.int32)[None, :]
    return {
        "x": jax.random.normal(ks[0], (BATCH, SEQ, D), jnp.float32),
        "mem": jax.random.normal(ks[1], (BATCH, MEM_LEN, D), jnp.float32),
        "positions": positions,
        "ffn1_norm": gain(ks[3], (L, D)),
        "ffn1_w_in": w(ks[4], (L, D, 2 * F), D),
        "ffn1_w_out": w(ks[5], (L, F, D), F),
        "mix_norm": gain(ks[6], (L, D)),
        "w_in": w(ks[7], (L, D, IN_WIDTH), D),
        "conv_dw": w(ks[8], (L, CONV_WIDTH, CONV_CH), CONV_WIDTH),
        "conv_b": 0.02 * jax.random.normal(ks[9], (L, CONV_CH), jnp.float32),
        "conv_ln_g": gain(ks[10], (L, CONV_CH)),
        "conv_ln_b": 0.02 * jax.random.normal(ks[11], (L, CONV_CH), jnp.float32),
        "sb_q_hnorm": gain(ks[12], (L, SB_HEAD_DIM)),
        "sb_k_hnorm": gain(ks[13], (L, SB_HEAD_DIM)),
        "mla_q_norm": gain(ks[14], (L, MLA_Q_LORA)),
        "mla_w_uq": w(ks[15], (L, MLA_Q_LORA, MLA_HEADS * (MLA_NOPE + MLA_ROPE)), MLA_Q_LORA),
        "mla_kv_norm": gain(ks[16], (L, MLA_KV_LORA)),
        "mla_w_ukv": w(ks[17], (L, MLA_KV_LORA, MLA_HEADS * (MLA_NOPE + MLA_V)), MLA_KV_LORA),
        "mla_q_hnorm": gain(ks[18], (L, MLA_NOPE + MLA_ROPE)),
        "mla_k_hnorm": gain(ks[19], (L, MLA_NOPE + MLA_ROPE)),
        "mem_norm": gain(ks[20], (L, D)),
        "mem_w_kv": w(ks[21], (L, D, 2 * MEM_HEADS * MEM_HEAD_DIM), D),
        "mem_q_hnorm": gain(ks[22], (L, MEM_HEAD_DIM)),
        "mem_k_hnorm": gain(ks[23], (L, MEM_HEAD_DIM)),
        "w_branch": w(ks[24], (L, N_BRANCH, BRANCH_WIDTH, D), BRANCH_WIDTH),
        "w_out": w(ks[25], (L, D, D), D),
        "ffn2_norm": gain(ks[26], (L, D)),
        "ffn2_w_in": w(ks[27], (L, D, 2 * F), D),
        "ffn2_w_out": w(ks[28], (L, F, D), F),
    }


def _fwd_reference(x, mem, positions, ffn1_norm, ffn1_w_in, ffn1_w_out, mix_norm, w_in,
              conv_dw, conv_b, conv_ln_g, conv_ln_b, sb_q_hnorm, sb_k_hnorm,
              mla_q_norm, mla_w_uq, mla_kv_norm, mla_w_ukv, mla_q_hnorm, mla_k_hnorm,
              mem_norm, mem_w_kv, mem_q_hnorm, mem_k_hnorm, w_branch, w_out,
              ffn2_norm, ffn2_w_in, ffn2_w_out):
    B, S, D = x.shape
    cos, sin = rope_tables(positions)
    split_at = np.cumsum(IN_SPLITS)[:-1].tolist()
    for l in range(DEPTH):
        x = x + 0.5 * swiglu_ffn(rms_norm(x, ffn1_norm[l]), ffn1_w_in[l], ffn1_w_out[l])
        h = rms_norm(x, mix_norm[l])
        u = h @ w_in[l]
        conv_u, sb_qkv, q_lat, kv_lat, k_rope, mem_q, gate_logits = jnp.split(u, split_at, axis=-1)
        branches = (
            conformer_conv_branch(conv_u, conv_dw[l], conv_b[l], conv_ln_g[l], conv_ln_b[l]),
            stick_breaking_branch(sb_qkv, sb_q_hnorm[l], sb_k_hnorm[l]),
            mla_branch(q_lat, kv_lat, k_rope, cos, sin, mla_q_norm[l], mla_w_uq[l],
                       mla_kv_norm[l], mla_w_ukv[l], mla_q_hnorm[l], mla_k_hnorm[l]),
            memory_branch(mem_q, mem, mem_norm[l], mem_w_kv[l], mem_q_hnorm[l], mem_k_hnorm[l]),
        )
        gates = jax.nn.sigmoid(gate_logits).reshape(B, S, N_BRANCH, D)
        merged = gates[:, :, 0, :] * (branches[0] @ w_branch[l, 0])
        for i in range(1, N_BRANCH):
            merged = merged + gates[:, :, i, :] * (branches[i] @ w_branch[l, i])
        x = x + merged @ w_out[l]
        x = x + 0.5 * swiglu_ffn(rms_norm(x, ffn2_norm[l]), ffn2_w_in[l], ffn2_w_out[l])
    return x


import jax as _jax
import jax.numpy as _jnp

TWIN_FORMAT = 'train_step'
FWD_PARAMS = ['x', 'mem', 'positions', 'ffn1_norm', 'ffn1_w_in', 'ffn1_w_out', 'mix_norm', 'w_in', 'conv_dw', 'conv_b', 'conv_ln_g', 'conv_ln_b', 'sb_q_hnorm', 'sb_k_hnorm', 'mla_q_norm', 'mla_w_uq', 'mla_kv_norm', 'mla_w_ukv', 'mla_q_hnorm', 'mla_k_hnorm', 'mem_norm', 'mem_w_kv', 'mem_q_hnorm', 'mem_k_hnorm', 'w_branch', 'w_out', 'ffn2_norm', 'ffn2_w_in', 'ffn2_w_out']
TWIN_WEIGHTS = ['ffn1_norm', 'ffn1_w_in', 'ffn1_w_out', 'mix_norm', 'w_in', 'conv_dw', 'conv_b', 'conv_ln_g', 'conv_ln_b', 'sb_q_hnorm', 'sb_k_hnorm', 'mla_q_norm', 'mla_w_uq', 'mla_kv_norm', 'mla_w_ukv', 'mla_q_hnorm', 'mla_k_hnorm', 'mem_norm', 'mem_w_kv', 'mem_q_hnorm', 'mem_k_hnorm', 'w_branch', 'w_out', 'ffn2_norm', 'ffn2_w_in', 'ffn2_w_out']
TWIN_DIFF_INPUT = 'x'
TWIN_INPUTS = ['x', 'mem', 'positions', 'ffn1_norm', 'ffn1_w_in', 'ffn1_w_out', 'mix_norm', 'w_in', 'conv_dw', 'conv_b', 'conv_ln_g', 'conv_ln_b', 'sb_q_hnorm', 'sb_k_hnorm', 'mla_q_norm', 'mla_w_uq', 'mla_kv_norm', 'mla_w_ukv', 'mla_q_hnorm', 'mla_k_hnorm', 'mem_norm', 'mem_w_kv', 'mem_q_hnorm', 'mem_k_hnorm', 'w_branch', 'w_out', 'ffn2_norm', 'ffn2_w_in', 'ffn2_w_out', 'loss_target', 'm_ffn1_norm', 'm_ffn1_w_in', 'm_ffn1_w_out', 'm_mix_norm', 'm_w_in', 'm_conv_dw', 'm_conv_b', 'm_conv_ln_g', 'm_conv_ln_b', 'm_sb_q_hnorm', 'm_sb_k_hnorm', 'm_mla_q_norm', 'm_mla_w_uq', 'm_mla_kv_norm', 'm_mla_w_ukv', 'm_mla_q_hnorm', 'm_mla_k_hnorm', 'm_mem_norm', 'm_mem_w_kv', 'm_mem_q_hnorm', 'm_mem_k_hnorm', 'm_w_branch', 'm_w_out', 'm_ffn2_norm', 'm_ffn2_w_in', 'm_ffn2_w_out', 'v_ffn1_norm', 'v_ffn1_w_in', 'v_ffn1_w_out', 'v_mix_norm', 'v_w_in', 'v_conv_dw', 'v_conv_b', 'v_conv_ln_g', 'v_conv_ln_b', 'v_sb_q_hnorm', 'v_sb_k_hnorm', 'v_mla_q_norm', 'v_mla_w_uq', 'v_mla_kv_norm', 'v_mla_w_ukv', 'v_mla_q_hnorm', 'v_mla_k_hnorm', 'v_mem_norm', 'v_mem_w_kv', 'v_mem_q_hnorm', 'v_mem_k_hnorm', 'v_w_branch', 'v_w_out', 'v_ffn2_norm', 'v_ffn2_w_in', 'v_ffn2_w_out']
TWIN_OUTPUTS = ['loss', 'grad_x', 'grad_ffn1_norm', 'grad_ffn1_w_in', 'grad_ffn1_w_out', 'grad_mix_norm', 'grad_w_in', 'grad_conv_dw', 'grad_conv_b', 'grad_conv_ln_g', 'grad_conv_ln_b', 'grad_sb_q_hnorm', 'grad_sb_k_hnorm', 'grad_mla_q_norm', 'grad_mla_w_uq', 'grad_mla_kv_norm', 'grad_mla_w_ukv', 'grad_mla_q_hnorm', 'grad_mla_k_hnorm', 'grad_mem_norm', 'grad_mem_w_kv', 'grad_mem_q_hnorm', 'grad_mem_k_hnorm', 'grad_w_branch', 'grad_w_out', 'grad_ffn2_norm', 'grad_ffn2_w_in', 'grad_ffn2_w_out', 'delta_ffn1_norm', 'delta_ffn1_w_in', 'delta_ffn1_w_out', 'delta_mix_norm', 'delta_w_in', 'delta_conv_dw', 'delta_conv_b', 'delta_conv_ln_g', 'delta_conv_ln_b', 'delta_sb_q_hnorm', 'delta_sb_k_hnorm', 'delta_mla_q_norm', 'delta_mla_w_uq', 'delta_mla_kv_norm', 'delta_mla_w_ukv', 'delta_mla_q_hnorm', 'delta_mla_k_hnorm', 'delta_mem_norm', 'delta_mem_w_kv', 'delta_mem_q_hnorm', 'delta_mem_k_hnorm', 'delta_w_branch', 'delta_w_out', 'delta_ffn2_norm', 'delta_ffn2_w_in', 'delta_ffn2_w_out', 'new_m_ffn1_norm', 'new_m_ffn1_w_in', 'new_m_ffn1_w_out', 'new_m_mix_norm', 'new_m_w_in', 'new_m_conv_dw', 'new_m_conv_b', 'new_m_conv_ln_g', 'new_m_conv_ln_b', 'new_m_sb_q_hnorm', 'new_m_sb_k_hnorm', 'new_m_mla_q_norm', 'new_m_mla_w_uq', 'new_m_mla_kv_norm', 'new_m_mla_w_ukv', 'new_m_mla_q_hnorm', 'new_m_mla_k_hnorm', 'new_m_mem_norm', 'new_m_mem_w_kv', 'new_m_mem_q_hnorm', 'new_m_mem_k_hnorm', 'new_m_w_branch', 'new_m_w_out', 'new_m_ffn2_norm', 'new_m_ffn2_w_in', 'new_m_ffn2_w_out', 'new_v_ffn1_norm', 'new_v_ffn1_w_in', 'new_v_ffn1_w_out', 'new_v_mix_norm', 'new_v_w_in', 'new_v_conv_dw', 'new_v_conv_b', 'new_v_conv_ln_g', 'new_v_conv_ln_b', 'new_v_sb_q_hnorm', 'new_v_sb_k_hnorm', 'new_v_mla_q_norm', 'new_v_mla_w_uq', 'new_v_mla_kv_norm', 'new_v_mla_w_ukv', 'new_v_mla_q_hnorm', 'new_v_mla_k_hnorm', 'new_v_mem_norm', 'new_v_mem_w_kv', 'new_v_mem_q_hnorm', 'new_v_mem_k_hnorm', 'new_v_w_branch', 'new_v_w_out', 'new_v_ffn2_norm', 'new_v_ffn2_w_in', 'new_v_ffn2_w_out']
TWIN_LEAF_KINDS = {'loss': 'loss', 'grad_x': 'grad_x', 'grad_ffn1_norm': 'grad_w', 'grad_ffn1_w_in': 'grad_w', 'grad_ffn1_w_out': 'grad_w', 'grad_mix_norm': 'grad_w', 'grad_w_in': 'grad_w', 'grad_conv_dw': 'grad_w', 'grad_conv_b': 'grad_w', 'grad_conv_ln_g': 'grad_w', 'grad_conv_ln_b': 'grad_w', 'grad_sb_q_hnorm': 'grad_w', 'grad_sb_k_hnorm': 'grad_w', 'grad_mla_q_norm': 'grad_w', 'grad_mla_w_uq': 'grad_w', 'grad_mla_kv_norm': 'grad_w', 'grad_mla_w_ukv': 'grad_w', 'grad_mla_q_hnorm': 'grad_w', 'grad_mla_k_hnorm': 'grad_w', 'grad_mem_norm': 'grad_w', 'grad_mem_w_kv': 'grad_w', 'grad_mem_q_hnorm': 'grad_w', 'grad_mem_k_hnorm': 'grad_w', 'grad_w_branch': 'grad_w', 'grad_w_out': 'grad_w', 'grad_ffn2_norm': 'grad_w', 'grad_ffn2_w_in': 'grad_w', 'grad_ffn2_w_out': 'grad_w', 'delta_ffn1_norm': 'delta_w', 'delta_ffn1_w_in': 'delta_w', 'delta_ffn1_w_out': 'delta_w', 'delta_mix_norm': 'delta_w', 'delta_w_in': 'delta_w', 'delta_conv_dw': 'delta_w', 'delta_conv_b': 'delta_w', 'delta_conv_ln_g': 'delta_w', 'delta_conv_ln_b': 'delta_w', 'delta_sb_q_hnorm': 'delta_w', 'delta_sb_k_hnorm': 'delta_w', 'delta_mla_q_norm': 'delta_w', 'delta_mla_w_uq': 'delta_w', 'delta_mla_kv_norm': 'delta_w', 'delta_mla_w_ukv': 'delta_w', 'delta_mla_q_hnorm': 'delta_w', 'delta_mla_k_hnorm': 'delta_w', 'delta_mem_norm': 'delta_w', 'delta_mem_w_kv': 'delta_w', 'delta_mem_q_hnorm': 'delta_w', 'delta_mem_k_hnorm': 'delta_w', 'delta_w_branch': 'delta_w', 'delta_w_out': 'delta_w', 'delta_ffn2_norm': 'delta_w', 'delta_ffn2_w_in': 'delta_w', 'delta_ffn2_w_out': 'delta_w', 'new_m_ffn1_norm': 'new_m', 'new_m_ffn1_w_in': 'new_m', 'new_m_ffn1_w_out': 'new_m', 'new_m_mix_norm': 'new_m', 'new_m_w_in': 'new_m', 'new_m_conv_dw': 'new_m', 'new_m_conv_b': 'new_m', 'new_m_conv_ln_g': 'new_m', 'new_m_conv_ln_b': 'new_m', 'new_m_sb_q_hnorm': 'new_m', 'new_m_sb_k_hnorm': 'new_m', 'new_m_mla_q_norm': 'new_m', 'new_m_mla_w_uq': 'new_m', 'new_m_mla_kv_norm': 'new_m', 'new_m_mla_w_ukv': 'new_m', 'new_m_mla_q_hnorm': 'new_m', 'new_m_mla_k_hnorm': 'new_m', 'new_m_mem_norm': 'new_m', 'new_m_mem_w_kv': 'new_m', 'new_m_mem_q_hnorm': 'new_m', 'new_m_mem_k_hnorm': 'new_m', 'new_m_w_branch': 'new_m', 'new_m_w_out': 'new_m', 'new_m_ffn2_norm': 'new_m', 'new_m_ffn2_w_in': 'new_m', 'new_m_ffn2_w_out': 'new_m', 'new_v_ffn1_norm': 'new_v', 'new_v_ffn1_w_in': 'new_v', 'new_v_ffn1_w_out': 'new_v', 'new_v_mix_norm': 'new_v', 'new_v_w_in': 'new_v', 'new_v_conv_dw': 'new_v', 'new_v_conv_b': 'new_v', 'new_v_conv_ln_g': 'new_v', 'new_v_conv_ln_b': 'new_v', 'new_v_sb_q_hnorm': 'new_v', 'new_v_sb_k_hnorm': 'new_v', 'new_v_mla_q_norm': 'new_v', 'new_v_mla_w_uq': 'new_v', 'new_v_mla_kv_norm': 'new_v', 'new_v_mla_w_ukv': 'new_v', 'new_v_mla_q_hnorm': 'new_v', 'new_v_mla_k_hnorm': 'new_v', 'new_v_mem_norm': 'new_v', 'new_v_mem_w_kv': 'new_v', 'new_v_mem_q_hnorm': 'new_v', 'new_v_mem_k_hnorm': 'new_v', 'new_v_w_branch': 'new_v', 'new_v_w_out': 'new_v', 'new_v_ffn2_norm': 'new_v', 'new_v_ffn2_w_in': 'new_v', 'new_v_ffn2_w_out': 'new_v'}


def _forward(args):
    return _fwd_reference(*[args[k] for k in FWD_PARAMS])


def _output_shape():
    def fwd():
        inp = _fwd_setup_inputs(0)
        return _fwd_reference(*[inp[k] for k in FWD_PARAMS])
    out = _jax.eval_shape(fwd)
    return out.shape, out.dtype

N_MICROBATCH = 1
ADAM_LR = 0.001
ADAM_B1 = 0.9
ADAM_B2 = 0.999
ADAM_EPS = 1e-08
ADAM_WD = 0.01
ADAM_STEP = 10
PER_EXAMPLE_BATCH_AXIS = {'x': 0, 'mem': 0, 'positions': 0, 'loss_target': 0}
SHARED_INPUTS = []
_WEIGHT_DTYPES = {'ffn1_norm': _jnp.float32, 'ffn1_w_in': _jnp.float32, 'ffn1_w_out': _jnp.float32, 'mix_norm': _jnp.float32, 'w_in': _jnp.float32, 'conv_dw': _jnp.float32, 'conv_b': _jnp.float32, 'conv_ln_g': _jnp.float32, 'conv_ln_b': _jnp.float32, 'sb_q_hnorm': _jnp.float32, 'sb_k_hnorm': _jnp.float32, 'mla_q_norm': _jnp.float32, 'mla_w_uq': _jnp.float32, 'mla_kv_norm': _jnp.float32, 'mla_w_ukv': _jnp.float32, 'mla_q_hnorm': _jnp.float32, 'mla_k_hnorm': _jnp.float32, 'mem_norm': _jnp.float32, 'mem_w_kv': _jnp.float32, 'mem_q_hnorm': _jnp.float32, 'mem_k_hnorm': _jnp.float32, 'w_branch': _jnp.float32, 'w_out': _jnp.float32, 'ffn2_norm': _jnp.float32, 'ffn2_w_in': _jnp.float32, 'ffn2_w_out': _jnp.float32}
MOMENT_SCALE = {'ffn1_norm': 1.204190e+01, 'ffn1_w_in': 2.879103e-01, 'ffn1_w_out': 4.211439e-01, 'mix_norm': 9.957001e+00, 'w_in': 6.493328e-01, 'conv_dw': 1.328285e+00, 'conv_b': 2.045982e+01, 'conv_ln_g': 1.965351e+01, 'conv_ln_b': 1.641585e+01, 'sb_q_hnorm': 8.641224e+00, 'sb_k_hnorm': 8.665299e+00, 'mla_q_norm': 1.298919e-01, 'mla_w_uq': 7.490700e-02, 'mla_kv_norm': 3.241484e+00, 'mla_w_ukv': 1.338966e+00, 'mla_q_hnorm': 4.148209e-01, 'mla_k_hnorm': 4.179308e-01, 'mem_norm': 3.605173e-01, 'mem_w_kv': 3.009482e-01, 'mem_q_hnorm': 1.416389e+00, 'mem_k_hnorm': 1.414906e+00, 'w_branch': 1.833060e+00, 'w_out': 3.289124e+00, 'ffn2_norm': 1.216888e+01, 'ffn2_w_in': 2.978129e-01, 'ffn2_w_out': 4.207772e-01}


def _to_microbatches(a, axis):
    t = _jnp.moveaxis(a, axis, 0)
    t = t.reshape((N_MICROBATCH, t.shape[0] // N_MICROBATCH) + t.shape[1:])
    return _jnp.moveaxis(t, 1, axis + 1)


def setup_inputs(seed: int = 0) -> dict:
    inp = _fwd_setup_inputs(seed)
    key = _jax.random.fold_in(_jax.random.key(seed), 7919)
    shape, _ = _output_shape()
    out = dict(inp)
    out["loss_target"] = _jax.random.normal(_jax.random.fold_in(key, 0), shape, _jnp.float32)
    for i, name in enumerate(TWIN_WEIGHTS):
        w = inp[name].astype(_jnp.float32)
        if MOMENT_SCALE is None:
            s = _jnp.sqrt(_jnp.mean(_jnp.square(w)) + 1e-30)
        else:
            s = MOMENT_SCALE[name]
        km, kv = _jax.random.split(_jax.random.fold_in(key, i + 1))
        out[name] = w
        out["m_" + name] = s * _jax.random.normal(km, w.shape, _jnp.float32)
        out["v_" + name] = (s * s) * _jax.random.uniform(kv, w.shape, _jnp.float32, 0.5, 1.5)
    if N_MICROBATCH > 1:
        for name, axis in PER_EXAMPLE_BATCH_AXIS.items():
            out[name] = _to_microbatches(out[name], axis)
    return {'x': out['x'], 'mem': out['mem'], 'positions': out['positions'], 'ffn1_norm': out['ffn1_norm'], 'ffn1_w_in': out['ffn1_w_in'], 'ffn1_w_out': out['ffn1_w_out'], 'mix_norm': out['mix_norm'], 'w_in': out['w_in'], 'conv_dw': out['conv_dw'], 'conv_b': out['conv_b'], 'conv_ln_g': out['conv_ln_g'], 'conv_ln_b': out['conv_ln_b'], 'sb_q_hnorm': out['sb_q_hnorm'], 'sb_k_hnorm': out['sb_k_hnorm'], 'mla_q_norm': out['mla_q_norm'], 'mla_w_uq': out['mla_w_uq'], 'mla_kv_norm': out['mla_kv_norm'], 'mla_w_ukv': out['mla_w_ukv'], 'mla_q_hnorm': out['mla_q_hnorm'], 'mla_k_hnorm': out['mla_k_hnorm'], 'mem_norm': out['mem_norm'], 'mem_w_kv': out['mem_w_kv'], 'mem_q_hnorm': out['mem_q_hnorm'], 'mem_k_hnorm': out['mem_k_hnorm'], 'w_branch': out['w_branch'], 'w_out': out['w_out'], 'ffn2_norm': out['ffn2_norm'], 'ffn2_w_in': out['ffn2_w_in'], 'ffn2_w_out': out['ffn2_w_out'], 'loss_target': out['loss_target'], 'm_ffn1_norm': out['m_ffn1_norm'], 'm_ffn1_w_in': out['m_ffn1_w_in'], 'm_ffn1_w_out': out['m_ffn1_w_out'], 'm_mix_norm': out['m_mix_norm'], 'm_w_in': out['m_w_in'], 'm_conv_dw': out['m_conv_dw'], 'm_conv_b': out['m_conv_b'], 'm_conv_ln_g': out['m_conv_ln_g'], 'm_conv_ln_b': out['m_conv_ln_b'], 'm_sb_q_hnorm': out['m_sb_q_hnorm'], 'm_sb_k_hnorm': out['m_sb_k_hnorm'], 'm_mla_q_norm': out['m_mla_q_norm'], 'm_mla_w_uq': out['m_mla_w_uq'], 'm_mla_kv_norm': out['m_mla_kv_norm'], 'm_mla_w_ukv': out['m_mla_w_ukv'], 'm_mla_q_hnorm': out['m_mla_q_hnorm'], 'm_mla_k_hnorm': out['m_mla_k_hnorm'], 'm_mem_norm': out['m_mem_norm'], 'm_mem_w_kv': out['m_mem_w_kv'], 'm_mem_q_hnorm': out['m_mem_q_hnorm'], 'm_mem_k_hnorm': out['m_mem_k_hnorm'], 'm_w_branch': out['m_w_branch'], 'm_w_out': out['m_w_out'], 'm_ffn2_norm': out['m_ffn2_norm'], 'm_ffn2_w_in': out['m_ffn2_w_in'], 'm_ffn2_w_out': out['m_ffn2_w_out'], 'v_ffn1_norm': out['v_ffn1_norm'], 'v_ffn1_w_in': out['v_ffn1_w_in'], 'v_ffn1_w_out': out['v_ffn1_w_out'], 'v_mix_norm': out['v_mix_norm'], 'v_w_in': out['v_w_in'], 'v_conv_dw': out['v_conv_dw'], 'v_conv_b': out['v_conv_b'], 'v_conv_ln_g': out['v_conv_ln_g'], 'v_conv_ln_b': out['v_conv_ln_b'], 'v_sb_q_hnorm': out['v_sb_q_hnorm'], 'v_sb_k_hnorm': out['v_sb_k_hnorm'], 'v_mla_q_norm': out['v_mla_q_norm'], 'v_mla_w_uq': out['v_mla_w_uq'], 'v_mla_kv_norm': out['v_mla_kv_norm'], 'v_mla_w_ukv': out['v_mla_w_ukv'], 'v_mla_q_hnorm': out['v_mla_q_hnorm'], 'v_mla_k_hnorm': out['v_mla_k_hnorm'], 'v_mem_norm': out['v_mem_norm'], 'v_mem_w_kv': out['v_mem_w_kv'], 'v_mem_q_hnorm': out['v_mem_q_hnorm'], 'v_mem_k_hnorm': out['v_mem_k_hnorm'], 'v_w_branch': out['v_w_branch'], 'v_w_out': out['v_w_out'], 'v_ffn2_norm': out['v_ffn2_norm'], 'v_ffn2_w_in': out['v_ffn2_w_in'], 'v_ffn2_w_out': out['v_ffn2_w_out']}


def _loss(weights, diff, rest, loss_target):
    with _jax.named_scope("forward"):
        args = {**rest, TWIN_DIFF_INPUT: diff, **{k: w.astype(_WEIGHT_DTYPES[k]) for k, w in weights.items()}}
        y = _forward(args)
    with _jax.named_scope("loss_head"):
        err = _jnp.square(y.astype(_jnp.float32) - loss_target)
        return 0.5 * _jnp.sum(_jnp.mean(err, axis=-1)) if err.ndim else 0.5 * err


def _adamw(w, g, m, v):
    m = ADAM_B1 * m + (1.0 - ADAM_B1) * g
    v = ADAM_B2 * v + (1.0 - ADAM_B2) * _jnp.square(g)
    m_hat = m / (1.0 - ADAM_B1 ** ADAM_STEP)
    v_hat = v / (1.0 - ADAM_B2 ** ADAM_STEP)
    delta = -ADAM_LR * (m_hat / (_jnp.sqrt(v_hat) + ADAM_EPS) + ADAM_WD * w)
    return delta, m, v


def reference(x, mem, positions, ffn1_norm, ffn1_w_in, ffn1_w_out, mix_norm, w_in, conv_dw, conv_b, conv_ln_g, conv_ln_b, sb_q_hnorm, sb_k_hnorm, mla_q_norm, mla_w_uq, mla_kv_norm, mla_w_ukv, mla_q_hnorm, mla_k_hnorm, mem_norm, mem_w_kv, mem_q_hnorm, mem_k_hnorm, w_branch, w_out, ffn2_norm, ffn2_w_in, ffn2_w_out, loss_target, m_ffn1_norm, m_ffn1_w_in, m_ffn1_w_out, m_mix_norm, m_w_in, m_conv_dw, m_conv_b, m_conv_ln_g, m_conv_ln_b, m_sb_q_hnorm, m_sb_k_hnorm, m_mla_q_norm, m_mla_w_uq, m_mla_kv_norm, m_mla_w_ukv, m_mla_q_hnorm, m_mla_k_hnorm, m_mem_norm, m_mem_w_kv, m_mem_q_hnorm, m_mem_k_hnorm, m_w_branch, m_w_out, m_ffn2_norm, m_ffn2_w_in, m_ffn2_w_out, v_ffn1_norm, v_ffn1_w_in, v_ffn1_w_out, v_mix_norm, v_w_in, v_conv_dw, v_conv_b, v_conv_ln_g, v_conv_ln_b, v_sb_q_hnorm, v_sb_k_hnorm, v_mla_q_norm, v_mla_w_uq, v_mla_kv_norm, v_mla_w_ukv, v_mla_q_hnorm, v_mla_k_hnorm, v_mem_norm, v_mem_w_kv, v_mem_q_hnorm, v_mem_k_hnorm, v_w_branch, v_w_out, v_ffn2_norm, v_ffn2_w_in, v_ffn2_w_out):
    given = dict(x=x, mem=mem, positions=positions, ffn1_norm=ffn1_norm, ffn1_w_in=ffn1_w_in, ffn1_w_out=ffn1_w_out, mix_norm=mix_norm, w_in=w_in, conv_dw=conv_dw, conv_b=conv_b, conv_ln_g=conv_ln_g, conv_ln_b=conv_ln_b, sb_q_hnorm=sb_q_hnorm, sb_k_hnorm=sb_k_hnorm, mla_q_norm=mla_q_norm, mla_w_uq=mla_w_uq, mla_kv_norm=mla_kv_norm, mla_w_ukv=mla_w_ukv, mla_q_hnorm=mla_q_hnorm, mla_k_hnorm=mla_k_hnorm, mem_norm=mem_norm, mem_w_kv=mem_w_kv, mem_q_hnorm=mem_q_hnorm, mem_k_hnorm=mem_k_hnorm, w_branch=w_branch, w_out=w_out, ffn2_norm=ffn2_norm, ffn2_w_in=ffn2_w_in, ffn2_w_out=ffn2_w_out, loss_target=loss_target, m_ffn1_norm=m_ffn1_norm, m_ffn1_w_in=m_ffn1_w_in, m_ffn1_w_out=m_ffn1_w_out, m_mix_norm=m_mix_norm, m_w_in=m_w_in, m_conv_dw=m_conv_dw, m_conv_b=m_conv_b, m_conv_ln_g=m_conv_ln_g, m_conv_ln_b=m_conv_ln_b, m_sb_q_hnorm=m_sb_q_hnorm, m_sb_k_hnorm=m_sb_k_hnorm, m_mla_q_norm=m_mla_q_norm, m_mla_w_uq=m_mla_w_uq, m_mla_kv_norm=m_mla_kv_norm, m_mla_w_ukv=m_mla_w_ukv, m_mla_q_hnorm=m_mla_q_hnorm, m_mla_k_hnorm=m_mla_k_hnorm, m_mem_norm=m_mem_norm, m_mem_w_kv=m_mem_w_kv, m_mem_q_hnorm=m_mem_q_hnorm, m_mem_k_hnorm=m_mem_k_hnorm, m_w_branch=m_w_branch, m_w_out=m_w_out, m_ffn2_norm=m_ffn2_norm, m_ffn2_w_in=m_ffn2_w_in, m_ffn2_w_out=m_ffn2_w_out, v_ffn1_norm=v_ffn1_norm, v_ffn1_w_in=v_ffn1_w_in, v_ffn1_w_out=v_ffn1_w_out, v_mix_norm=v_mix_norm, v_w_in=v_w_in, v_conv_dw=v_conv_dw, v_conv_b=v_conv_b, v_conv_ln_g=v_conv_ln_g, v_conv_ln_b=v_conv_ln_b, v_sb_q_hnorm=v_sb_q_hnorm, v_sb_k_hnorm=v_sb_k_hnorm, v_mla_q_norm=v_mla_q_norm, v_mla_w_uq=v_mla_w_uq, v_mla_kv_norm=v_mla_kv_norm, v_mla_w_ukv=v_mla_w_ukv, v_mla_q_hnorm=v_mla_q_hnorm, v_mla_k_hnorm=v_mla_k_hnorm, v_mem_norm=v_mem_norm, v_mem_w_kv=v_mem_w_kv, v_mem_q_hnorm=v_mem_q_hnorm, v_mem_k_hnorm=v_mem_k_hnorm, v_w_branch=v_w_branch, v_w_out=v_w_out, v_ffn2_norm=v_ffn2_norm, v_ffn2_w_in=v_ffn2_w_in, v_ffn2_w_out=v_ffn2_w_out)
    weights = {n: given[n] for n in TWIN_WEIGHTS}
    shared = {n: given[n] for n in SHARED_INPUTS}
    per_example = {n: given[n] for n in ['x', 'mem', 'positions']}
    grad_fn = _jax.value_and_grad(_loss, argnums=(0, 1))

    def one_microbatch(ex, loss_target):
        ex = dict(ex)
        diff = ex.pop(TWIN_DIFF_INPUT)
        return grad_fn(weights, diff, {**shared, **ex}, loss_target)

    if N_MICROBATCH == 1:
        loss, (grad_w, grad_x) = one_microbatch(per_example, given["loss_target"])
    else:
        def body(carry, xs):
            loss_sum, grad_sum = carry
            l_k, (gw_k, gx_k) = one_microbatch(xs[0], xs[1])
            with _jax.named_scope("update"):
                return (loss_sum + l_k, _jax.tree.map(_jnp.add, grad_sum, gw_k)), gx_k

        init = (_jnp.zeros((), _jnp.float32), _jax.tree.map(_jnp.zeros_like, weights))
        (loss, grad_w), grad_x = _jax.lax.scan(body, init, (per_example, given["loss_target"]))
    with _jax.named_scope("update"):
        delta_w, new_m, new_v = {}, {}, {}
        for n in TWIN_WEIGHTS:
            delta_w[n], new_m[n], new_v[n] = _adamw(weights[n], grad_w[n], given["m_" + n], given["v_" + n])
    return (loss, grad_x, *[grad_w[n] for n in TWIN_WEIGHTS], *[delta_w[n] for n in TWIN_WEIGHTS],
            *[new_m[n] for n in TWIN_WEIGHTS], *[new_v[n] for n in TWIN_WEIGHTS])
```

```python
import math

import jax
import jax.numpy as jnp
from jax import lax
from jax.experimental import pallas as pl
from jax.experimental.pallas import tpu as pltpu

f32, bf16 = jnp.float32, jnp.bfloat16

D_MODEL = 1024
DEPTH = 4
CHUNK = 64
FFN_HIDDEN = 2048
CONV_CH = 512
CONV_WIDTH = 31
HEADS = 4
HEAD_DIM = 128
MLA_NOPE = 128
MLA_ROPE = 64
MLA_QK = MLA_NOPE + MLA_ROPE
MLA_PAD = 256
MLA_Q_LORA = 256
MLA_KV_LORA = 256
N_BRANCH = 4
BRANCH_WIDTH = 512
ROPE_BASE = 10000.0
EPS = 1e-6
NEG_INF = -1e30
IN_WIDTH = 7744
U_WIDTH = 8192
_U_SEGS = (("gates", 3648, 4096, 0), ("conv", 0, 1024, 4096), ("sb", 1024, 1536, 5120), ("qlat", 2560, 256, 6656),
           ("kvlat", 2816, 256, 6912), ("memq", 3136, 512, 7168), ("krope", 3072, 64, 7680))
U_PAD_FROM = 7744

ADAM_LR, ADAM_B1, ADAM_B2, ADAM_EPS, ADAM_WD, ADAM_STEP = 0.001, 0.9, 0.999, 1e-08, 0.01, 10

VMEM_LIMIT = 48 * 1024 * 1024
ATTN_BLOCK = 256
MESH = pl.DeviceIdType.MESH

SHARDED = ("ffn1_w_in", "ffn1_w_out", "w_in", "conv_dw", "mla_w_uq", "mla_w_ukv", "mem_w_kv", "w_branch", "w_out",
           "ffn2_w_in", "ffn2_w_out")
SHARD_AXIS = {"ffn1_w_in": 2, "ffn1_w_out": 1, "w_in": 2, "conv_dw": 2, "mla_w_uq": 2, "mla_w_ukv": 2, "mem_w_kv": 1,
              "w_branch": 3, "w_out": 1, "ffn2_w_in": 2, "ffn2_w_out": 1}
SMALL = ("ffn1_norm", "mix_norm", "conv_b", "conv_ln_g", "conv_ln_b", "sb_q_hnorm", "sb_k_hnorm", "mla_q_norm",
         "mla_kv_norm", "mla_q_hnorm", "mla_k_hnorm", "mem_norm", "mem_q_hnorm", "mem_k_hnorm", "ffn2_norm")
WEIGHTS = ("ffn1_norm", "ffn1_w_in", "ffn1_w_out", "mix_norm", "w_in", "conv_dw", "conv_b", "conv_ln_g", "conv_ln_b",
           "sb_q_hnorm", "sb_k_hnorm", "mla_q_norm", "mla_w_uq", "mla_kv_norm", "mla_w_ukv", "mla_q_hnorm",
           "mla_k_hnorm", "mem_norm", "mem_w_kv", "mem_q_hnorm", "mem_k_hnorm", "w_branch", "w_out", "ffn2_norm",
           "ffn2_w_in", "ffn2_w_out")
PACK_GRAIN = 2 * 256 * 1024
SMALL_PAD = 8 * 128


def _params(sem, vmem=VMEM_LIMIT):
    return pltpu.CompilerParams(dimension_semantics=sem, vmem_limit_bytes=vmem)


def _pick(n, pref):
    for t in pref:
        if n % t == 0:
            return t
    return n


def mm(name, a, b, form, *, out_dtype=f32, alpha=1.0, res=None, tm=None, tn=None, tk=None):
    if form == "nn":
        (M, K), (K2, N) = a.shape, b.shape
    elif form == "nt":
        (M, K), (N, K2) = a.shape, b.shape
    else:
        (K, M), (K2, N) = a.shape, b.shape
    assert K == K2, (name, a.shape, b.shape)
    tm = tm or _pick(M, (1024, 512, 256, 128))
    tn = tn or _pick(N, (1024, 512, 256, 128))
    tk = tk or (_pick(K, (512,)) if form == "tn" else _pick(K, (2048, 1024, 512, 256)))
    nk = K // tk
    if form == "nn":
        a_spec = pl.BlockSpec((tm, tk), lambda i, j, k: (i, k))
        b_spec = pl.BlockSpec((tk, tn), lambda i, j, k: (k, j))
        dims = (((1,), (0,)), ((), ()))
    elif form == "nt":
        a_spec = pl.BlockSpec((tm, tk), lambda i, j, k: (i, k))
        b_spec = pl.BlockSpec((tn, tk), lambda i, j, k: (j, k))
        dims = (((1,), (1,)), ((), ()))
    else:
        a_spec = pl.BlockSpec((tk, tm), lambda i, j, k: (k, i))
        b_spec = pl.BlockSpec((tk, tn), lambda i, j, k: (k, j))
        dims = (((0,), (0,)), ((), ()))
    o_spec = pl.BlockSpec((tm, tn), lambda i, j, k: (i, j))
    has_res = res is not None

    def body(a_ref, b_ref, *rest):
        if has_res:
            r_ref, o_ref, acc_ref = rest
        else:
            o_ref, acc_ref = rest
        k = pl.program_id(2)
        part = lax.dot_general(a_ref[...].astype(bf16), b_ref[...].astype(bf16), dims, preferred_element_type=f32)

        def finish(acc):
            r = acc if alpha == 1.0 else acc * alpha
            if has_res:
                r = r_ref[...].astype(f32) + r
            o_ref[...] = r.astype(o_ref.dtype)

        if nk == 1:
            finish(part)
        else:
            @pl.when(k == 0)
            def _():
                acc_ref[...] = part

            @pl.when(k > 0)
            def _():
                acc_ref[...] += part

            @pl.when(k == nk - 1)
            def _():
                finish(acc_ref[...])

    ins = [a, b] + ([res] if has_res else [])
    in_specs = [a_spec, b_spec] + ([o_spec] if has_res else [])
    return pl.pallas_call(
        body, name=name, grid=(M // tm, N // tn, nk), in_specs=in_specs, out_specs=o_spec,
        out_shape=jax.ShapeDtypeStruct((M, N), out_dtype), scratch_shapes=[pltpu.VMEM((tm, tn), f32)],
        compiler_params=_params(("parallel", "parallel", "arbitrary")))(*ins)


def _row_spec(T, w, off, st):
    return pl.BlockSpec((T, w), lambda i, h: (i, off + st * h))


def _full_spec(p):
    return pl.BlockSpec(p.shape, lambda i, h: (0,) * p.ndim)


def rowwise(name, fn, rows, params, outs, *, T=512, H=1):
    S = rows[0][0].shape[0]
    T = min(T, S)
    n_r, n_p = len(rows), len(params)

    def body(*refs):
        vals = [r[...] for r in refs[:n_r + n_p]]
        res = fn(*vals)
        for o_ref, r in zip(refs[n_r + n_p:], res):
            o_ref[...] = r.astype(o_ref.dtype)

    res = pl.pallas_call(
        body, name=name, grid=(S // T, H),
        in_specs=[_row_spec(T, w, off, st) for (_, w, off, st) in rows] + [_full_spec(p) for p in params],
        out_specs=[_row_spec(T, w, off, st) for (_, _, w, off, st) in outs],
        out_shape=[jax.ShapeDtypeStruct((S, tw), dt) for (tw, dt, _, _, _) in outs],
        compiler_params=_params(("parallel", "arbitrary")))(*[r[0] for r in rows], *params)
    return res


def rowwise_bwd(name, fn, rows, params, douts, drows, *, T=512, H=1, nondiff=()):
    S = rows[0][0].shape[0]
    T = min(T, S)
    n_r, n_p, n_d = len(rows), len(params), len(douts)
    diff_idx = [k for k in range(n_r) if k not in nondiff]
    shared = [H > 1 and rows[k][3] == 0 for k in diff_idx]

    def body(*refs):
        i, h = pl.program_id(0), pl.program_id(1)
        row_vals = [r[...].astype(f32) for r in refs[:n_r]]
        p_vals = [r[...].astype(f32) for r in refs[n_r:n_r + n_p]]
        d_vals = [r[...].astype(f32) for r in refs[n_r + n_p:n_r + n_p + n_d]]
        out_refs = refs[n_r + n_p + n_d:]

        def f(*args):
            full = list(row_vals)
            for k, v in zip(diff_idx, args[:len(diff_idx)]):
                full[k] = v
            return tuple(fn(*full, *args[len(diff_idx):]))

        _, vjp = jax.vjp(f, *[row_vals[k] for k in diff_idx], *p_vals)
        cts = vjp(tuple(d_vals))
        for o_ref, ct, sh in zip(out_refs[:len(diff_idx)], cts[:len(diff_idx)], shared):
            if sh:
                @pl.when(h == 0)
                def _():
                    o_ref[...] = ct.astype(o_ref.dtype)

                @pl.when(h > 0)
                def _():
                    o_ref[...] += ct.astype(o_ref.dtype)
            else:
                o_ref[...] = ct.astype(o_ref.dtype)
        first = jnp.logical_and(i == 0, h == 0)
        for o_ref, ct in zip(out_refs[len(diff_idx):], cts[len(diff_idx):]):
            @pl.when(first)
            def _():
                o_ref[...] = ct

            @pl.when(jnp.logical_not(first))
            def _():
                o_ref[...] += ct

    res = pl.pallas_call(
        body, name=name, grid=(S // T, H),
        in_specs=[_row_spec(T, w, off, st) for (_, w, off, st) in rows] + [_full_spec(p) for p in params]
        + [_row_spec(T, w, off, st) for (_, w, off, st) in douts],
        out_specs=[_row_spec(T, w, off, st) for (_, _, w, off, st) in drows] + [_full_spec(p) for p in params],
        out_shape=[jax.ShapeDtypeStruct((S, tw), dt) for (tw, dt, _, _, _) in drows]
        + [jax.ShapeDtypeStruct(p.shape, f32) for p in params],
        compiler_params=_params(("arbitrary", "arbitrary")))(*[r[0] for r in rows], *params, *[d[0] for d in douts])
    return res[:len(diff_idx)], res[len(diff_idx):]


def _rms(x, g, n=None):
    x = x.astype(f32)
    n = n or x.shape[-1]
    return x * lax.rsqrt(jnp.sum(x * x, axis=-1, keepdims=True) * (1.0 / n) + EPS) * g.astype(f32)


def _sigmoid(x):
    return 1.0 / (1.0 + jnp.exp(-x))


def _silu(x):
    return x * _sigmoid(x)


def fn_rms(x, g):
    return (_rms(x, g),)


def fn_rms_res(x, g):
    return (x.astype(f32), _rms(x, g))


def fn_swiglu(gate, up):
    return (_silu(gate.astype(f32)) * up.astype(f32),)


def fn_sb_prep(q, k, gq, gk):
    return (_rms(q, gq), _rms(k, gk))


def fn_ln_silu(y, b, g, beta):
    y = y.astype(f32) + b
    mu = jnp.mean(y, axis=-1, keepdims=True)
    var = jnp.mean(jnp.square(y - mu), axis=-1, keepdims=True)
    return (_silu((y - mu) * lax.rsqrt(var + EPS) * g + beta),)


def fn_merge(g0, g1, g2, g3, p0, p1, p2, p3):
    out = _sigmoid(g0.astype(f32)) * p0.astype(f32)
    for g, p in ((g1, p1), (g2, p2), (g3, p3)):
        out = out + _sigmoid(g.astype(f32)) * p.astype(f32)
    return (out,)


def _rot_fwd(x):
    z = jnp.zeros_like(x[:, :MLA_NOPE])
    h = MLA_ROPE // 2
    return jnp.concatenate([z, -x[:, MLA_NOPE + h:MLA_QK], x[:, MLA_NOPE:MLA_NOPE + h], z[:, :MLA_PAD - MLA_QK]], axis=-1)


def _rot_bwd(g):
    z = jnp.zeros_like(g[:, :MLA_NOPE])
    h = MLA_ROPE // 2
    return jnp.concatenate([z, g[:, MLA_NOPE + h:MLA_QK], -g[:, MLA_NOPE:MLA_NOPE + h], z[:, :MLA_PAD - MLA_QK]], axis=-1)


@jax.custom_vjp
def _rope(x, c, s):
    return x * c + _rot_fwd(x) * s


def _rope_f(x, c, s):
    return _rope(x, c, s), (c, s)


def _rope_b(res, g):
    c, s = res
    return g * c + _rot_bwd(g * s), jnp.zeros_like(c), jnp.zeros_like(s)


_rope.defvjp(_rope_f, _rope_b)


def fn_mla_q(q, c, s, gain):
    return (_rope(_rms(q, gain, MLA_QK), c, s),)


def fn_mla_k(kn, kr, c, s, gain):
    k = jnp.concatenate([kn.astype(f32), kr.astype(f32)], axis=-1)
    return (_rope(_rms(k, gain, MLA_QK), c, s),)


def fn_mla_k_v(kn, kr, v, c, s, gain):
    return (fn_mla_k(kn, kr, c, s, gain)[0], v.astype(f32))


def fn_mem_k_v(k, v, gain):
    return (_rms(k, gain), v.astype(f32))


def fn_adamw(w, g, m, v):
    m = ADAM_B1 * m + (1.0 - ADAM_B1) * g
    v = ADAM_B2 * v + (1.0 - ADAM_B2) * jnp.square(g)
    m_hat = m / (1.0 - ADAM_B1 ** ADAM_STEP)
    v_hat = v / (1.0 - ADAM_B2 ** ADAM_STEP)
    delta = -ADAM_LR * (m_hat / (jnp.sqrt(v_hat) + ADAM_EPS) + ADAM_WD * w)
    return delta, m, v


def _head_spec(rows, w, off, st):
    return pl.BlockSpec((rows, w), lambda h, i: (0, off + st * h))


def _qblk_spec(B, w, off, st):
    return pl.BlockSpec((B, w), lambda h, i: (i, off + st * h))


def _chunk_mask(B):
    r = lax.broadcasted_iota(jnp.int32, (B, B), 0) // CHUNK
    c = lax.broadcasted_iota(jnp.int32, (B, B), 1) // CHUNK
    return c <= r


_NT = (((1,), (1,)), ((), ()))
_TN = (((0,), (0,)), ((), ()))


def attn_fwd(name, q, k, v, *, scale, mask):
    Sq, Sk = q[0].shape[0], k[0].shape[0]
    B = min(ATTN_BLOCK, Sq)
    KB = B if mask == "chunk" else Sk
    dq = q[1]

    def body(q_ref, k_ref, v_ref, o_ref, lse_ref):
        i = pl.program_id(1)
        qv = q_ref[...]

        def block(j, carry, masked):
            m, l, acc = carry
            off = pl.multiple_of(j * KB, KB)
            kb, vb = k_ref[pl.ds(off, KB), :].astype(bf16), v_ref[pl.ds(off, KB), :].astype(bf16)
            s = lax.dot_general(qv, kb, _NT, preferred_element_type=f32) * scale
            if masked:
                s = jnp.where(_chunk_mask(B), s, NEG_INF)
            m_new = jnp.maximum(m, jnp.max(s, axis=-1, keepdims=True))
            p = jnp.exp(s - m_new)
            corr = jnp.exp(m - m_new)
            l = l * corr + jnp.sum(p, axis=-1, keepdims=True)
            acc = acc * corr + jnp.dot(p.astype(bf16), vb, preferred_element_type=f32)
            return m_new, l, acc

        init = (jnp.full((B, 1), NEG_INF, f32), jnp.zeros((B, 1), f32), jnp.zeros((B, HEAD_DIM), f32))
        if mask == "chunk":
            carry = lax.fori_loop(0, i, lambda j, c: block(j, c, False), init)
            m, l, acc = block(i, carry, True)
        else:
            m, l, acc = block(0, init, False)
        o_ref[...] = (acc / l).astype(o_ref.dtype)
        lse_ref[...] = jnp.broadcast_to(m + jnp.log(l), (B, HEAD_DIM))

    return pl.pallas_call(
        body, name=name, grid=(HEADS, Sq // B),
        in_specs=[_qblk_spec(B, *q[1:]), _head_spec(Sk, *k[1:]), _head_spec(Sk, *v[1:])],
        out_specs=[_qblk_spec(B, HEAD_DIM, 0, 1), _qblk_spec(B, HEAD_DIM, 0, 1)],
        out_shape=[jax.ShapeDtypeStruct((Sq, HEADS * HEAD_DIM), f32), jax.ShapeDtypeStruct((Sq, HEADS * HEAD_DIM), f32)],
        compiler_params=_params(("parallel", "arbitrary")))(q[0], k[0], v[0])


def attn_bwd(name, q, k, v, o, do, lse, *, scale, mask):
    Sq, Sk = q[0].shape[0], k[0].shape[0]
    B = min(ATTN_BLOCK, Sq)
    KB = B if mask == "chunk" else Sk
    dq_w = q[1]

    def body(q_ref, k_ref, v_ref, o_ref, do_ref, lse_ref, dq_ref, dk_ref, dv_ref):
        i = pl.program_id(1)

        @pl.when(i == 0)
        def _():
            dk_ref[...] = jnp.zeros_like(dk_ref)
            dv_ref[...] = jnp.zeros_like(dv_ref)

        qv, dov = q_ref[...], do_ref[...].astype(bf16)
        delta = jnp.sum(do_ref[...].astype(f32) * o_ref[...].astype(f32), axis=-1, keepdims=True)
        lse_v = lse_ref[:, :1]

        def block(j, dq_acc, masked):
            off = pl.multiple_of(j * KB, KB)
            kb, vb = k_ref[pl.ds(off, KB), :].astype(bf16), v_ref[pl.ds(off, KB), :].astype(bf16)
            s = lax.dot_general(qv, kb, _NT, preferred_element_type=f32) * scale
            if masked:
                s = jnp.where(_chunk_mask(B), s, NEG_INF)
            p = jnp.exp(s - lse_v)
            dv_ref[pl.ds(off, KB), :] += lax.dot_general(p.astype(bf16), dov, _TN, preferred_element_type=f32)
            dp = lax.dot_general(dov, vb, _NT, preferred_element_type=f32)
            ds = (p * (dp - delta) * scale).astype(bf16)
            dk_ref[pl.ds(off, KB), :] += lax.dot_general(ds, qv, _TN, preferred_element_type=f32)
            return dq_acc + jnp.dot(ds, kb, preferred_element_type=f32)

        init = jnp.zeros((B, dq_w), f32)
        if mask == "chunk":
            acc = lax.fori_loop(0, i, lambda j, c: block(j, c, False), init)
            acc = block(i, acc, True)
        else:
            acc = block(0, init, False)
        dq_ref[...] = acc.astype(dq_ref.dtype)

    hd = _qblk_spec(B, HEAD_DIM, 0, 1)
    return pl.pallas_call(
        body, name=name, grid=(HEADS, Sq // B),
        in_specs=[_qblk_spec(B, *q[1:]), _head_spec(Sk, *k[1:]), _head_spec(Sk, *v[1:]), hd, hd, hd],
        out_specs=[_qblk_spec(B, dq_w, 0, 1), _head_spec(Sk, dq_w, 0, 1), _head_spec(Sk, HEAD_DIM, 0, 1)],
        out_shape=[jax.ShapeDtypeStruct((Sq, HEADS * dq_w), f32), jax.ShapeDtypeStruct((Sk, HEADS * dq_w), f32),
                   jax.ShapeDtypeStruct((Sk, HEADS * HEAD_DIM), f32)],
        compiler_params=_params(("arbitrary", "arbitrary"), 56 * 1024 * 1024))(q[0], k[0], v[0], o, do, lse)


def _tri(B, rel):
    r = lax.broadcasted_iota(jnp.int32, (B, B), 0)
    c = lax.broadcasted_iota(jnp.int32, (B, B), 1)
    return rel(r, c).astype(bf16)


def _sb_scores(qv, kb, scale):
    z = lax.dot_general(qv, kb, _NT, preferred_element_type=f32) * scale
    e = jnp.exp(-jnp.abs(z))
    log_keep = -(jnp.maximum(z, 0.0) + jnp.log(1.0 + e))
    return z, e, log_keep


def _split_dot(x, m):
    hi = x.astype(bf16)
    lo = (x - hi.astype(f32)).astype(bf16)
    return jnp.dot(hi, m, preferred_element_type=f32) + jnp.dot(lo, m, preferred_element_type=f32)


def sb_fwd(name, q, k, v):
    S = q[0].shape[0]
    B = min(ATTN_BLOCK, S)
    scale = HEAD_DIM ** -0.5
    m_ex = _tri(B, lambda j, s: j > s)

    def body(q_ref, k_ref, v_ref, mex_ref, o_ref, tot_ref):
        i = pl.program_id(1)
        qv, mex = q_ref[...], mex_ref[...]
        strict = lax.broadcasted_iota(jnp.int32, (B, B), 1) < lax.broadcasted_iota(jnp.int32, (B, B), 0)

        def block(j, carry, diag):
            later, acc = carry
            off = pl.multiple_of(j * B, B)
            kb, vb = k_ref[pl.ds(off, B), :].astype(bf16), v_ref[pl.ds(off, B), :].astype(bf16)
            z, _, lk = _sb_scores(qv, kb, scale)
            if diag:
                lk = jnp.where(strict, lk, 0.0)
            a = jnp.exp(z + lk + _split_dot(lk, mex) + later)
            if diag:
                a = jnp.where(strict, a, 0.0)
            acc = acc + jnp.dot(a.astype(bf16), vb, preferred_element_type=f32)
            return later + jnp.sum(lk, axis=-1, keepdims=True), acc

        carry = block(i, (jnp.zeros((B, 1), f32), jnp.zeros((B, HEAD_DIM), f32)), True)
        total, acc = lax.fori_loop(0, i, lambda t, c: block(i - 1 - t, c, False), carry)
        o_ref[...] = acc.astype(o_ref.dtype)
        tot_ref[...] = jnp.broadcast_to(total, (B, HEAD_DIM))

    hd = _qblk_spec(B, HEAD_DIM, 0, 1)
    return pl.pallas_call(
        body, name=name, grid=(HEADS, S // B),
        in_specs=[_qblk_spec(B, *q[1:]), _head_spec(S, *k[1:]), _head_spec(S, *v[1:]),
                  pl.BlockSpec((B, B), lambda h, i: (0, 0))],
        out_specs=[hd, hd],
        out_shape=[jax.ShapeDtypeStruct((S, HEADS * HEAD_DIM), f32), jax.ShapeDtypeStruct((S, HEADS * HEAD_DIM), f32)],
        compiler_params=_params(("parallel", "arbitrary")))(q[0], k[0], v[0], m_ex)


def sb_bwd(name, q, k, v, tot, do):
    S = q[0].shape[0]
    B = min(ATTN_BLOCK, S)
    scale = HEAD_DIM ** -0.5
    m_le, m_lt = _tri(B, lambda j, s: j <= s), _tri(B, lambda j, s: j < s)

    def body(q_ref, k_ref, v_ref, tot_ref, do_ref, mle_ref, mlt_ref, dq_ref, dk_ref, dv_ref):
        i = pl.program_id(1)

        @pl.when(i == 0)
        def _():
            dk_ref[...] = jnp.zeros_like(dk_ref)
            dv_ref[...] = jnp.zeros_like(dv_ref)

        qv, dov, mle, mlt = q_ref[...], do_ref[...].astype(bf16), mle_ref[...], mlt_ref[...]
        total = tot_ref[:, :1]
        strict = lax.broadcasted_iota(jnp.int32, (B, B), 1) < lax.broadcasted_iota(jnp.int32, (B, B), 0)

        def block(j, carry, diag):
            before, g_before, dq_acc = carry
            off = pl.multiple_of(j * B, B)
            kb, vb = k_ref[pl.ds(off, B), :].astype(bf16), v_ref[pl.ds(off, B), :].astype(bf16)
            z, e, lk = _sb_scores(qv, kb, scale)
            sig = jnp.where(z >= 0, 1.0, e) / (1.0 + e)
            if diag:
                lk = jnp.where(strict, lk, 0.0)
            later = (total - before) - _split_dot(lk, mle)
            a = jnp.exp(z + lk + later)
            if diag:
                a = jnp.where(strict, a, 0.0)
            g = a * lax.dot_general(dov, vb, _NT, preferred_element_type=f32)
            prefix = g_before + jnp.dot(g.astype(bf16), mlt, preferred_element_type=f32)
            dz = (g * (1.0 - sig) - prefix * sig) * scale
            if diag:
                dz = jnp.where(strict, dz, 0.0)
            dzb = dz.astype(bf16)
            dk_ref[pl.ds(off, B), :] += lax.dot_general(dzb, qv, _TN, preferred_element_type=f32)
            dv_ref[pl.ds(off, B), :] += lax.dot_general(a.astype(bf16), dov, _TN, preferred_element_type=f32)
            return (before + jnp.sum(lk, axis=-1, keepdims=True), g_before + jnp.sum(g, axis=-1, keepdims=True),
                    dq_acc + jnp.dot(dzb, kb, preferred_element_type=f32))

        zero = jnp.zeros((B, 1), f32)
        carry = lax.fori_loop(0, i, lambda j, c: block(j, c, False), (zero, zero, jnp.zeros((B, HEAD_DIM), f32)))
        _, _, dq_acc = block(i, carry, True)
        dq_ref[...] = dq_acc.astype(dq_ref.dtype)

    hd = _qblk_spec(B, HEAD_DIM, 0, 1)
    tri = pl.BlockSpec((B, B), lambda h, i: (0, 0))
    return pl.pallas_call(
        body, name=name, grid=(HEADS, S // B),
        in_specs=[_qblk_spec(B, *q[1:]), _head_spec(S, *k[1:]), _head_spec(S, *v[1:]), hd, hd, tri, tri],
        out_specs=[hd, _head_spec(S, HEAD_DIM, 0, 1), _head_spec(S, HEAD_DIM, 0, 1)],
        out_shape=[jax.ShapeDtypeStruct((S, HEADS * HEAD_DIM), f32), jax.ShapeDtypeStruct((S, HEADS * HEAD_DIM), f32),
                   jax.ShapeDtypeStruct((S, HEADS * HEAD_DIM), f32)],
        compiler_params=_params(("arbitrary", "arbitrary")))(q[0], k[0], v[0], tot, do, m_le, m_lt)


CONV_HALO = 32
CONV_A_BLK, CONV_G_BLK = 8, 9


def _glu(a, g):
    return a.astype(f32) * _sigmoid(g.astype(f32))


def conv_fwd(name, u, dw):
    S = u.shape[0]
    T = min(512, S)
    nT = S // T

    def body(a_ref, g_ref, ap_ref, gp_ref, dw_ref, y_ref, ext_ref):
        i = pl.program_id(0)
        prev = _glu(ap_ref[T - CONV_HALO:, :], gp_ref[T - CONV_HALO:, :])
        ext_ref[:CONV_HALO, :] = jnp.where(i > 0, prev, 0.0)
        ext_ref[CONV_HALO:, :] = _glu(a_ref[...], g_ref[...])
        acc = jnp.zeros((T, CONV_CH), f32)
        for w in range(CONV_WIDTH):
            acc = acc + dw_ref[w:w + 1, :] * ext_ref[pl.ds(w + CONV_HALO - (CONV_WIDTH - 1), T), :]
        y_ref[...] = acc

    cur = lambda blk: pl.BlockSpec((T, CONV_CH), lambda i: (i, blk))
    prv = lambda blk: pl.BlockSpec((T, CONV_CH), lambda i: (jnp.maximum(i - 1, 0), blk))
    return pl.pallas_call(
        body, name=name, grid=(nT,),
        in_specs=[cur(CONV_A_BLK), cur(CONV_G_BLK), prv(CONV_A_BLK), prv(CONV_G_BLK),
                  pl.BlockSpec(dw.shape, lambda i: (0, 0))],
        out_specs=pl.BlockSpec((T, CONV_CH), lambda i: (i, 0)),
        out_shape=jax.ShapeDtypeStruct((S, CONV_CH), f32),
        scratch_shapes=[pltpu.VMEM((T + CONV_HALO, CONV_CH), f32)],
        compiler_params=_params(("arbitrary",)))(u, u, u, u, dw)


def conv_bwd(name, u, dy, dw):
    S = u.shape[0]
    T = min(512, S)
    nT = S // T
    lead = CONV_HALO - (CONV_WIDTH - 1)

    def body(a_ref, g_ref, ap_ref, gp_ref, dy_ref, dyn_ref, dw_ref, du_ref, ddw_ref, ext_ref, dext_ref):
        i = pl.program_id(0)
        prev = _glu(ap_ref[T - CONV_HALO:, :], gp_ref[T - CONV_HALO:, :])
        ext_ref[:CONV_HALO, :] = jnp.where(i > 0, prev, 0.0)
        a, sg = a_ref[...].astype(f32), _sigmoid(g_ref[...].astype(f32))
        ext_ref[CONV_HALO:, :] = a * sg
        dyv = dy_ref[...]
        dext_ref[:T, :] = dyv
        dext_ref[T:, :] = jnp.where(i < nT - 1, dyn_ref[:CONV_HALO, :], 0.0)
        @pl.when(i == 0)
        def _():
            ddw_ref[...] = jnp.zeros_like(ddw_ref)

        dglu = jnp.zeros((T, CONV_CH), f32)
        for w in range(CONV_WIDTH):
            dglu = dglu + dw_ref[w:w + 1, :] * dext_ref[pl.ds(CONV_WIDTH - 1 - w, T), :]
            ddw_ref[w:w + 1, :] += jnp.sum(dyv * ext_ref[pl.ds(w + lead, T), :], axis=0, keepdims=True)

        du_ref[:, :CONV_CH] = (dglu * sg).astype(du_ref.dtype)
        du_ref[:, CONV_CH:] = (dglu * a * sg * (1.0 - sg)).astype(du_ref.dtype)

    cur = lambda blk: pl.BlockSpec((T, CONV_CH), lambda i: (i, blk))
    prv = lambda blk: pl.BlockSpec((T, CONV_CH), lambda i: (jnp.maximum(i - 1, 0), blk))
    return pl.pallas_call(
        body, name=name, grid=(nT,),
        in_specs=[cur(CONV_A_BLK), cur(CONV_G_BLK), prv(CONV_A_BLK), prv(CONV_G_BLK),
                  pl.BlockSpec((T, CONV_CH), lambda i: (i, 0)),
                  pl.BlockSpec((T, CONV_CH), lambda i: (jnp.minimum(i + 1, nT - 1), 0)),
                  pl.BlockSpec(dw.shape, lambda i: (0, 0))],
        out_specs=[pl.BlockSpec((T, 2 * CONV_CH), lambda i: (i, 0)), pl.BlockSpec(dw.shape, lambda i: (0, 0))],
        out_shape=[jax.ShapeDtypeStruct((S, 2 * CONV_CH), bf16), jax.ShapeDtypeStruct(dw.shape, f32)],
        scratch_shapes=[pltpu.VMEM((T + CONV_HALO, CONV_CH), f32), pltpu.VMEM((T + CONV_HALO, CONV_CH), f32)],
        compiler_params=_params(("arbitrary",)))(u, u, u, u, dy, dy, dw)


def rope_tables(pos_col):
    S = pos_col.shape[0]
    T = min(512, S)
    inv_freq = ROPE_BASE ** (-jnp.arange(0, MLA_ROPE, 2, dtype=f32) / MLA_ROPE)
    zeros = jnp.zeros((MLA_NOPE,), f32)
    inv_row = jnp.concatenate([zeros, inv_freq, inv_freq, zeros[:MLA_PAD - MLA_QK]]).reshape(1, MLA_PAD)

    def body(p_ref, f_ref, c_ref, s_ref):
        lane = lax.broadcasted_iota(jnp.int32, (T, MLA_PAD), 1)
        ang = p_ref[...].astype(f32) * f_ref[...]
        rot = jnp.logical_and(lane >= MLA_NOPE, lane < MLA_QK)
        c_ref[...] = jnp.where(rot, jnp.cos(ang), jnp.where(lane < MLA_NOPE, 1.0, 0.0))
        s_ref[...] = jnp.where(rot, jnp.sin(ang), 0.0)

    spec = pl.BlockSpec((T, MLA_PAD), lambda i: (i, 0))
    return pl.pallas_call(
        body, name="rope_tables", grid=(S // T,),
        in_specs=[pl.BlockSpec((T, 1), lambda i: (i, 0)), pl.BlockSpec((1, MLA_PAD), lambda i: (0, 0))],
        out_specs=[spec, spec], out_shape=[jax.ShapeDtypeStruct((S, MLA_PAD), f32)] * 2,
        compiler_params=_params(("parallel",)))(pos_col, inv_row)


def loss_head(y, target):
    S, D = y.shape
    T = min(512, S)

    def body(y_ref, t_ref, dy_ref, l_ref):
        i = pl.program_id(0)
        err = y_ref[...] - t_ref[...]
        dy_ref[...] = err * (1.0 / D)
        part = 0.5 * jnp.sum(jnp.sum(err * err, axis=-1, keepdims=True) * (1.0 / D), axis=0, keepdims=True)
        part = jnp.broadcast_to(part, l_ref.shape)

        @pl.when(i == 0)
        def _():
            l_ref[...] = part

        @pl.when(i > 0)
        def _():
            l_ref[...] += part

    spec = pl.BlockSpec((T, D), lambda i: (i, 0))
    return pl.pallas_call(
        body, name="loss_head", grid=(S // T,), in_specs=[spec, spec],
        out_specs=[spec, pl.BlockSpec((8, 128), lambda i: (0, 0))],
        out_shape=[jax.ShapeDtypeStruct((S, D), f32), jax.ShapeDtypeStruct((8, 128), f32)],
        compiler_params=_params(("arbitrary",)))(y, target)


def _row(a, w=None, off=0, st=0):
    return (a, w or a.shape[1], off, st)


def _out(tw, dt, w=None, off=0, st=0):
    return (tw, dt, w or tw, off, st)


ACT = f32


def ffn_fwd(tag, x, g, w_in, w_out):
    (h,) = rowwise(f"{tag}_rms", fn_rms, [_row(x)], [g], [_out(D_MODEL, bf16)])
    u = mm(f"{tag}_in", h, w_in, "nn", out_dtype=ACT)
    (a,) = rowwise(f"{tag}_swiglu", fn_swiglu, [_row(u, FFN_HIDDEN, 0), _row(u, FFN_HIDDEN, 1)], [],
                   [_out(FFN_HIDDEN, bf16)])
    y = mm(f"{tag}_out", a, w_out, "nn", alpha=0.5, res=x)
    return y, (x, h, u, a)


def ffn_bwd(tag, saved, g, w_in, w_out, dy):
    x, h, u, a = saved
    d_w_out = mm(f"{tag}_dwout", a, dy, "tn", alpha=0.5)
    da = mm(f"{tag}_da", dy, w_out, "nt", alpha=0.5, out_dtype=ACT)
    (dgate, dup), _ = rowwise_bwd(f"{tag}_dswiglu", fn_swiglu, [_row(u, FFN_HIDDEN, 0), _row(u, FFN_HIDDEN, 1)], [],
                                  [_row(da)], [_out(FFN_HIDDEN, bf16), _out(FFN_HIDDEN, bf16)])
    du = jnp.concatenate([dgate, dup], axis=1)
    d_w_in = mm(f"{tag}_dwin", h, du, "tn")
    dh = mm(f"{tag}_dh", du, w_in, "nt", out_dtype=ACT)
    (dx,), (dg,) = rowwise_bwd(f"{tag}_drms", fn_rms_res, [_row(x)], [g], [_row(dy), _row(dh)], [_out(D_MODEL, f32)])
    return dx, dg, d_w_in, d_w_out


def _seg(name):
    for n, _, w, start in _U_SEGS:
        if n == name:
            return start, w
    raise KeyError(name)


def mix_fwd(tag, x, mem_n_in, tabs, p):
    cos_t, sin_t = tabs
    (h,) = rowwise(f"{tag}_rms", fn_rms, [_row(x)], [p["mix_norm"]], [_out(D_MODEL, bf16)])
    u = mm(f"{tag}_in", h, p["w_in"], "nn", out_dtype=ACT)
    yc = conv_fwd(f"{tag}_conv", u, p["conv_dw"])
    (br_a,) = rowwise(f"{tag}_lnsilu", fn_ln_silu, [_row(yc)], [p["conv_b"], p["conv_ln_g"], p["conv_ln_b"]],
                      [_out(BRANCH_WIDTH, bf16)])
    sb0 = _seg("sb")[0] // HEAD_DIM
    qs, ks = rowwise(f"{tag}_sbprep", fn_sb_prep, [_row(u, HEAD_DIM, sb0, 1), _row(u, HEAD_DIM, sb0 + HEADS, 1)],
                     [p["sb_q_hnorm"], p["sb_k_hnorm"]],
                     [_out(BRANCH_WIDTH, bf16, HEAD_DIM, 0, 1), _out(BRANCH_WIDTH, bf16, HEAD_DIM, 0, 1)], H=HEADS)
    sb_v = _row(u, HEAD_DIM, sb0 + 2 * HEADS, 1)
    br_b, tot_b = sb_fwd(f"{tag}_sb", _row(qs, HEAD_DIM, 0, 1), _row(ks, HEAD_DIM, 0, 1), sb_v)
    ql_n, kvl_n = rowwise(f"{tag}_latrms", lambda a, b, ga, gb: (_rms(a, ga), _rms(b, gb)),
                          [_row(u, MLA_Q_LORA, _seg("qlat")[0] // MLA_Q_LORA), _row(u, MLA_KV_LORA, _seg("kvlat")[0] // MLA_KV_LORA)],
                          [p["mla_q_norm"], p["mla_kv_norm"]], [_out(MLA_Q_LORA, bf16), _out(MLA_KV_LORA, bf16)])
    qfull = mm(f"{tag}_uq", ql_n, p["mla_w_uq"], "nn", out_dtype=ACT)
    kvfull = mm(f"{tag}_ukv", kvl_n, p["mla_w_ukv"], "nn", out_dtype=ACT)
    kr_row = _row(u, HEAD_DIM, _seg("krope")[0] // HEAD_DIM, 0)
    (qr,) = rowwise(f"{tag}_mlaq", fn_mla_q, [_row(qfull, MLA_PAD, 0, 1), _row(cos_t), _row(sin_t)], [p["mla_q_hnorm"]],
                    [_out(HEADS * MLA_PAD, bf16, MLA_PAD, 0, 1)], H=HEADS)
    (kr,) = rowwise(f"{tag}_mlak", fn_mla_k, [_row(kvfull, HEAD_DIM, 0, 2), kr_row, _row(cos_t), _row(sin_t)],
                    [p["mla_k_hnorm"]], [_out(HEADS * MLA_PAD, bf16, MLA_PAD, 0, 1)], H=HEADS)
    mla_v = _row(kvfull, HEAD_DIM, 1, 2)
    br_c, lse_c = attn_fwd(f"{tag}_mla", _row(qr, MLA_PAD, 0, 1), _row(kr, MLA_PAD, 0, 1), mla_v,
                           scale=MLA_QK ** -0.5, mask="chunk")
    (mem_n,) = rowwise(f"{tag}_memrms", fn_rms, [_row(mem_n_in)], [p["mem_norm"]], [_out(D_MODEL, bf16)])
    kvm = mm(f"{tag}_memkv", mem_n, p["mem_w_kv"], "nn", out_dtype=ACT)
    (km,) = rowwise(f"{tag}_memk", fn_rms, [_row(kvm, HEAD_DIM, 0, 1)], [p["mem_k_hnorm"]],
                    [_out(BRANCH_WIDTH, bf16, HEAD_DIM, 0, 1)], H=HEADS)
    mq0 = _seg("memq")[0] // HEAD_DIM
    (qm,) = rowwise(f"{tag}_memq", fn_rms, [_row(u, HEAD_DIM, mq0, 1)], [p["mem_q_hnorm"]],
                    [_out(BRANCH_WIDTH, bf16, HEAD_DIM, 0, 1)], H=HEADS)
    mem_v = _row(kvm, HEAD_DIM, HEADS, 1)
    br_d, lse_d = attn_fwd(f"{tag}_memattn", _row(qm, HEAD_DIM, 0, 1), _row(km, HEAD_DIM, 0, 1), mem_v,
                           scale=HEAD_DIM ** -0.5, mask=None)
    branches = (br_a, br_b, br_c, br_d)
    proj = [mm(f"{tag}_branch{b}", branches[b], p["w_branch"][b], "nn", out_dtype=ACT) for b in range(N_BRANCH)]
    gate_rows = [_row(u, D_MODEL, b) for b in range(N_BRANCH)]
    (merged,) = rowwise(f"{tag}_merge", fn_merge, gate_rows + [_row(t) for t in proj], [], [_out(D_MODEL, bf16)], T=256)
    y = mm(f"{tag}_out", merged, p["w_out"], "nn", res=x)
    saved = dict(x=x, h=h, u=u, yc=yc, qs=qs, ks=ks, ql_n=ql_n, kvl_n=kvl_n, qfull=qfull, kvfull=kvfull, qr=qr, kr=kr,
                 lse_c=lse_c, mem_n=mem_n, kvm=kvm, km=km, qm=qm, lse_d=lse_d, branches=branches, proj=proj,
                 merged=merged, tot_b=tot_b)
    return y, saved


def mix_bwd(tag, sv, mem_n_in, tabs, p, dy):
    cos_t, sin_t = tabs
    u, S = sv["u"], sv["u"].shape[0]
    g = {}
    g["w_out"] = mm(f"{tag}_dwout", sv["merged"], dy, "tn")
    dmerged = mm(f"{tag}_dmerged", dy, p["w_out"], "nt", out_dtype=ACT)
    gate_rows = [_row(u, D_MODEL, b) for b in range(N_BRANCH)]
    d_merge, _ = rowwise_bwd(f"{tag}_dmerge", fn_merge, gate_rows + [_row(t) for t in sv["proj"]], [], [_row(dmerged)],
                             [_out(D_MODEL, bf16)] * (2 * N_BRANCH), T=256)
    d_gates, d_proj = d_merge[:N_BRANCH], d_merge[N_BRANCH:]
    g["w_branch"] = [mm(f"{tag}_dwbranch{b}", sv["branches"][b], d_proj[b], "tn") for b in range(N_BRANCH)]
    d_br = [mm(f"{tag}_dbranch{b}", d_proj[b], p["w_branch"][b], "nt", out_dtype=ACT) for b in range(N_BRANCH)]
    (dyc,), (g["conv_b"], g["conv_ln_g"], g["conv_ln_b"]) = rowwise_bwd(
        f"{tag}_dlnsilu", fn_ln_silu, [_row(sv["yc"])], [p["conv_b"], p["conv_ln_g"], p["conv_ln_b"]], [_row(d_br[0])],
        [_out(BRANCH_WIDTH, f32)])
    du_conv, g["conv_dw"] = conv_bwd(f"{tag}_dconv", u, dyc, p["conv_dw"])
    sb0 = _seg("sb")[0] // HEAD_DIM
    sb_v = _row(u, HEAD_DIM, sb0 + 2 * HEADS, 1)
    dqs, dks, dv_sb = sb_bwd(f"{tag}_dsb", _row(sv["qs"], HEAD_DIM, 0, 1), _row(sv["ks"], HEAD_DIM, 0, 1), sb_v,
                             sv["tot_b"], d_br[1])
    (du_sbq, du_sbk), (g["sb_q_hnorm"], g["sb_k_hnorm"]) = rowwise_bwd(
        f"{tag}_dsbprep", fn_sb_prep, [_row(u, HEAD_DIM, sb0, 1), _row(u, HEAD_DIM, sb0 + HEADS, 1)],
        [p["sb_q_hnorm"], p["sb_k_hnorm"]], [_row(dqs, HEAD_DIM, 0, 1), _row(dks, HEAD_DIM, 0, 1)],
        [_out(BRANCH_WIDTH, bf16, HEAD_DIM, 0, 1), _out(BRANCH_WIDTH, bf16, HEAD_DIM, 0, 1)], H=HEADS)
    mla_v = _row(sv["kvfull"], HEAD_DIM, 1, 2)
    dqr, dkr, dv_mla = attn_bwd(f"{tag}_dmla", _row(sv["qr"], MLA_PAD, 0, 1), _row(sv["kr"], MLA_PAD, 0, 1), mla_v,
                                sv["branches"][2], d_br[2], sv["lse_c"], scale=MLA_QK ** -0.5, mask="chunk")
    (dqfull,), (g["mla_q_hnorm"],) = rowwise_bwd(
        f"{tag}_dmlaq", fn_mla_q, [_row(sv["qfull"], MLA_PAD, 0, 1), _row(cos_t), _row(sin_t)], [p["mla_q_hnorm"]],
        [_row(dqr, MLA_PAD, 0, 1)], [_out(HEADS * MLA_PAD, bf16, MLA_PAD, 0, 1)], H=HEADS, nondiff=(1, 2))
    kr_row = _row(u, HEAD_DIM, _seg("krope")[0] // HEAD_DIM, 0)
    (dkn, du_krope, dvp), (g["mla_k_hnorm"],) = rowwise_bwd(
        f"{tag}_dmlak", fn_mla_k_v, [_row(sv["kvfull"], HEAD_DIM, 0, 2), kr_row, mla_v, _row(cos_t), _row(sin_t)],
        [p["mla_k_hnorm"]], [_row(dkr, MLA_PAD, 0, 1), _row(dv_mla, HEAD_DIM, 0, 1)],
        [_out(BRANCH_WIDTH, bf16, HEAD_DIM, 0, 1), _out(HEAD_DIM, f32), _out(BRANCH_WIDTH, bf16, HEAD_DIM, 0, 1)],
        H=HEADS, nondiff=(3, 4))
    dkvfull = _interleave(f"{tag}_dkvfull", dkn, dvp)
    g["mla_w_uq"] = mm(f"{tag}_dwuq", sv["ql_n"], dqfull, "tn")
    g["mla_w_ukv"] = mm(f"{tag}_dwukv", sv["kvl_n"], dkvfull, "tn")
    dql_n = mm(f"{tag}_dqln", dqfull, p["mla_w_uq"], "nt", out_dtype=ACT)
    dkvl_n = mm(f"{tag}_dkvln", dkvfull, p["mla_w_ukv"], "nt", out_dtype=ACT)
    (du_qlat, du_kvlat), (g["mla_q_norm"], g["mla_kv_norm"]) = rowwise_bwd(
        f"{tag}_dlatrms", lambda a, b, ga, gb: (_rms(a, ga), _rms(b, gb)),
        [_row(u, MLA_Q_LORA, _seg("qlat")[0] // MLA_Q_LORA), _row(u, MLA_KV_LORA, _seg("kvlat")[0] // MLA_KV_LORA)],
        [p["mla_q_norm"], p["mla_kv_norm"]], [_row(dql_n), _row(dkvl_n)], [_out(MLA_Q_LORA, bf16), _out(MLA_KV_LORA, bf16)])
    mem_v = _row(sv["kvm"], HEAD_DIM, HEADS, 1)
    dqm, dkm, dvm = attn_bwd(f"{tag}_dmemattn", _row(sv["qm"], HEAD_DIM, 0, 1), _row(sv["km"], HEAD_DIM, 0, 1), mem_v,
                             sv["branches"][3], d_br[3], sv["lse_d"], scale=HEAD_DIM ** -0.5, mask=None)
    mq0 = _seg("memq")[0] // HEAD_DIM
    (du_memq,), (g["mem_q_hnorm"],) = rowwise_bwd(
        f"{tag}_dmemq", fn_rms, [_row(u, HEAD_DIM, mq0, 1)], [p["mem_q_hnorm"]], [_row(dqm, HEAD_DIM, 0, 1)],
        [_out(BRANCH_WIDTH, bf16, HEAD_DIM, 0, 1)], H=HEADS)
    (dkvm_k, dkvm_v), (g["mem_k_hnorm"],) = rowwise_bwd(
        f"{tag}_dmemk", fn_mem_k_v, [_row(sv["kvm"], HEAD_DIM, 0, 1), mem_v], [p["mem_k_hnorm"]],
        [_row(dkm, HEAD_DIM, 0, 1), _row(dvm, HEAD_DIM, 0, 1)],
        [_out(BRANCH_WIDTH, bf16, HEAD_DIM, 0, 1), _out(BRANCH_WIDTH, bf16, HEAD_DIM, 0, 1)], H=HEADS)
    dkvm = jnp.concatenate([dkvm_k, dkvm_v], axis=1)
    g["mem_w_kv"] = mm(f"{tag}_dwmemkv", sv["mem_n"], dkvm, "tn")
    dmem_n = mm(f"{tag}_dmemn", dkvm, p["mem_w_kv"], "nt", out_dtype=ACT)
    _, (g["mem_norm"],) = rowwise_bwd(f"{tag}_dmemrms", fn_rms, [_row(mem_n_in)], [p["mem_norm"]], [_row(dmem_n)],
                                      [_out(D_MODEL, bf16)])
    du_krope_b = du_krope.astype(bf16)
    du = jnp.concatenate(list(d_gates) + [du_conv, du_sbq, du_sbk, dv_sb.astype(bf16), du_qlat, du_kvlat, du_memq,
                                          du_krope_b, jnp.zeros((S, U_WIDTH - _seg("krope")[0] - HEAD_DIM), bf16)], axis=1)
    g["w_in"] = mm(f"{tag}_dwin", sv["h"], du, "tn")
    dh = mm(f"{tag}_dh", du, p["w_in"], "nt", out_dtype=ACT)
    (dx,), (g["mix_norm"],) = rowwise_bwd(f"{tag}_drms", fn_rms_res, [_row(sv["x"])], [p["mix_norm"]],
                                          [_row(dy), _row(dh)], [_out(D_MODEL, f32)])
    return dx, g


def _interleave(name, a, b):
    S, W = a.shape
    T = min(512, S)

    def body(a_ref, b_ref, o_ref):
        o_ref[:, :HEAD_DIM] = a_ref[...]
        o_ref[:, HEAD_DIM:] = b_ref[...]

    blk = pl.BlockSpec((T, HEAD_DIM), lambda i, h: (i, h))
    return pl.pallas_call(
        body, name=name, grid=(S // T, W // HEAD_DIM), in_specs=[blk, blk],
        out_specs=pl.BlockSpec((T, 2 * HEAD_DIM), lambda i, h: (i, h)),
        out_shape=jax.ShapeDtypeStruct((S, 2 * W), a.dtype), compiler_params=_params(("parallel", "parallel")))(a, b)


def _u_layout(w):
    parts, at = [], 0
    for _, src, width, start in _U_SEGS:
        assert start == at
        parts.append(w[..., src:src + width])
        at += width
    parts.append(jnp.zeros(w.shape[:-1] + (U_WIDTH - at,), w.dtype))
    return jnp.concatenate(parts, axis=-1)


def _u_layout_inv(g):
    order = sorted(_U_SEGS, key=lambda s: s[1])
    return jnp.concatenate([g[..., start:start + width] for _, _, width, start in order], axis=-1)


def _pad_heads(w, n=MLA_QK, to=MLA_PAD):
    w = w.reshape(w.shape[:-1] + (HEADS, n))
    w = jnp.pad(w, [(0, 0)] * (w.ndim - 1) + [(0, to - n)])
    return w.reshape(w.shape[:-2] + (HEADS * to,))


def _unpad_heads(g, n=MLA_QK, to=MLA_PAD):
    g = g.reshape(g.shape[:-1] + (HEADS, to))[..., :n]
    return g.reshape(g.shape[:-2] + (HEADS * n,))


def layer_params(W, l):
    row = lambda name: W[name][l].reshape(1, -1).astype(f32)
    p = {n: row(n) for n in SMALL if n != "mla_q_hnorm" and n != "mla_k_hnorm"}
    for n in ("mla_q_hnorm", "mla_k_hnorm"):
        p[n] = jnp.pad(row(n), ((0, 0), (0, MLA_PAD - MLA_QK)))
    for n in ("ffn1_w_in", "ffn1_w_out", "ffn2_w_in", "ffn2_w_out", "mla_w_ukv", "mem_w_kv", "w_out"):
        p[n] = W[n][l]
    p["w_branch"] = [W["w_branch"][l, b] for b in range(N_BRANCH)]
    p["w_in"] = _u_layout(W["w_in"][l])
    p["mla_w_uq"] = _pad_heads(W["mla_w_uq"][l])
    p["conv_dw"] = jnp.pad(W["conv_dw"][l].astype(f32), ((0, 1), (0, 0)))
    return p


def layer_grads_to_original(g):
    out = dict(g)
    out["w_in"] = _u_layout_inv(g["w_in"])
    out["mla_w_uq"] = _unpad_heads(g["mla_w_uq"])
    out["conv_dw"] = g["conv_dw"][:CONV_WIDTH]
    out["w_branch"] = jnp.stack(g["w_branch"])
    for n in ("mla_q_hnorm", "mla_k_hnorm"):
        out[n] = g[n][:, :MLA_QK]
    return {n: (out[n].reshape(-1) if n in SMALL else out[n]) for n in out}


def local_step(x, mem, pos_col, target, W):
    tabs = rope_tables(pos_col)
    params = [layer_params(W, l) for l in range(DEPTH)]
    saved = []
    for l, p in enumerate(params):
        x, s1 = ffn_fwd(f"l{l}_ffn1", x, p["ffn1_norm"], p["ffn1_w_in"], p["ffn1_w_out"])
        x, s2 = mix_fwd(f"l{l}_mix", x, mem, tabs, p)
        x, s3 = ffn_fwd(f"l{l}_ffn2", x, p["ffn2_norm"], p["ffn2_w_in"], p["ffn2_w_out"])
        saved.append((s1, s2, s3))
    dx, loss_blk = loss_head(x, target)
    grads = [None] * DEPTH
    for l in reversed(range(DEPTH)):
        p, (s1, s2, s3) = params[l], saved[l]
        dx, g_n2, g_in2, g_out2 = ffn_bwd(f"l{l}_ffn2", s3, p["ffn2_norm"], p["ffn2_w_in"], p["ffn2_w_out"], dx)
        dx, g = mix_bwd(f"l{l}_mix", s2, mem, tabs, p, dx)
        dx, g_n1, g_in1, g_out1 = ffn_bwd(f"l{l}_ffn1", s1, p["ffn1_norm"], p["ffn1_w_in"], p["ffn1_w_out"], dx)
        g.update(ffn1_norm=g_n1, ffn1_w_in=g_in1, ffn1_w_out=g_out1, ffn2_norm=g_n2, ffn2_w_in=g_in2, ffn2_w_out=g_out2)
        grads[l] = layer_grads_to_original(g)
    stacked = {n: jnp.stack([grads[l][n] for l in range(DEPTH)]) for n in WEIGHTS}
    return loss_blk, dx, stacked


_ANY = pl.BlockSpec(memory_space=pl.ANY)
_COMM = pltpu.CompilerParams(has_side_effects=True)


def _coords():
    return lax.axis_index("x"), lax.axis_index("y"), lax.axis_index("c")


def chip_exchange(name, src, scatter):
    shape = src.shape[1:] if scatter else src.shape

    def body(src_ref, out_ref, send_sems, recv_sems):
        x, y, c = _coords()
        copies = []
        for k, (px, py) in enumerate([(1 - x, y), (x, 1 - y), (1 - x, 1 - y)]):
            piece = src_ref.at[2 * px + py] if scatter else src_ref
            cp = pltpu.make_async_remote_copy(piece, out_ref.at[k], send_sems.at[k], recv_sems.at[k],
                                              device_id=(px, py, c), device_id_type=MESH)
            cp.start()
            copies.append(cp)
        for cp in copies:
            cp.wait()

    return pl.pallas_call(
        body, name=name, in_specs=[_ANY], out_specs=_ANY, out_shape=jax.ShapeDtypeStruct((3,) + shape, src.dtype),
        scratch_shapes=[pltpu.SemaphoreType.DMA((3,)), pltpu.SemaphoreType.DMA((3,))], compiler_params=_COMM)(src)


def sibling_exchange(name, src):
    def body(src_ref, out_ref, send_sem, recv_sem):
        x, y, c = _coords()
        cp = pltpu.make_async_remote_copy(src_ref, out_ref, send_sem, recv_sem, device_id=(x, y, 1 - c),
                                          device_id_type=MESH)
        cp.start()
        cp.wait()

    return pl.pallas_call(
        body, name=name, in_specs=[_ANY], out_specs=_ANY, out_shape=jax.ShapeDtypeStruct(src.shape, src.dtype),
        scratch_shapes=[pltpu.SemaphoreType.DMA, pltpu.SemaphoreType.DMA], compiler_params=_COMM)(src)


def all8_gather(name, src):
    def body(src_ref, out_ref, send_sems, recv_sems, local_sem):
        x, y, c = _coords()
        me = 4 * x + 2 * y + c
        mine = pltpu.make_async_copy(src_ref, out_ref.at[me], local_sem)
        mine.start()
        copies = []
        for k in range(1, 8):
            peer = (1 - x if k & 4 else x, 1 - y if k & 2 else y, 1 - c if k & 1 else c)
            cp = pltpu.make_async_remote_copy(src_ref, out_ref.at[me], send_sems.at[k - 1], recv_sems.at[k - 1],
                                              device_id=peer, device_id_type=MESH)
            cp.start()
            copies.append(cp)
        for cp in copies:
            cp.wait()
        mine.wait()

    return pl.pallas_call(
        body, name=name, in_specs=[_ANY], out_specs=_ANY, out_shape=jax.ShapeDtypeStruct((8,) + src.shape, src.dtype),
        scratch_shapes=[pltpu.SemaphoreType.DMA((7,)), pltpu.SemaphoreType.DMA((7,)), pltpu.SemaphoreType.DMA],
        compiler_params=_COMM)(src)


def sum8(name, g):
    def body(g_ref, o_ref):
        acc = g_ref[0]
        for k in range(1, 8):
            acc = acc + g_ref[k]
        o_ref[...] = acc

    return pl.pallas_call(body, name=name, out_shape=jax.ShapeDtypeStruct(g.shape[1:], g.dtype))(g)


PACK_COLS = 1024


def _slots(name, n):
    return 2 * n if name == "conv_dw" else n


def _pack(parts, dtype):
    flat = jnp.concatenate([parts[n].astype(dtype) for n in SHARDED])
    total = -(-flat.shape[0] // PACK_GRAIN) * PACK_GRAIN
    flat = jnp.pad(flat, (0, total - flat.shape[0]))
    return flat.reshape(2, total // (2 * PACK_COLS), PACK_COLS)


def _unpack(packed, shard_shapes):
    flat = packed.reshape(-1)
    out, at = {}, 0
    for n in SHARDED:
        size = _slots(n, math.prod(shard_shapes[n]))
        out[n] = flat[at:at + size]
        at += size
    return out


def _split_shards(a, axis):
    return jnp.split(a, 4, axis=axis)


def gather_weights(w):
    x, y, c = _coords()
    me = 2 * x + y
    parts = {}
    for n in SHARDED:
        if n == "conv_dw":
            parts[n] = lax.bitcast_convert_type(w[n], bf16).reshape(-1)
        else:
            parts[n] = w[n].astype(bf16).reshape(-1)
    own = _pack(parts, bf16)
    mine = lax.dynamic_index_in_dim(own, c, axis=0, keepdims=False)
    got = chip_exchange("ag_chips", mine, scatter=False)
    sib = sibling_exchange("ag_sibling", got)
    first = jnp.where(c == 0, got, sib)
    second = jnp.where(c == 0, sib, got)
    others = jnp.stack([first, second], axis=1)
    by_mask = jnp.stack([own, others[1], others[0], others[2]])
    shapes = {n: w[n].shape for n in SHARDED}
    pieces = [_unpack(lax.dynamic_index_in_dim(by_mask, jnp.bitwise_xor(s, me), axis=0, keepdims=False), shapes)
              for s in range(4)]
    full = {}
    for n in SHARDED:
        shards = []
        for s in range(4):
            p = pieces[s][n]
            if n == "conv_dw":
                p = lax.bitcast_convert_type(p.reshape(w[n].shape + (2,)), f32)
            shards.append(p.reshape(w[n].shape))
        full[n] = jnp.concatenate(shards, axis=SHARD_AXIS[n])
    return full


def reduce_grads(g, shard_shapes):
    x, y, c = _coords()
    me = 2 * x + y
    streams = []
    for s in range(4):
        parts = {}
        for n in SHARDED:
            p = _split_shards(g[n], SHARD_AXIS[n])[s].reshape(-1)
            parts[n] = jnp.pad(p, (0, p.shape[0])) if n == "conv_dw" else p
        streams.append(_pack(parts, f32))
    G = jnp.stack(streams)
    R = G.shape[2]
    rows2 = lambda a: a.reshape(-1, PACK_COLS)
    theirs = lax.dynamic_index_in_dim(G, 1 - c, axis=1, keepdims=False).astype(bf16)
    from_sib = sibling_exchange("rs_sibling", theirs)
    mine = lax.dynamic_index_in_dim(G, c, axis=1, keepdims=False)
    chip_sum, chip_sum_b = rowwise("rs_add_sibling", lambda a, b: (a + b.astype(f32),) * 2, [_row(rows2(mine)), _row(rows2(from_sib))],
                                   [], [_out(PACK_COLS, f32), _out(PACK_COLS, bf16)], T=256)
    got = chip_exchange("rs_chips", chip_sum_b.reshape(4, R, PACK_COLS), scatter=True)
    own = lax.dynamic_index_in_dim(chip_sum.reshape(4, R, PACK_COLS), me, axis=0, keepdims=False)
    (half,) = rowwise("rs_add_chips", lambda a, b0, b1, b2: (((a + b0.astype(f32)) + b1.astype(f32)) + b2.astype(f32),),
                      [_row(own), _row(got[0]), _row(got[1]), _row(got[2])], [], [_out(PACK_COLS, f32)], T=256)
    other = sibling_exchange("rs_final", half)
    shard = jnp.stack([jnp.where(c == 0, half, other), jnp.where(c == 0, other, half)])
    flat = _unpack(shard, shard_shapes)
    out = {}
    for n in SHARDED:
        k = math.prod(shard_shapes[n])
        out[n] = flat[n][:k].reshape(shard_shapes[n])
    return out


def _small_pack(t):
    flat = jnp.concatenate([t[n].reshape(-1) for n in SMALL])
    total = -(-flat.shape[0] // SMALL_PAD) * SMALL_PAD
    return jnp.pad(flat, (0, total - flat.shape[0])).reshape(-1, 128)


def _small_unpack(a, shapes):
    flat, out, at = a.reshape(-1), {}, 0
    for n in SMALL:
        k = math.prod(shapes[n])
        out[n] = flat[at:at + k].reshape(shapes[n])
        at += k
    return out


def adamw(name, w, g, m, v):
    shape = w.shape
    two = lambda a: a.reshape(-1, shape[-1])
    rows = two(w).shape[0]
    T = 256 if rows % 256 == 0 else rows
    outs = rowwise(name, fn_adamw, [_row(two(w)), _row(two(g)), _row(two(m)), _row(two(v))], [],
                   [_out(shape[-1], f32)] * 3, T=T)
    return [o.reshape(shape) for o in outs]


def kernel(x, mem, positions, ffn1_norm, ffn1_w_in, ffn1_w_out, mix_norm, w_in, conv_dw, conv_b, conv_ln_g, conv_ln_b, sb_q_hnorm, sb_k_hnorm, mla_q_norm, mla_w_uq, mla_kv_norm, mla_w_ukv, mla_q_hnorm, mla_k_hnorm, mem_norm, mem_w_kv, mem_q_hnorm, mem_k_hnorm, w_branch, w_out, ffn2_norm, ffn2_w_in, ffn2_w_out, loss_target, m_ffn1_norm, m_ffn1_w_in, m_ffn1_w_out, m_mix_norm, m_w_in, m_conv_dw, m_conv_b, m_conv_ln_g, m_conv_ln_b, m_sb_q_hnorm, m_sb_k_hnorm, m_mla_q_norm, m_mla_w_uq, m_mla_kv_norm, m_mla_w_ukv, m_mla_q_hnorm, m_mla_k_hnorm, m_mem_norm, m_mem_w_kv, m_mem_q_hnorm, m_mem_k_hnorm, m_w_branch, m_w_out, m_ffn2_norm, m_ffn2_w_in, m_ffn2_w_out, v_ffn1_norm, v_ffn1_w_in, v_ffn1_w_out, v_mix_norm, v_w_in, v_conv_dw, v_conv_b, v_conv_ln_g, v_conv_ln_b, v_sb_q_hnorm, v_sb_k_hnorm, v_mla_q_norm, v_mla_w_uq, v_mla_kv_norm, v_mla_w_ukv, v_mla_q_hnorm, v_mla_k_hnorm, v_mem_norm, v_mem_w_kv, v_mem_q_hnorm, v_mem_k_hnorm, v_w_branch, v_w_out, v_ffn2_norm, v_ffn2_w_in, v_ffn2_w_out):
    w = dict(zip(WEIGHTS, (ffn1_norm, ffn1_w_in, ffn1_w_out, mix_norm, w_in, conv_dw, conv_b, conv_ln_g, conv_ln_b, sb_q_hnorm, sb_k_hnorm, mla_q_norm, mla_w_uq, mla_kv_norm, mla_w_ukv, mla_q_hnorm, mla_k_hnorm, mem_norm, mem_w_kv, mem_q_hnorm, mem_k_hnorm, w_branch, w_out, ffn2_norm, ffn2_w_in, ffn2_w_out)))
    m = dict(zip(WEIGHTS, (m_ffn1_norm, m_ffn1_w_in, m_ffn1_w_out, m_mix_norm, m_w_in, m_conv_dw, m_conv_b, m_conv_ln_g, m_conv_ln_b, m_sb_q_hnorm, m_sb_k_hnorm, m_mla_q_norm, m_mla_w_uq, m_mla_kv_norm, m_mla_w_ukv, m_mla_q_hnorm, m_mla_k_hnorm, m_mem_norm, m_mem_w_kv, m_mem_q_hnorm, m_mem_k_hnorm, m_w_branch, m_w_out, m_ffn2_norm, m_ffn2_w_in, m_ffn2_w_out)))
    v = dict(zip(WEIGHTS, (v_ffn1_norm, v_ffn1_w_in, v_ffn1_w_out, v_mix_norm, v_w_in, v_conv_dw, v_conv_b, v_conv_ln_g, v_conv_ln_b, v_sb_q_hnorm, v_sb_k_hnorm, v_mla_q_norm, v_mla_w_uq, v_mla_kv_norm, v_mla_w_ukv, v_mla_q_hnorm, v_mla_k_hnorm, v_mem_norm, v_mem_w_kv, v_mem_q_hnorm, v_mem_k_hnorm, v_w_branch, v_w_out, v_ffn2_norm, v_ffn2_w_in, v_ffn2_w_out)))
    S = x.shape[1]
    full = gather_weights(w)
    full.update({n: w[n] for n in SMALL})
    loss_blk, dx, g = local_step(x[0], mem[0], positions.reshape(S, 1), loss_target[0], full)
    loss = lax.psum(loss_blk[0, 0], ("x", "y", "c"))
    grads = reduce_grads(g, {n: w[n].shape for n in SHARDED})
    small_shapes = {n: w[n].shape for n in SMALL}
    g_small = sum8("small_sum", all8_gather("small_gather", _small_pack(g)))
    grads.update(_small_unpack(g_small, small_shapes))
    delta, new_m, new_v = {}, {}, {}
    for n in SHARDED:
        delta[n], new_m[n], new_v[n] = adamw(f"adamw_{n}", w[n], grads[n], m[n], v[n])
    d_s, m_s, v_s = adamw("adamw_small", _small_pack(w), g_small, _small_pack(m), _small_pack(v))
    for t, packed in ((delta, d_s), (new_m, m_s), (new_v, v_s)):
        t.update(_small_unpack(packed, small_shapes))
    return (loss, dx.reshape(x.shape), *[grads[n] for n in WEIGHTS], *[delta[n] for n in WEIGHTS],
            *[new_m[n] for n in WEIGHTS], *[new_v[n] for n in WEIGHTS])
```

```python
import math

import jax
import jax.numpy as jnp
from jax import lax
from jax.experimental import pallas as pl
from jax.experimental.pallas import tpu as pltpu

f32, bf16 = jnp.float32, jnp.bfloat16

D_MODEL = 1024
DEPTH = 4
CHUNK = 64
FFN_HIDDEN = 2048
CONV_CH = 512
CONV_WIDTH = 31
HEADS = 4
HEAD_DIM = 128
MLA_NOPE = 128
MLA_ROPE = 64
MLA_QK = MLA_NOPE + MLA_ROPE
MLA_PAD = 256
MLA_Q_LORA = 256
MLA_KV_LORA = 256
N_BRANCH = 4
BRANCH_WIDTH = 512
ROPE_BASE = 10000.0
EPS = 1e-6
NEG_INF = -1e30
IN_WIDTH = 7744
U_WIDTH = 8192
_U_SEGS = (("gates", 3648, 4096, 0), ("conv", 0, 1024, 4096), ("sb", 1024, 1536, 5120), ("qlat", 2560, 256, 6656),
           ("kvlat", 2816, 256, 6912), ("memq", 3136, 512, 7168), ("krope", 3072, 64, 7680))
U_PAD_FROM = 7744

ADAM_LR, ADAM_B1, ADAM_B2, ADAM_EPS, ADAM_WD, ADAM_STEP = 0.001, 0.9, 0.999, 1e-08, 0.01, 10

VMEM_LIMIT = 48 * 1024 * 1024
MESH = pl.DeviceIdType.MESH

SHARDED = ("ffn1_w_in", "ffn1_w_out", "w_in", "conv_dw", "mla_w_uq", "mla_w_ukv", "mem_w_kv", "w_branch", "w_out",
           "ffn2_w_in", "ffn2_w_out")
SHARD_AXIS = {"ffn1_w_in": 2, "ffn1_w_out": 1, "w_in": 2, "conv_dw": 2, "mla_w_uq": 2, "mla_w_ukv": 2, "mem_w_kv": 1,
              "w_branch": 3, "w_out": 1, "ffn2_w_in": 2, "ffn2_w_out": 1}
SMALL = ("ffn1_norm", "mix_norm", "conv_b", "conv_ln_g", "conv_ln_b", "sb_q_hnorm", "sb_k_hnorm", "mla_q_norm",
         "mla_kv_norm", "mla_q_hnorm", "mla_k_hnorm", "mem_norm", "mem_q_hnorm", "mem_k_hnorm", "ffn2_norm")
WEIGHTS = ("ffn1_norm", "ffn1_w_in", "ffn1_w_out", "mix_norm", "w_in", "conv_dw", "conv_b", "conv_ln_g", "conv_ln_b",
           "sb_q_hnorm", "sb_k_hnorm", "mla_q_norm", "mla_w_uq", "mla_kv_norm", "mla_w_ukv", "mla_q_hnorm",
           "mla_k_hnorm", "mem_norm", "mem_w_kv", "mem_q_hnorm", "mem_k_hnorm", "w_branch", "w_out", "ffn2_norm",
           "ffn2_w_in", "ffn2_w_out")
PACK_GRAIN = 2 * 256 * 1024
SMALL_PAD = 8 * 128


def _params(sem, vmem=VMEM_LIMIT):
    return pltpu.CompilerParams(dimension_semantics=sem, vmem_limit_bytes=vmem)


def _pick(n, pref):
    for t in pref:
        if n % t == 0:
            return t
    return n


def mm(name, a, b, form, *, out_dtype=f32, alpha=1.0, res=None, tm=None, tn=None, tk=None):
    if form == "nn":
        (M, K), (K2, N) = a.shape, b.shape
    elif form == "nt":
        (M, K), (N, K2) = a.shape, b.shape
    else:
        (K, M), (K2, N) = a.shape, b.shape
    assert K == K2, (name, a.shape, b.shape)
    tm = tm or _pick(M, (1024, 512, 256, 128))
    tn = tn or _pick(N, (1024, 512, 256, 128))
    tk = tk or (_pick(K, (512,)) if form == "tn" else _pick(K, (2048, 1024, 512, 256)))
    nk = K // tk
    if form == "nn":
        a_spec = pl.BlockSpec((tm, tk), lambda i, j, k: (i, k))
        b_spec = pl.BlockSpec((tk, tn), lambda i, j, k: (k, j))
        dims = (((1,), (0,)), ((), ()))
    elif form == "nt":
        a_spec = pl.BlockSpec((tm, tk), lambda i, j, k: (i, k))
        b_spec = pl.BlockSpec((tn, tk), lambda i, j, k: (j, k))
        dims = (((1,), (1,)), ((), ()))
    else:
        a_spec = pl.BlockSpec((tk, tm), lambda i, j, k: (k, i))
        b_spec = pl.BlockSpec((tk, tn), lambda i, j, k: (k, j))
        dims = (((0,), (0,)), ((), ()))
    o_spec = pl.BlockSpec((tm, tn), lambda i, j, k: (i, j))
    has_res = res is not None

    def body(a_ref, b_ref, *rest):
        if has_res:
            r_ref, o_ref, acc_ref = rest
        else:
            o_ref, acc_ref = rest
        k = pl.program_id(2)
        part = lax.dot_general(a_ref[...].astype(bf16), b_ref[...].astype(bf16), dims, preferred_element_type=f32)

        def finish(acc):
            r = acc if alpha == 1.0 else acc * alpha
            if has_res:
                r = r_ref[...].astype(f32) + r
            o_ref[...] = r.astype(o_ref.dtype)

        if nk == 1:
            finish(part)
        else:
            @pl.when(k == 0)
            def _():
                acc_ref[...] = part

            @pl.when(k > 0)
            def _():
                acc_ref[...] += part

            @pl.when(k == nk - 1)
            def _():
                finish(acc_ref[...])

    ins = [a, b] + ([res] if has_res else [])
    in_specs = [a_spec, b_spec] + ([o_spec] if has_res else [])
    return pl.pallas_call(
        body, name=name, grid=(M // tm, N // tn, nk), in_specs=in_specs, out_specs=o_spec,
        out_shape=jax.ShapeDtypeStruct((M, N), out_dtype), scratch_shapes=[pltpu.VMEM((tm, tn), f32)],
        compiler_params=_params(("parallel", "parallel", "arbitrary")))(*ins)


def _row_spec(T, w, off, st):
    return pl.BlockSpec((T, w), lambda i, h: (i, off + st * h))


def _full_spec(p):
    return pl.BlockSpec(p.shape, lambda i, h: (0,) * p.ndim)


def rowwise(name, fn, rows, params, outs, *, T=512, H=1):
    S = rows[0][0].shape[0]
    T = min(T, S)
    n_r, n_p = len(rows), len(params)

    def body(*refs):
        vals = [r[...] for r in refs[:n_r + n_p]]
        res = fn(*vals)
        for o_ref, r in zip(refs[n_r + n_p:], res):
            o_ref[...] = r.astype(o_ref.dtype)

    res = pl.pallas_call(
        body, name=name, grid=(S // T, H),
        in_specs=[_row_spec(T, w, off, st) for (_, w, off, st) in rows] + [_full_spec(p) for p in params],
        out_specs=[_row_spec(T, w, off, st) for (_, _, w, off, st) in outs],
        out_shape=[jax.ShapeDtypeStruct((S, tw), dt) for (tw, dt, _, _, _) in outs],
        compiler_params=_params(("parallel", "arbitrary")))(*[r[0] for r in rows], *params)
    return res


def rowwise_bwd(name, fn, rows, params, douts, drows, *, T=512, H=1, nondiff=()):
    S = rows[0][0].shape[0]
    T = min(T, S)
    n_r, n_p, n_d = len(rows), len(params), len(douts)
    diff_idx = [k for k in range(n_r) if k not in nondiff]
    shared = [H > 1 and rows[k][3] == 0 for k in diff_idx]

    def body(*refs):
        i, h = pl.program_id(0), pl.program_id(1)
        row_vals = [r[...].astype(f32) for r in refs[:n_r]]
        p_vals = [r[...].astype(f32) for r in refs[n_r:n_r + n_p]]
        d_vals = [r[...].astype(f32) for r in refs[n_r + n_p:n_r + n_p + n_d]]
        out_refs = refs[n_r + n_p + n_d:]

        def f(*args):
            full = list(row_vals)
            for k, v in zip(diff_idx, args[:len(diff_idx)]):
                full[k] = v
            return tuple(fn(*full, *args[len(diff_idx):]))

        _, vjp = jax.vjp(f, *[row_vals[k] for k in diff_idx], *p_vals)
        cts = vjp(tuple(d_vals))
        for o_ref, ct, sh in zip(out_refs[:len(diff_idx)], cts[:len(diff_idx)], shared):
            if sh:
                @pl.when(h == 0)
                def _():
                    o_ref[...] = ct.astype(o_ref.dtype)

                @pl.when(h > 0)
                def _():
                    o_ref[...] += ct.astype(o_ref.dtype)
            else:
                o_ref[...] = ct.astype(o_ref.dtype)
        first = jnp.logical_and(i == 0, h == 0)
        for o_ref, ct in zip(out_refs[len(diff_idx):], cts[len(diff_idx):]):
            @pl.when(first)
            def _():
                o_ref[...] = ct

            @pl.when(jnp.logical_not(first))
            def _():
                o_ref[...] += ct

    res = pl.pallas_call(
        body, name=name, grid=(S // T, H),
        in_specs=[_row_spec(T, w, off, st) for (_, w, off, st) in rows] + [_full_spec(p) for p in params]
        + [_row_spec(T, w, off, st) for (_, w, off, st) in douts],
        out_specs=[_row_spec(T, w, off, st) for (_, _, w, off, st) in drows] + [_full_spec(p) for p in params],
        out_shape=[jax.ShapeDtypeStruct((S, tw), dt) for (tw, dt, _, _, _) in drows]
        + [jax.ShapeDtypeStruct(p.shape, f32) for p in params],
        compiler_params=_params(("arbitrary", "arbitrary")))(*[r[0] for r in rows], *params, *[d[0] for d in douts])
    return res[:len(diff_idx)], res[len(diff_idx):]


def _rms(x, g, n=None):
    x = x.astype(f32)
    n = n or x.shape[-1]
    return x * lax.rsqrt(jnp.sum(x * x, axis=-1, keepdims=True) * (1.0 / n) + EPS) * g.astype(f32)


def _sigmoid(x):
    return 1.0 / (1.0 + jnp.exp(-x))


def _silu(x):
    return x * _sigmoid(x)


def fn_rms(x, g):
    return (_rms(x, g),)


def fn_rms_res(x, g):
    return (x.astype(f32), _rms(x, g))


def fn_swiglu(gate, up):
    return (_silu(gate.astype(f32)) * up.astype(f32),)


def fn_sb_prep(q, k, gq, gk):
    return (_rms(q, gq), _rms(k, gk))


def fn_ln_silu(y, b, g, beta):
    y = y.astype(f32) + b
    mu = jnp.mean(y, axis=-1, keepdims=True)
    var = jnp.mean(jnp.square(y - mu), axis=-1, keepdims=True)
    return (_silu((y - mu) * lax.rsqrt(var + EPS) * g + beta),)


def fn_merge(g0, g1, g2, g3, p0, p1, p2, p3):
    out = _sigmoid(g0.astype(f32)) * p0.astype(f32)
    for g, p in ((g1, p1), (g2, p2), (g3, p3)):
        out = out + _sigmoid(g.astype(f32)) * p.astype(f32)
    return (out,)


def _rot_fwd(x):
    z = jnp.zeros_like(x[:, :MLA_NOPE])
    h = MLA_ROPE // 2
    return jnp.concatenate([z, -x[:, MLA_NOPE + h:MLA_QK], x[:, MLA_NOPE:MLA_NOPE + h], z[:, :MLA_PAD - MLA_QK]], axis=-1)


def _rot_bwd(g):
    z = jnp.zeros_like(g[:, :MLA_NOPE])
    h = MLA_ROPE // 2
    return jnp.concatenate([z, g[:, MLA_NOPE + h:MLA_QK], -g[:, MLA_NOPE:MLA_NOPE + h], z[:, :MLA_PAD - MLA_QK]], axis=-1)


@jax.custom_vjp
def _rope(x, c, s):
    return x * c + _rot_fwd(x) * s


def _rope_f(x, c, s):
    return _rope(x, c, s), (c, s)


def _rope_b(res, g):
    c, s = res
    return g * c + _rot_bwd(g * s), jnp.zeros_like(c), jnp.zeros_like(s)


_rope.defvjp(_rope_f, _rope_b)


def fn_mla_q(q, c, s, gain):
    return (_rope(_rms(q, gain, MLA_QK), c, s),)


def fn_mla_k(kn, kr, c, s, gain):
    k = jnp.concatenate([kn.astype(f32), kr.astype(f32)], axis=-1)
    return (_rope(_rms(k, gain, MLA_QK), c, s),)


def fn_mla_k_v(kn, kr, v, c, s, gain):
    return (fn_mla_k(kn, kr, c, s, gain)[0], v.astype(f32))


def fn_mem_k_v(k, v, gain):
    return (_rms(k, gain), v.astype(f32))


def fn_adamw(w, g, m, v):
    m = ADAM_B1 * m + (1.0 - ADAM_B1) * g
    v = ADAM_B2 * v + (1.0 - ADAM_B2) * jnp.square(g)
    m_hat = m / (1.0 - ADAM_B1 ** ADAM_STEP)
    v_hat = v / (1.0 - ADAM_B2 ** ADAM_STEP)
    delta = -ADAM_LR * (m_hat / (jnp.sqrt(v_hat) + ADAM_EPS) + ADAM_WD * w)
    return delta, m, v


def _head_spec(rows, w, off, st):
    return pl.BlockSpec((rows, w), lambda h, i: (0, off + st * h))


def _qblk_spec(B, w, off, st):
    return pl.BlockSpec((B, w), lambda h, i: (i, off + st * h))


def _chunk_mask(tq, tk, d):
    r = lax.broadcasted_iota(jnp.int32, (tq, tk), 0) // CHUNK
    c = (d * tk + lax.broadcasted_iota(jnp.int32, (tq, tk), 1)) // CHUNK
    return c <= r


def _strict_mask(tq, tk, d):
    r = lax.broadcasted_iota(jnp.int32, (tq, tk), 0)
    c = d * tk + lax.broadcasted_iota(jnp.int32, (tq, tk), 1)
    return c < r


_NT = (((1,), (1,)), ((), ()))
_TN = (((0,), (0,)), ((), ()))


def _tiles(Sq, Sk, mask, tq, tk):
    tq = min(tq, Sq)
    if mask is None:
        return tq, Sk, 0
    tk = min(tk, tq)
    assert Sq == Sk and tq % tk == 0 and tk % CHUNK == 0
    return tq, tk, tq // tk


def attn_fwd(name, q, k, v, *, scale, mask, tq=1024, tk=512):
    Sq, Sk = q[0].shape[0], k[0].shape[0]
    tq, tk, nd = _tiles(Sq, Sk, mask, tq, tk)

    def body(q_ref, k_ref, v_ref, o_ref, lse_ref):
        i = pl.program_id(1)
        qv = q_ref[...]

        def block(off, carry, d):
            m, l, acc = carry
            kb, vb = k_ref[pl.ds(off, tk), :].astype(bf16), v_ref[pl.ds(off, tk), :].astype(bf16)
            s = lax.dot_general(qv, kb, _NT, preferred_element_type=f32) * scale
            if d is not None:
                s = jnp.where(_chunk_mask(tq, tk, d), s, NEG_INF)
            m_new = jnp.maximum(m, jnp.max(s, axis=-1, keepdims=True))
            p = jnp.exp(s - m_new)
            corr = jnp.exp(m - m_new)
            l = l * corr + jnp.sum(p, axis=-1, keepdims=True)
            acc = acc * corr + jnp.dot(p.astype(bf16), vb, preferred_element_type=f32)
            return m_new, l, acc

        carry = (jnp.full((tq, 1), NEG_INF, f32), jnp.zeros((tq, 1), f32), jnp.zeros((tq, HEAD_DIM), f32))
        if nd:
            carry = lax.fori_loop(0, i * nd, lambda j, c: block(pl.multiple_of(j * tk, tk), c, None), carry)
            for d in range(nd):
                carry = block(pl.multiple_of(i * tq + d * tk, tk), carry, d)
        else:
            carry = block(0, carry, None)
        m, l, acc = carry
        o_ref[...] = (acc / l).astype(o_ref.dtype)
        lse_ref[...] = jnp.broadcast_to(m + jnp.log(l), (tq, HEAD_DIM))

    return pl.pallas_call(
        body, name=name, grid=(HEADS, Sq // tq),
        in_specs=[_qblk_spec(tq, *q[1:]), _head_spec(Sk, *k[1:]), _head_spec(Sk, *v[1:])],
        out_specs=[_qblk_spec(tq, HEAD_DIM, 0, 1), _qblk_spec(tq, HEAD_DIM, 0, 1)],
        out_shape=[jax.ShapeDtypeStruct((Sq, HEADS * HEAD_DIM), f32), jax.ShapeDtypeStruct((Sq, HEADS * HEAD_DIM), f32)],
        compiler_params=_params(("parallel", "arbitrary")))(q[0], k[0], v[0])


def attn_bwd(name, q, k, v, o, do, lse, *, scale, mask, tq=1024, tk=512):
    Sq, Sk = q[0].shape[0], k[0].shape[0]
    tq, tk, nd = _tiles(Sq, Sk, mask, tq, tk)
    dq_w = q[1]

    def body(q_ref, k_ref, v_ref, o_ref, do_ref, lse_ref, dq_ref, dk_ref, dv_ref):
        i = pl.program_id(1)

        @pl.when(i == 0)
        def _():
            dk_ref[...] = jnp.zeros_like(dk_ref)
            dv_ref[...] = jnp.zeros_like(dv_ref)

        qv, dov = q_ref[...], do_ref[...].astype(bf16)
        delta = jnp.sum(do_ref[...].astype(f32) * o_ref[...].astype(f32), axis=-1, keepdims=True)
        lse_v = lse_ref[:, :1]

        def block(off, dq_acc, d):
            kb, vb = k_ref[pl.ds(off, tk), :].astype(bf16), v_ref[pl.ds(off, tk), :].astype(bf16)
            s = lax.dot_general(qv, kb, _NT, preferred_element_type=f32) * scale
            if d is not None:
                s = jnp.where(_chunk_mask(tq, tk, d), s, NEG_INF)
            p = jnp.exp(s - lse_v)
            dv_ref[pl.ds(off, tk), :] += lax.dot_general(p.astype(bf16), dov, _TN, preferred_element_type=f32)
            dp = lax.dot_general(dov, vb, _NT, preferred_element_type=f32)
            ds = (p * (dp - delta) * scale).astype(bf16)
            dk_ref[pl.ds(off, tk), :] += lax.dot_general(ds, qv, _TN, preferred_element_type=f32)
            return dq_acc + jnp.dot(ds, kb, preferred_element_type=f32)

        acc = jnp.zeros((tq, dq_w), f32)
        if nd:
            acc = lax.fori_loop(0, i * nd, lambda j, c: block(pl.multiple_of(j * tk, tk), c, None), acc)
            for d in range(nd):
                acc = block(pl.multiple_of(i * tq + d * tk, tk), acc, d)
        else:
            acc = block(0, acc, None)
        dq_ref[...] = acc.astype(dq_ref.dtype)

    hd = _qblk_spec(tq, HEAD_DIM, 0, 1)
    return pl.pallas_call(
        body, name=name, grid=(HEADS, Sq // tq),
        in_specs=[_qblk_spec(tq, *q[1:]), _head_spec(Sk, *k[1:]), _head_spec(Sk, *v[1:]), hd, hd, hd],
        out_specs=[_qblk_spec(tq, dq_w, 0, 1), _head_spec(Sk, dq_w, 0, 1), _head_spec(Sk, HEAD_DIM, 0, 1)],
        out_shape=[jax.ShapeDtypeStruct((Sq, HEADS * dq_w), f32), jax.ShapeDtypeStruct((Sk, HEADS * dq_w), f32),
                   jax.ShapeDtypeStruct((Sk, HEADS * HEAD_DIM), f32)],
        compiler_params=_params(("arbitrary", "arbitrary"), 56 * 1024 * 1024))(q[0], k[0], v[0], o, do, lse)


SB_LOG_ZERO = -104.0


def _tri(B, rel):
    r = lax.broadcasted_iota(jnp.int32, (B, B), 0)
    c = lax.broadcasted_iota(jnp.int32, (B, B), 1)
    return rel(r, c).astype(bf16)


def _sb_scores(qv, kb, scale):
    z = lax.dot_general(qv, kb, _NT, preferred_element_type=f32) * scale
    e = jnp.exp(-jnp.abs(z))
    log_keep = -(jnp.maximum(z, 0.0) + jnp.log(1.0 + e))
    return z, e, log_keep


def _split_dot(x, m):
    hi = x.astype(bf16)
    lo = (x - hi.astype(f32)).astype(bf16)
    return jnp.dot(hi, m, preferred_element_type=f32) + jnp.dot(lo, m, preferred_element_type=f32)


def sb_fwd(name, q, k, v, *, tq=512, tk=256):
    S = q[0].shape[0]
    tq, tk, nd = _tiles(S, S, "strict", tq, tk)
    scale = HEAD_DIM ** -0.5
    m_ex = _tri(tk, lambda j, s: j > s)

    def body(q_ref, k_ref, v_ref, mex_ref, o_ref, tot_ref, cnt_ref):
        i = pl.program_id(1)
        qv, mex = q_ref[...], mex_ref[...]

        def block(off, carry, d):
            later, acc = carry
            kb, vb = k_ref[pl.ds(off, tk), :].astype(bf16), v_ref[pl.ds(off, tk), :].astype(bf16)
            z, _, lk = _sb_scores(qv, kb, scale)
            if d is not None:
                lk = jnp.where(_strict_mask(tq, tk, d), lk, 0.0)
            a = jnp.exp(z + lk + _split_dot(lk, mex) + later)
            if d is not None:
                a = jnp.where(_strict_mask(tq, tk, d), a, 0.0)
            acc = acc + jnp.dot(a.astype(bf16), vb, preferred_element_type=f32)
            return later + jnp.sum(lk, axis=-1, keepdims=True), acc

        carry = (jnp.zeros((tq, 1), f32), jnp.zeros((tq, HEAD_DIM), f32))
        for d in reversed(range(nd)):
            carry = block(pl.multiple_of(i * tq + d * tk, tk), carry, d)
        n_full = i * nd

        def more(state):
            t, later, _ = state
            return jnp.logical_and(t < n_full, jnp.max(later) > SB_LOG_ZERO)

        def step(state):
            t, later, acc = state
            later, acc = block(pl.multiple_of((n_full - 1 - t) * tk, tk), (later, acc), None)
            return t + 1, later, acc

        done, total, acc = lax.while_loop(more, step, (jnp.int32(0),) + carry)
        o_ref[...] = acc.astype(o_ref.dtype)
        tot_ref[...] = jnp.broadcast_to(total, (tq, HEAD_DIM))
        cnt_ref[...] = jnp.full((8, HEAD_DIM), done, f32)

    hd = _qblk_spec(tq, HEAD_DIM, 0, 1)
    return pl.pallas_call(
        body, name=name, grid=(HEADS, S // tq),
        in_specs=[_qblk_spec(tq, *q[1:]), _head_spec(S, *k[1:]), _head_spec(S, *v[1:]),
                  pl.BlockSpec((tk, tk), lambda h, i: (0, 0))],
        out_specs=[hd, hd, _qblk_spec(8, HEAD_DIM, 0, 1)],
        out_shape=[jax.ShapeDtypeStruct((S, HEADS * HEAD_DIM), f32), jax.ShapeDtypeStruct((S, HEADS * HEAD_DIM), f32),
                   jax.ShapeDtypeStruct((8 * (S // tq), HEADS * HEAD_DIM), f32)],
        compiler_params=_params(("parallel", "arbitrary")))(q[0], k[0], v[0], m_ex)


def sb_bwd(name, q, k, v, tot, cnt, do, *, tq=512, tk=256):
    S = q[0].shape[0]
    tq, tk, nd = _tiles(S, S, "strict", tq, tk)
    scale = HEAD_DIM ** -0.5
    m_le, m_lt = _tri(tk, lambda j, s: j <= s), _tri(tk, lambda j, s: j < s)

    def body(q_ref, k_ref, v_ref, tot_ref, cnt_ref, do_ref, mle_ref, mlt_ref, dq_ref, dk_ref, dv_ref):
        i = pl.program_id(1)

        @pl.when(i == 0)
        def _():
            dk_ref[...] = jnp.zeros_like(dk_ref)
            dv_ref[...] = jnp.zeros_like(dv_ref)

        qv, dov, mle, mlt = q_ref[...], do_ref[...].astype(bf16), mle_ref[...], mlt_ref[...]
        total = tot_ref[:, :1]

        def block(off, carry, d):
            before, g_before, dq_acc = carry
            kb, vb = k_ref[pl.ds(off, tk), :].astype(bf16), v_ref[pl.ds(off, tk), :].astype(bf16)
            z, e, lk = _sb_scores(qv, kb, scale)
            sig = jnp.where(z >= 0, 1.0, e) / (1.0 + e)
            if d is not None:
                lk = jnp.where(_strict_mask(tq, tk, d), lk, 0.0)
            later = (total - before) - _split_dot(lk, mle)
            a = jnp.exp(z + lk + later)
            if d is not None:
                a = jnp.where(_strict_mask(tq, tk, d), a, 0.0)
            g = a * lax.dot_general(dov, vb, _NT, preferred_element_type=f32)
            prefix = g_before + jnp.dot(g.astype(bf16), mlt, preferred_element_type=f32)
            dz = (g * (1.0 - sig) - prefix * sig) * scale
            if d is not None:
                dz = jnp.where(_strict_mask(tq, tk, d), dz, 0.0)
            dzb = dz.astype(bf16)
            dk_ref[pl.ds(off, tk), :] += lax.dot_general(dzb, qv, _TN, preferred_element_type=f32)
            dv_ref[pl.ds(off, tk), :] += lax.dot_general(a.astype(bf16), dov, _TN, preferred_element_type=f32)
            return (before + jnp.sum(lk, axis=-1, keepdims=True), g_before + jnp.sum(g, axis=-1, keepdims=True),
                    dq_acc + jnp.dot(dzb, kb, preferred_element_type=f32))

        zero = jnp.zeros((tq, 1), f32)
        first = i * nd - jnp.max(cnt_ref[...]).astype(jnp.int32)
        carry = lax.fori_loop(first, i * nd, lambda j, c: block(pl.multiple_of(j * tk, tk), c, None),
                              (zero, zero, jnp.zeros((tq, HEAD_DIM), f32)))
        for d in range(nd):
            carry = block(pl.multiple_of(i * tq + d * tk, tk), carry, d)
        dq_ref[...] = carry[2].astype(dq_ref.dtype)

    hd = _qblk_spec(tq, HEAD_DIM, 0, 1)
    tri = pl.BlockSpec((tk, tk), lambda h, i: (0, 0))
    return pl.pallas_call(
        body, name=name, grid=(HEADS, S // tq),
        in_specs=[_qblk_spec(tq, *q[1:]), _head_spec(S, *k[1:]), _head_spec(S, *v[1:]), hd, _qblk_spec(8, HEAD_DIM, 0, 1),
                  hd, tri, tri],
        out_specs=[hd, _head_spec(S, HEAD_DIM, 0, 1), _head_spec(S, HEAD_DIM, 0, 1)],
        out_shape=[jax.ShapeDtypeStruct((S, HEADS * HEAD_DIM), f32), jax.ShapeDtypeStruct((S, HEADS * HEAD_DIM), f32),
                   jax.ShapeDtypeStruct((S, HEADS * HEAD_DIM), f32)],
        compiler_params=_params(("arbitrary", "arbitrary")))(q[0], k[0], v[0], tot, cnt, do, m_le, m_lt)


CONV_HALO = 32
CONV_A_BLK, CONV_G_BLK = 8, 9


def _glu(a, g):
    return a.astype(f32) * _sigmoid(g.astype(f32))


def conv_fwd(name, u, dw):
    S = u.shape[0]
    T = min(512, S)
    nT = S // T

    def body(a_ref, g_ref, ap_ref, gp_ref, dw_ref, y_ref, ext_ref):
        i = pl.program_id(0)
        prev = _glu(ap_ref[T - CONV_HALO:, :], gp_ref[T - CONV_HALO:, :])
        ext_ref[:CONV_HALO, :] = jnp.where(i > 0, prev, 0.0)
        ext_ref[CONV_HALO:, :] = _glu(a_ref[...], g_ref[...])
        acc = jnp.zeros((T, CONV_CH), f32)
        for w in range(CONV_WIDTH):
            acc = acc + dw_ref[w:w + 1, :] * ext_ref[pl.ds(w + CONV_HALO - (CONV_WIDTH - 1), T), :]
        y_ref[...] = acc

    cur = lambda blk: pl.BlockSpec((T, CONV_CH), lambda i: (i, blk))
    prv = lambda blk: pl.BlockSpec((T, CONV_CH), lambda i: (jnp.maximum(i - 1, 0), blk))
    return pl.pallas_call(
        body, name=name, grid=(nT,),
        in_specs=[cur(CONV_A_BLK), cur(CONV_G_BLK), prv(CONV_A_BLK), prv(CONV_G_BLK),
                  pl.BlockSpec(dw.shape, lambda i: (0, 0))],
        out_specs=pl.BlockSpec((T, CONV_CH), lambda i: (i, 0)),
        out_shape=jax.ShapeDtypeStruct((S, CONV_CH), f32),
        scratch_shapes=[pltpu.VMEM((T + CONV_HALO, CONV_CH), f32)],
        compiler_params=_params(("arbitrary",)))(u, u, u, u, dw)


def conv_bwd(name, u, dy, dw):
    S = u.shape[0]
    T = min(512, S)
    nT = S // T
    lead = CONV_HALO - (CONV_WIDTH - 1)

    def body(a_ref, g_ref, ap_ref, gp_ref, dy_ref, dyn_ref, dw_ref, du_ref, ddw_ref, ext_ref, dext_ref):
        i = pl.program_id(0)
        prev = _glu(ap_ref[T - CONV_HALO:, :], gp_ref[T - CONV_HALO:, :])
        ext_ref[:CONV_HALO, :] = jnp.where(i > 0, prev, 0.0)
        a, sg = a_ref[...].astype(f32), _sigmoid(g_ref[...].astype(f32))
        ext_ref[CONV_HALO:, :] = a * sg
        dyv = dy_ref[...]
        dext_ref[:T, :] = dyv
        dext_ref[T:, :] = jnp.where(i < nT - 1, dyn_ref[:CONV_HALO, :], 0.0)
        @pl.when(i == 0)
        def _():
            ddw_ref[...] = jnp.zeros_like(ddw_ref)

        dglu = jnp.zeros((T, CONV_CH), f32)
        for w in range(CONV_WIDTH):
            dglu = dglu + dw_ref[w:w + 1, :] * dext_ref[pl.ds(CONV_WIDTH - 1 - w, T), :]
            ddw_ref[w:w + 1, :] += jnp.sum(dyv * ext_ref[pl.ds(w + lead, T), :], axis=0, keepdims=True)

        du_ref[:, :CONV_CH] = (dglu * sg).astype(du_ref.dtype)
        du_ref[:, CONV_CH:] = (dglu * a * sg * (1.0 - sg)).astype(du_ref.dtype)

    cur = lambda blk: pl.BlockSpec((T, CONV_CH), lambda i: (i, blk))
    prv = lambda blk: pl.BlockSpec((T, CONV_CH), lambda i: (jnp.maximum(i - 1, 0), blk))
    return pl.pallas_call(
        body, name=name, grid=(nT,),
        in_specs=[cur(CONV_A_BLK), cur(CONV_G_BLK), prv(CONV_A_BLK), prv(CONV_G_BLK),
                  pl.BlockSpec((T, CONV_CH), lambda i: (i, 0)),
                  pl.BlockSpec((T, CONV_CH), lambda i: (jnp.minimum(i + 1, nT - 1), 0)),
                  pl.BlockSpec(dw.shape, lambda i: (0, 0))],
        out_specs=[pl.BlockSpec((T, 2 * CONV_CH), lambda i: (i, 0)), pl.BlockSpec(dw.shape, lambda i: (0, 0))],
        out_shape=[jax.ShapeDtypeStruct((S, 2 * CONV_CH), bf16), jax.ShapeDtypeStruct(dw.shape, f32)],
        scratch_shapes=[pltpu.VMEM((T + CONV_HALO, CONV_CH), f32), pltpu.VMEM((T + CONV_HALO, CONV_CH), f32)],
        compiler_params=_params(("arbitrary",)))(u, u, u, u, dy, dy, dw)


def rope_tables(pos_col):
    S = pos_col.shape[0]
    T = min(512, S)
    inv_freq = ROPE_BASE ** (-jnp.arange(0, MLA_ROPE, 2, dtype=f32) / MLA_ROPE)
    zeros = jnp.zeros((MLA_NOPE,), f32)
    inv_row = jnp.concatenate([zeros, inv_freq, inv_freq, zeros[:MLA_PAD - MLA_QK]]).reshape(1, MLA_PAD)

    def body(p_ref, f_ref, c_ref, s_ref):
        lane = lax.broadcasted_iota(jnp.int32, (T, MLA_PAD), 1)
        ang = p_ref[...].astype(f32) * f_ref[...]
        rot = jnp.logical_and(lane >= MLA_NOPE, lane < MLA_QK)
        c_ref[...] = jnp.where(rot, jnp.cos(ang), jnp.where(lane < MLA_NOPE, 1.0, 0.0))
        s_ref[...] = jnp.where(rot, jnp.sin(ang), 0.0)

    spec = pl.BlockSpec((T, MLA_PAD), lambda i: (i, 0))
    return pl.pallas_call(
        body, name="rope_tables", grid=(S // T,),
        in_specs=[pl.BlockSpec((T, 1), lambda i: (i, 0)), pl.BlockSpec((1, MLA_PAD), lambda i: (0, 0))],
        out_specs=[spec, spec], out_shape=[jax.ShapeDtypeStruct((S, MLA_PAD), f32)] * 2,
        compiler_params=_params(("parallel",)))(pos_col, inv_row)


def loss_head(y, target):
    S, D = y.shape
    T = min(512, S)

    def body(y_ref, t_ref, dy_ref, l_ref):
        i = pl.program_id(0)
        err = y_ref[...] - t_ref[...]
        dy_ref[...] = err * (1.0 / D)
        part = 0.5 * jnp.sum(jnp.sum(err * err, axis=-1, keepdims=True) * (1.0 / D), axis=0, keepdims=True)
        part = jnp.broadcast_to(part, l_ref.shape)

        @pl.when(i == 0)
        def _():
            l_ref[...] = part

        @pl.when(i > 0)
        def _():
            l_ref[...] += part

    spec = pl.BlockSpec((T, D), lambda i: (i, 0))
    return pl.pallas_call(
        body, name="loss_head", grid=(S // T,), in_specs=[spec, spec],
        out_specs=[spec, pl.BlockSpec((8, 128), lambda i: (0, 0))],
        out_shape=[jax.ShapeDtypeStruct((S, D), f32), jax.ShapeDtypeStruct((8, 128), f32)],
        compiler_params=_params(("arbitrary",)))(y, target)


def _row(a, w=None, off=0, st=0):
    return (a, w or a.shape[1], off, st)


def _out(tw, dt, w=None, off=0, st=0):
    return (tw, dt, w or tw, off, st)


ACT = f32


def ffn_fwd(tag, x, g, w_in, w_out):
    (h,) = rowwise(f"{tag}_rms", fn_rms, [_row(x)], [g], [_out(D_MODEL, bf16)])
    u = mm(f"{tag}_in", h, w_in, "nn", out_dtype=ACT)
    (a,) = rowwise(f"{tag}_swiglu", fn_swiglu, [_row(u, FFN_HIDDEN, 0), _row(u, FFN_HIDDEN, 1)], [],
                   [_out(FFN_HIDDEN, bf16)])
    y = mm(f"{tag}_out", a, w_out, "nn", alpha=0.5, res=x)
    return y, (x, h, u, a)


def ffn_bwd(tag, saved, g, w_in, w_out, dy):
    x, h, u, a = saved
    d_w_out = mm(f"{tag}_dwout", a, dy, "tn", alpha=0.5)
    da = mm(f"{tag}_da", dy, w_out, "nt", alpha=0.5, out_dtype=ACT)
    (dgate, dup), _ = rowwise_bwd(f"{tag}_dswiglu", fn_swiglu, [_row(u, FFN_HIDDEN, 0), _row(u, FFN_HIDDEN, 1)], [],
                                  [_row(da)], [_out(FFN_HIDDEN, bf16), _out(FFN_HIDDEN, bf16)])
    du = jnp.concatenate([dgate, dup], axis=1)
    d_w_in = mm(f"{tag}_dwin", h, du, "tn")
    dh = mm(f"{tag}_dh", du, w_in, "nt", out_dtype=ACT)
    (dx,), (dg,) = rowwise_bwd(f"{tag}_drms", fn_rms_res, [_row(x)], [g], [_row(dy), _row(dh)], [_out(D_MODEL, f32)])
    return dx, dg, d_w_in, d_w_out


def _seg(name):
    for n, _, w, start in _U_SEGS:
        if n == name:
            return start, w
    raise KeyError(name)


def mix_fwd(tag, x, mem_n_in, tabs, p):
    cos_t, sin_t = tabs
    (h,) = rowwise(f"{tag}_rms", fn_rms, [_row(x)], [p["mix_norm"]], [_out(D_MODEL, bf16)])
    u = mm(f"{tag}_in", h, p["w_in"], "nn", out_dtype=ACT)
    yc = conv_fwd(f"{tag}_conv", u, p["conv_dw"])
    (br_a,) = rowwise(f"{tag}_lnsilu", fn_ln_silu, [_row(yc)], [p["conv_b"], p["conv_ln_g"], p["conv_ln_b"]],
                      [_out(BRANCH_WIDTH, bf16)])
    sb0 = _seg("sb")[0] // HEAD_DIM
    qs, ks = rowwise(f"{tag}_sbprep", fn_sb_prep, [_row(u, HEAD_DIM, sb0, 1), _row(u, HEAD_DIM, sb0 + HEADS, 1)],
                     [p["sb_q_hnorm"], p["sb_k_hnorm"]],
                     [_out(BRANCH_WIDTH, bf16, HEAD_DIM, 0, 1), _out(BRANCH_WIDTH, bf16, HEAD_DIM, 0, 1)], H=HEADS)
    sb_v = _row(u, HEAD_DIM, sb0 + 2 * HEADS, 1)
    br_b, tot_b, cnt_b = sb_fwd(f"{tag}_sb", _row(qs, HEAD_DIM, 0, 1), _row(ks, HEAD_DIM, 0, 1), sb_v)
    ql_n, kvl_n = rowwise(f"{tag}_latrms", lambda a, b, ga, gb: (_rms(a, ga), _rms(b, gb)),
                          [_row(u, MLA_Q_LORA, _seg("qlat")[0] // MLA_Q_LORA), _row(u, MLA_KV_LORA, _seg("kvlat")[0] // MLA_KV_LORA)],
                          [p["mla_q_norm"], p["mla_kv_norm"]], [_out(MLA_Q_LORA, bf16), _out(MLA_KV_LORA, bf16)])
    qfull = mm(f"{tag}_uq", ql_n, p["mla_w_uq"], "nn", out_dtype=ACT)
    kvfull = mm(f"{tag}_ukv", kvl_n, p["mla_w_ukv"], "nn", out_dtype=ACT)
    kr_row = _row(u, HEAD_DIM, _seg("krope")[0] // HEAD_DIM, 0)
    (qr,) = rowwise(f"{tag}_mlaq", fn_mla_q, [_row(qfull, MLA_PAD, 0, 1), _row(cos_t), _row(sin_t)], [p["mla_q_hnorm"]],
                    [_out(HEADS * MLA_PAD, bf16, MLA_PAD, 0, 1)], H=HEADS)
    (kr,) = rowwise(f"{tag}_mlak", fn_mla_k, [_row(kvfull, HEAD_DIM, 0, 2), kr_row, _row(cos_t), _row(sin_t)],
                    [p["mla_k_hnorm"]], [_out(HEADS * MLA_PAD, bf16, MLA_PAD, 0, 1)], H=HEADS)
    mla_v = _row(kvfull, HEAD_DIM, 1, 2)
    br_c, lse_c = attn_fwd(f"{tag}_mla", _row(qr, MLA_PAD, 0, 1), _row(kr, MLA_PAD, 0, 1), mla_v,
                           scale=MLA_QK ** -0.5, mask="chunk")
    (mem_n,) = rowwise(f"{tag}_memrms", fn_rms, [_row(mem_n_in)], [p["mem_norm"]], [_out(D_MODEL, bf16)])
    kvm = mm(f"{tag}_memkv", mem_n, p["mem_w_kv"], "nn", out_dtype=ACT)
    (km,) = rowwise(f"{tag}_memk", fn_rms, [_row(kvm, HEAD_DIM, 0, 1)], [p["mem_k_hnorm"]],
                    [_out(BRANCH_WIDTH, bf16, HEAD_DIM, 0, 1)], H=HEADS)
    mq0 = _seg("memq")[0] // HEAD_DIM
    (qm,) = rowwise(f"{tag}_memq", fn_rms, [_row(u, HEAD_DIM, mq0, 1)], [p["mem_q_hnorm"]],
                    [_out(BRANCH_WIDTH, bf16, HEAD_DIM, 0, 1)], H=HEADS)
    mem_v = _row(kvm, HEAD_DIM, HEADS, 1)
    br_d, lse_d = attn_fwd(f"{tag}_memattn", _row(qm, HEAD_DIM, 0, 1), _row(km, HEAD_DIM, 0, 1), mem_v,
                           scale=HEAD_DIM ** -0.5, mask=None)
    branches = (br_a, br_b, br_c, br_d)
    proj = [mm(f"{tag}_branch{b}", branches[b], p["w_branch"][b], "nn", out_dtype=ACT) for b in range(N_BRANCH)]
    gate_rows = [_row(u, D_MODEL, b) for b in range(N_BRANCH)]
    (merged,) = rowwise(f"{tag}_merge", fn_merge, gate_rows + [_row(t) for t in proj], [], [_out(D_MODEL, bf16)], T=256)
    y = mm(f"{tag}_out", merged, p["w_out"], "nn", res=x)
    saved = dict(x=x, h=h, u=u, yc=yc, qs=qs, ks=ks, ql_n=ql_n, kvl_n=kvl_n, qfull=qfull, kvfull=kvfull, qr=qr, kr=kr,
                 lse_c=lse_c, mem_n=mem_n, kvm=kvm, km=km, qm=qm, lse_d=lse_d, branches=branches, proj=proj,
                 merged=merged, tot_b=tot_b, cnt_b=cnt_b)
    return y, saved


def mix_bwd(tag, sv, mem_n_in, tabs, p, dy):
    cos_t, sin_t = tabs
    u, S = sv["u"], sv["u"].shape[0]
    g = {}
    g["w_out"] = mm(f"{tag}_dwout", sv["merged"], dy, "tn")
    dmerged = mm(f"{tag}_dmerged", dy, p["w_out"], "nt", out_dtype=ACT)
    gate_rows = [_row(u, D_MODEL, b) for b in range(N_BRANCH)]
    d_merge, _ = rowwise_bwd(f"{tag}_dmerge", fn_merge, gate_rows + [_row(t) for t in sv["proj"]], [], [_row(dmerged)],
                             [_out(D_MODEL, bf16)] * (2 * N_BRANCH), T=256)
    d_gates, d_proj = d_merge[:N_BRANCH], d_merge[N_BRANCH:]
    g["w_branch"] = [mm(f"{tag}_dwbranch{b}", sv["branches"][b], d_proj[b], "tn") for b in range(N_BRANCH)]
    d_br = [mm(f"{tag}_dbranch{b}", d_proj[b], p["w_branch"][b], "nt", out_dtype=ACT) for b in range(N_BRANCH)]
    (dyc,), (g["conv_b"], g["conv_ln_g"], g["conv_ln_b"]) = rowwise_bwd(
        f"{tag}_dlnsilu", fn_ln_silu, [_row(sv["yc"])], [p["conv_b"], p["conv_ln_g"], p["conv_ln_b"]], [_row(d_br[0])],
        [_out(BRANCH_WIDTH, f32)])
    du_conv, g["conv_dw"] = conv_bwd(f"{tag}_dconv", u, dyc, p["conv_dw"])
    sb0 = _seg("sb")[0] // HEAD_DIM
    sb_v = _row(u, HEAD_DIM, sb0 + 2 * HEADS, 1)
    dqs, dks, dv_sb = sb_bwd(f"{tag}_dsb", _row(sv["qs"], HEAD_DIM, 0, 1), _row(sv["ks"], HEAD_DIM, 0, 1), sb_v,
                             sv["tot_b"], sv["cnt_b"], d_br[1])
    (du_sbq, du_sbk), (g["sb_q_hnorm"], g["sb_k_hnorm"]) = rowwise_bwd(
        f"{tag}_dsbprep", fn_sb_prep, [_row(u, HEAD_DIM, sb0, 1), _row(u, HEAD_DIM, sb0 + HEADS, 1)],
        [p["sb_q_hnorm"], p["sb_k_hnorm"]], [_row(dqs, HEAD_DIM, 0, 1), _row(dks, HEAD_DIM, 0, 1)],
        [_out(BRANCH_WIDTH, bf16, HEAD_DIM, 0, 1), _out(BRANCH_WIDTH, bf16, HEAD_DIM, 0, 1)], H=HEADS)
    mla_v = _row(sv["kvfull"], HEAD_DIM, 1, 2)
    dqr, dkr, dv_mla = attn_bwd(f"{tag}_dmla", _row(sv["qr"], MLA_PAD, 0, 1), _row(sv["kr"], MLA_PAD, 0, 1), mla_v,
                                sv["branches"][2], d_br[2], sv["lse_c"], scale=MLA_QK ** -0.5, mask="chunk")
    (dqfull,), (g["mla_q_hnorm"],) = rowwise_bwd(
        f"{tag}_dmlaq", fn_mla_q, [_row(sv["qfull"], MLA_PAD, 0, 1), _row(cos_t), _row(sin_t)], [p["mla_q_hnorm"]],
        [_row(dqr, MLA_PAD, 0, 1)], [_out(HEADS * MLA_PAD, bf16, MLA_PAD, 0, 1)], H=HEADS, nondiff=(1, 2))
    kr_row = _row(u, HEAD_DIM, _seg("krope")[0] // HEAD_DIM, 0)
    (dkn, du_krope, dvp), (g["mla_k_hnorm"],) = rowwise_bwd(
        f"{tag}_dmlak", fn_mla_k_v, [_row(sv["kvfull"], HEAD_DIM, 0, 2), kr_row, mla_v, _row(cos_t), _row(sin_t)],
        [p["mla_k_hnorm"]], [_row(dkr, MLA_PAD, 0, 1), _row(dv_mla, HEAD_DIM, 0, 1)],
        [_out(BRANCH_WIDTH, bf16, HEAD_DIM, 0, 1), _out(HEAD_DIM, f32), _out(BRANCH_WIDTH, bf16, HEAD_DIM, 0, 1)],
        H=HEADS, nondiff=(3, 4))
    dkvfull = _interleave(f"{tag}_dkvfull", dkn, dvp)
    g["mla_w_uq"] = mm(f"{tag}_dwuq", sv["ql_n"], dqfull, "tn")
    g["mla_w_ukv"] = mm(f"{tag}_dwukv", sv["kvl_n"], dkvfull, "tn")
    dql_n = mm(f"{tag}_dqln", dqfull, p["mla_w_uq"], "nt", out_dtype=ACT)
    dkvl_n = mm(f"{tag}_dkvln", dkvfull, p["mla_w_ukv"], "nt", out_dtype=ACT)
    (du_qlat, du_kvlat), (g["mla_q_norm"], g["mla_kv_norm"]) = rowwise_bwd(
        f"{tag}_dlatrms", lambda a, b, ga, gb: (_rms(a, ga), _rms(b, gb)),
        [_row(u, MLA_Q_LORA, _seg("qlat")[0] // MLA_Q_LORA), _row(u, MLA_KV_LORA, _seg("kvlat")[0] // MLA_KV_LORA)],
        [p["mla_q_norm"], p["mla_kv_norm"]], [_row(dql_n), _row(dkvl_n)], [_out(MLA_Q_LORA, bf16), _out(MLA_KV_LORA, bf16)])
    mem_v = _row(sv["kvm"], HEAD_DIM, HEADS, 1)
    dqm, dkm, dvm = attn_bwd(f"{tag}_dmemattn", _row(sv["qm"], HEAD_DIM, 0, 1), _row(sv["km"], HEAD_DIM, 0, 1), mem_v,
                             sv["branches"][3], d_br[3], sv["lse_d"], scale=HEAD_DIM ** -0.5, mask=None)
    mq0 = _seg("memq")[0] // HEAD_DIM
    (du_memq,), (g["mem_q_hnorm"],) = rowwise_bwd(
        f"{tag}_dmemq", fn_rms, [_row(u, HEAD_DIM, mq0, 1)], [p["mem_q_hnorm"]], [_row(dqm, HEAD_DIM, 0, 1)],
        [_out(BRANCH_WIDTH, bf16, HEAD_DIM, 0, 1)], H=HEADS)
    (dkvm_k, dkvm_v), (g["mem_k_hnorm"],) = rowwise_bwd(
        f"{tag}_dmemk", fn_mem_k_v, [_row(sv["kvm"], HEAD_DIM, 0, 1), mem_v], [p["mem_k_hnorm"]],
        [_row(dkm, HEAD_DIM, 0, 1), _row(dvm, HEAD_DIM, 0, 1)],
        [_out(BRANCH_WIDTH, bf16, HEAD_DIM, 0, 1), _out(BRANCH_WIDTH, bf16, HEAD_DIM, 0, 1)], H=HEADS)
    dkvm = jnp.concatenate([dkvm_k, dkvm_v], axis=1)
    g["mem_w_kv"] = mm(f"{tag}_dwmemkv", sv["mem_n"], dkvm, "tn")
    dmem_n = mm(f"{tag}_dmemn", dkvm, p["mem_w_kv"], "nt", out_dtype=ACT)
    _, (g["mem_norm"],) = rowwise_bwd(f"{tag}_dmemrms", fn_rms, [_row(mem_n_in)], [p["mem_norm"]], [_row(dmem_n)],
                                      [_out(D_MODEL, bf16)])
    du_krope_b = du_krope.astype(bf16)
    du = jnp.concatenate(list(d_gates) + [du_conv, du_sbq, du_sbk, dv_sb.astype(bf16), du_qlat, du_kvlat, du_memq,
                                          du_krope_b, jnp.zeros((S, U_WIDTH - _seg("krope")[0] - HEAD_DIM), bf16)], axis=1)
    g["w_in"] = mm(f"{tag}_dwin", sv["h"], du, "tn")
    dh = mm(f"{tag}_dh", du, p["w_in"], "nt", out_dtype=ACT)
    (dx,), (g["mix_norm"],) = rowwise_bwd(f"{tag}_drms", fn_rms_res, [_row(sv["x"])], [p["mix_norm"]],
                                          [_row(dy), _row(dh)], [_out(D_MODEL, f32)])
    return dx, g


def _interleave(name, a, b):
    S, W = a.shape
    T = min(512, S)

    def body(a_ref, b_ref, o_ref):
        o_ref[:, :HEAD_DIM] = a_ref[...]
        o_ref[:, HEAD_DIM:] = b_ref[...]

    blk = pl.BlockSpec((T, HEAD_DIM), lambda i, h: (i, h))
    return pl.pallas_call(
        body, name=name, grid=(S // T, W // HEAD_DIM), in_specs=[blk, blk],
        out_specs=pl.BlockSpec((T, 2 * HEAD_DIM), lambda i, h: (i, h)),
        out_shape=jax.ShapeDtypeStruct((S, 2 * W), a.dtype), compiler_params=_params(("parallel", "parallel")))(a, b)


def _u_layout(w):
    parts, at = [], 0
    for _, src, width, start in _U_SEGS:
        assert start == at
        parts.append(w[..., src:src + width])
        at += width
    parts.append(jnp.zeros(w.shape[:-1] + (U_WIDTH - at,), w.dtype))
    return jnp.concatenate(parts, axis=-1)


def _u_layout_inv(g):
    order = sorted(_U_SEGS, key=lambda s: s[1])
    return jnp.concatenate([g[..., start:start + width] for _, _, width, start in order], axis=-1)


def _pad_heads(w, n=MLA_QK, to=MLA_PAD):
    w = w.reshape(w.shape[:-1] + (HEADS, n))
    w = jnp.pad(w, [(0, 0)] * (w.ndim - 1) + [(0, to - n)])
    return w.reshape(w.shape[:-2] + (HEADS * to,))


def _unpad_heads(g, n=MLA_QK, to=MLA_PAD):
    g = g.reshape(g.shape[:-1] + (HEADS, to))[..., :n]
    return g.reshape(g.shape[:-2] + (HEADS * n,))


def layer_params(W, l):
    row = lambda name: W[name][l].reshape(1, -1).astype(f32)
    p = {n: row(n) for n in SMALL if n != "mla_q_hnorm" and n != "mla_k_hnorm"}
    for n in ("mla_q_hnorm", "mla_k_hnorm"):
        p[n] = jnp.pad(row(n), ((0, 0), (0, MLA_PAD - MLA_QK)))
    for n in ("ffn1_w_in", "ffn1_w_out", "ffn2_w_in", "ffn2_w_out", "mla_w_ukv", "mem_w_kv", "w_out"):
        p[n] = W[n][l]
    p["w_branch"] = [W["w_branch"][l, b] for b in range(N_BRANCH)]
    p["w_in"] = _u_layout(W["w_in"][l])
    p["mla_w_uq"] = _pad_heads(W["mla_w_uq"][l])
    p["conv_dw"] = jnp.pad(W["conv_dw"][l].astype(f32), ((0, 1), (0, 0)))
    return p


def layer_grads_to_original(g):
    out = dict(g)
    out["w_in"] = _u_layout_inv(g["w_in"])
    out["mla_w_uq"] = _unpad_heads(g["mla_w_uq"])
    out["conv_dw"] = g["conv_dw"][:CONV_WIDTH]
    out["w_branch"] = jnp.stack(g["w_branch"])
    for n in ("mla_q_hnorm", "mla_k_hnorm"):
        out[n] = g[n][:, :MLA_QK]
    return {n: (out[n].reshape(-1) if n in SMALL else out[n]) for n in out}


def local_step(x, mem, pos_col, target, W):
    tabs = rope_tables(pos_col)
    params = [layer_params(W, l) for l in range(DEPTH)]
    saved = []
    for l, p in enumerate(params):
        x, s1 = ffn_fwd(f"l{l}_ffn1", x, p["ffn1_norm"], p["ffn1_w_in"], p["ffn1_w_out"])
        x, s2 = mix_fwd(f"l{l}_mix", x, mem, tabs, p)
        x, s3 = ffn_fwd(f"l{l}_ffn2", x, p["ffn2_norm"], p["ffn2_w_in"], p["ffn2_w_out"])
        saved.append((s1, s2, s3))
    dx, loss_blk = loss_head(x, target)
    grads = [None] * DEPTH
    for l in reversed(range(DEPTH)):
        p, (s1, s2, s3) = params[l], saved[l]
        dx, g_n2, g_in2, g_out2 = ffn_bwd(f"l{l}_ffn2", s3, p["ffn2_norm"], p["ffn2_w_in"], p["ffn2_w_out"], dx)
        dx, g = mix_bwd(f"l{l}_mix", s2, mem, tabs, p, dx)
        dx, g_n1, g_in1, g_out1 = ffn_bwd(f"l{l}_ffn1", s1, p["ffn1_norm"], p["ffn1_w_in"], p["ffn1_w_out"], dx)
        g.update(ffn1_norm=g_n1, ffn1_w_in=g_in1, ffn1_w_out=g_out1, ffn2_norm=g_n2, ffn2_w_in=g_in2, ffn2_w_out=g_out2)
        grads[l] = layer_grads_to_original(g)
    return loss_blk, dx, grads


_ANY = pl.BlockSpec(memory_space=pl.ANY)
_COMM = pltpu.CompilerParams(has_side_effects=True)


def _coords():
    return lax.axis_index("x"), lax.axis_index("y"), lax.axis_index("c")


def chip_exchange(name, src, scatter):
    shape = src.shape[1:] if scatter else src.shape

    def body(src_ref, out_ref, send_sems, recv_sems):
        x, y, c = _coords()
        copies = []
        for k, (px, py) in enumerate([(1 - x, y), (x, 1 - y), (1 - x, 1 - y)]):
            piece = src_ref.at[2 * px + py] if scatter else src_ref
            cp = pltpu.make_async_remote_copy(piece, out_ref.at[k], send_sems.at[k], recv_sems.at[k],
                                              device_id=(px, py, c), device_id_type=MESH)
            cp.start()
            copies.append(cp)
        for cp in copies:
            cp.wait()

    return pl.pallas_call(
        body, name=name, in_specs=[_ANY], out_specs=_ANY, out_shape=jax.ShapeDtypeStruct((3,) + shape, src.dtype),
        scratch_shapes=[pltpu.SemaphoreType.DMA((3,)), pltpu.SemaphoreType.DMA((3,))], compiler_params=_COMM)(src)


def sibling_exchange(name, src):
    def body(src_ref, out_ref, send_sem, recv_sem):
        x, y, c = _coords()
        cp = pltpu.make_async_remote_copy(src_ref, out_ref, send_sem, recv_sem, device_id=(x, y, 1 - c),
                                          device_id_type=MESH)
        cp.start()
        cp.wait()

    return pl.pallas_call(
        body, name=name, in_specs=[_ANY], out_specs=_ANY, out_shape=jax.ShapeDtypeStruct(src.shape, src.dtype),
        scratch_shapes=[pltpu.SemaphoreType.DMA, pltpu.SemaphoreType.DMA], compiler_params=_COMM)(src)


def all8_gather(name, src):
    def body(src_ref, out_ref, send_sems, recv_sems, local_sem):
        x, y, c = _coords()
        me = 4 * x + 2 * y + c
        mine = pltpu.make_async_copy(src_ref, out_ref.at[me], local_sem)
        mine.start()
        copies = []
        for k in range(1, 8):
            peer = (1 - x if k & 4 else x, 1 - y if k & 2 else y, 1 - c if k & 1 else c)
            cp = pltpu.make_async_remote_copy(src_ref, out_ref.at[me], send_sems.at[k - 1], recv_sems.at[k - 1],
                                              device_id=peer, device_id_type=MESH)
            cp.start()
            copies.append(cp)
        for cp in copies:
            cp.wait()
        mine.wait()

    return pl.pallas_call(
        body, name=name, in_specs=[_ANY], out_specs=_ANY, out_shape=jax.ShapeDtypeStruct((8,) + src.shape, src.dtype),
        scratch_shapes=[pltpu.SemaphoreType.DMA((7,)), pltpu.SemaphoreType.DMA((7,)), pltpu.SemaphoreType.DMA],
        compiler_params=_COMM)(src)


def sum8(name, g):
    def body(g_ref, o_ref):
        acc = g_ref[0]
        for k in range(1, 8):
            acc = acc + g_ref[k]
        o_ref[...] = acc

    return pl.pallas_call(body, name=name, out_shape=jax.ShapeDtypeStruct(g.shape[1:], g.dtype))(g)


PACK_COLS = 1024


def _slots(name, n):
    return 2 * n if name == "conv_dw" else n


def _pack(parts, dtype):
    flat = jnp.concatenate([parts[n].astype(dtype) for n in SHARDED])
    total = -(-flat.shape[0] // PACK_GRAIN) * PACK_GRAIN
    flat = jnp.pad(flat, (0, total - flat.shape[0]))
    return flat.reshape(2, total // (2 * PACK_COLS), PACK_COLS)


def _unpack(packed, shard_shapes):
    flat = packed.reshape(-1)
    out, at = {}, 0
    for n in SHARDED:
        size = _slots(n, math.prod(shard_shapes[n]))
        out[n] = flat[at:at + size]
        at += size
    return out


def gather_shards(name, own):
    def body(own_ref, out_ref, send_sems, recv_sems, local_sem):
        x, y, c = _coords()
        sib = (x, y, 1 - c)
        mine = pltpu.make_async_copy(own_ref, out_ref.at[0], local_sem)
        mine.start()
        chips = [(2, (1 - x, y)), (1, (x, 1 - y)), (3, (1 - x, 1 - y))]

        def copy(k, src, m, half, to):
            return pltpu.make_async_remote_copy(src, out_ref.at[m, half], send_sems.at[k], recv_sems.at[k],
                                                device_id=to, device_id_type=MESH)

        first = [copy(k, own_ref.at[c], m, c, (px, py, c)) for k, (m, (px, py)) in enumerate(chips)]
        for cp in first:
            cp.start()
        passed = []
        for k, (m, _) in enumerate(chips):
            first[k].wait_recv()
            cp = copy(3 + k, out_ref.at[m, c], m, c, sib)
            cp.start()
            passed.append(cp)
        for k, (m, _) in enumerate(chips):
            copy(3 + k, out_ref.at[m, 1 - c], m, 1 - c, sib).wait_recv()
        for cp in first + passed:
            cp.wait_send()
        mine.wait()

    return pl.pallas_call(
        body, name=name, in_specs=[_ANY], out_specs=_ANY, out_shape=jax.ShapeDtypeStruct((4,) + own.shape, own.dtype),
        scratch_shapes=[pltpu.SemaphoreType.DMA((6,)), pltpu.SemaphoreType.DMA((6,)), pltpu.SemaphoreType.DMA],
        compiler_params=_COMM)(own)


def gather_weights(w):
    x, y, _ = _coords()
    me = 2 * x + y
    parts = {}
    for n in SHARDED:
        if n == "conv_dw":
            parts[n] = lax.bitcast_convert_type(w[n], bf16).reshape(-1)
        else:
            parts[n] = w[n].astype(bf16).reshape(-1)
    by_mask = gather_shards("ag_shards", _pack(parts, bf16))
    shapes = {n: w[n].shape for n in SHARDED}
    pieces = [_unpack(lax.dynamic_index_in_dim(by_mask, jnp.bitwise_xor(s, me), axis=0, keepdims=False), shapes)
              for s in range(4)]
    full = {}
    for n in SHARDED:
        shards = []
        for s in range(4):
            p = pieces[s][n]
            if n == "conv_dw":
                p = lax.bitcast_convert_type(p.reshape(w[n].shape + (2,)), f32)
            shards.append(p.reshape(w[n].shape))
        full[n] = jnp.concatenate(shards, axis=SHARD_AXIS[n])
    return full


def _add_streams(name, ins, selectors, out_dtypes, rows, T=256):
    n_streams = max([a.shape[sel.index("s")] for a, sel in zip(ins, selectors) if "s" in sel] + [1])

    def spec(sel):
        def index(s, i, pf):
            lead = tuple(s if e == "s" else (pf[e[1]] if isinstance(e, tuple) else e) for e in sel)
            return lead + (i, 0)
        return pl.BlockSpec((None,) * len(sel) + (T, PACK_COLS), index)

    def body(pf_ref, *refs):
        acc = refs[0][...].astype(f32)
        for r in refs[1:len(ins)]:
            acc = acc + r[...].astype(f32)
        for o in refs[len(ins):]:
            o[...] = acc.astype(o.dtype)

    def run(pf):
        grid_spec = pltpu.PrefetchScalarGridSpec(
            num_scalar_prefetch=1, grid=(n_streams, rows // T), in_specs=[spec(sel) for sel in selectors],
            out_specs=[spec(("s",)) for _ in out_dtypes])
        return pl.pallas_call(
            body, name=name, grid_spec=grid_spec,
            out_shape=[jax.ShapeDtypeStruct((n_streams, rows, PACK_COLS), dt) for dt in out_dtypes],
            compiler_params=_params(("parallel", "parallel")))(pf, *ins)
    return run


def reduce_grads(grads, shard_shapes):
    x, y, c = _coords()
    me = 2 * x + y
    pieces = []
    for s in range(4):
        at = 0
        for n in SHARDED:
            ax = SHARD_AXIS[n] - 1
            for g in grads:
                width = g[n].shape[ax] // 4
                p = lax.slice_in_dim(g[n], s * width, (s + 1) * width, axis=ax).reshape(-1)
                pieces.append(p)
                at += p.shape[0]
            if n == "conv_dw":
                pieces.append(jnp.zeros((_slots(n, 1) - 1) * math.prod(shard_shapes[n]), f32))
                at += pieces[-1].shape[0]
        total = -(-at // PACK_GRAIN) * PACK_GRAIN
        pieces.append(jnp.zeros(total - at, f32))
    R = total // (2 * PACK_COLS)
    G = jnp.concatenate(pieces).reshape(4, 2, R, PACK_COLS)
    pf = jnp.stack([c, me]).astype(jnp.int32)
    theirs = lax.dynamic_index_in_dim(G, 1 - c, axis=1, keepdims=False).astype(bf16)
    from_sib = sibling_exchange("rs_sibling", theirs)
    chip_sum, chip_sum_b = _add_streams("rs_add_sibling", [G, from_sib], [("s", ("pf", 0)), ("s",)], [f32, bf16], R)(pf)
    got = chip_exchange("rs_chips", chip_sum_b, scatter=True)
    (half,) = _add_streams("rs_add_chips", [chip_sum, got, got, got], [(("pf", 1),), (0,), (1,), (2,)], [f32], R)(pf)
    half = half[0]
    other = sibling_exchange("rs_final", half)
    shard = jnp.stack([jnp.where(c == 0, half, other), jnp.where(c == 0, other, half)])
    flat = _unpack(shard, shard_shapes)
    out = {}
    for n in SHARDED:
        k = math.prod(shard_shapes[n])
        out[n] = flat[n][:k].reshape(shard_shapes[n])
    return out


def _small_pack(t):
    flat = jnp.concatenate([t[n].reshape(-1) for n in SMALL])
    total = -(-flat.shape[0] // SMALL_PAD) * SMALL_PAD
    return jnp.pad(flat, (0, total - flat.shape[0])).reshape(-1, 128)


def _small_unpack(a, shapes):
    flat, out, at = a.reshape(-1), {}, 0
    for n in SMALL:
        k = math.prod(shapes[n])
        out[n] = flat[at:at + k].reshape(shapes[n])
        at += k
    return out


def adamw(name, w, g, m, v):
    shape = w.shape
    two = lambda a: a.reshape(-1, shape[-1])
    rows = two(w).shape[0]
    T = 256 if rows % 256 == 0 else rows
    outs = rowwise(name, fn_adamw, [_row(two(w)), _row(two(g)), _row(two(m)), _row(two(v))], [],
                   [_out(shape[-1], f32)] * 3, T=T)
    return [o.reshape(shape) for o in outs]


def kernel(x, mem, positions, ffn1_norm, ffn1_w_in, ffn1_w_out, mix_norm, w_in, conv_dw, conv_b, conv_ln_g, conv_ln_b, sb_q_hnorm, sb_k_hnorm, mla_q_norm, mla_w_uq, mla_kv_norm, mla_w_ukv, mla_q_hnorm, mla_k_hnorm, mem_norm, mem_w_kv, mem_q_hnorm, mem_k_hnorm, w_branch, w_out, ffn2_norm, ffn2_w_in, ffn2_w_out, loss_target, m_ffn1_norm, m_ffn1_w_in, m_ffn1_w_out, m_mix_norm, m_w_in, m_conv_dw, m_conv_b, m_conv_ln_g, m_conv_ln_b, m_sb_q_hnorm, m_sb_k_hnorm, m_mla_q_norm, m_mla_w_uq, m_mla_kv_norm, m_mla_w_ukv, m_mla_q_hnorm, m_mla_k_hnorm, m_mem_norm, m_mem_w_kv, m_mem_q_hnorm, m_mem_k_hnorm, m_w_branch, m_w_out, m_ffn2_norm, m_ffn2_w_in, m_ffn2_w_out, v_ffn1_norm, v_ffn1_w_in, v_ffn1_w_out, v_mix_norm, v_w_in, v_conv_dw, v_conv_b, v_conv_ln_g, v_conv_ln_b, v_sb_q_hnorm, v_sb_k_hnorm, v_mla_q_norm, v_mla_w_uq, v_mla_kv_norm, v_mla_w_ukv, v_mla_q_hnorm, v_mla_k_hnorm, v_mem_norm, v_mem_w_kv, v_mem_q_hnorm, v_mem_k_hnorm, v_w_branch, v_w_out, v_ffn2_norm, v_ffn2_w_in, v_ffn2_w_out):
    w = dict(zip(WEIGHTS, (ffn1_norm, ffn1_w_in, ffn1_w_out, mix_norm, w_in, conv_dw, conv_b, conv_ln_g, conv_ln_b, sb_q_hnorm, sb_k_hnorm, mla_q_norm, mla_w_uq, mla_kv_norm, mla_w_ukv, mla_q_hnorm, mla_k_hnorm, mem_norm, mem_w_kv, mem_q_hnorm, mem_k_hnorm, w_branch, w_out, ffn2_norm, ffn2_w_in, ffn2_w_out)))
    m = dict(zip(WEIGHTS, (m_ffn1_norm, m_ffn1_w_in, m_ffn1_w_out, m_mix_norm, m_w_in, m_conv_dw, m_conv_b, m_conv_ln_g, m_conv_ln_b, m_sb_q_hnorm, m_sb_k_hnorm, m_mla_q_norm, m_mla_w_uq, m_mla_kv_norm, m_mla_w_ukv, m_mla_q_hnorm, m_mla_k_hnorm, m_mem_norm, m_mem_w_kv, m_mem_q_hnorm, m_mem_k_hnorm, m_w_branch, m_w_out, m_ffn2_norm, m_ffn2_w_in, m_ffn2_w_out)))
    v = dict(zip(WEIGHTS, (v_ffn1_norm, v_ffn1_w_in, v_ffn1_w_out, v_mix_norm, v_w_in, v_conv_dw, v_conv_b, v_conv_ln_g, v_conv_ln_b, v_sb_q_hnorm, v_sb_k_hnorm, v_mla_q_norm, v_mla_w_uq, v_mla_kv_norm, v_mla_w_ukv, v_mla_q_hnorm, v_mla_k_hnorm, v_mem_norm, v_mem_w_kv, v_mem_q_hnorm, v_mem_k_hnorm, v_w_branch, v_w_out, v_ffn2_norm, v_ffn2_w_in, v_ffn2_w_out)))
    S = x.shape[1]
    full = gather_weights(w)
    full.update({n: w[n] for n in SMALL})
    loss_blk, dx, g = local_step(x[0], mem[0], positions.reshape(S, 1), loss_target[0], full)
    loss = lax.psum(loss_blk[0, 0], ("x", "y", "c"))
    grads = reduce_grads(g, {n: w[n].shape for n in SHARDED})
    small_shapes = {n: w[n].shape for n in SMALL}
    g_small = {n: jnp.stack([gl[n] for gl in g]) for n in SMALL}
    g_small = sum8("small_sum", all8_gather("small_gather", _small_pack(g_small)))
    grads.update(_small_unpack(g_small, small_shapes))
    delta, new_m, new_v = {}, {}, {}
    for n in SHARDED:
        delta[n], new_m[n], new_v[n] = adamw(f"adamw_{n}", w[n], grads[n], m[n], v[n])
    d_s, m_s, v_s = adamw("adamw_small", _small_pack(w), g_small, _small_pack(m), _small_pack(v))
    for t, packed in ((delta, d_s), (new_m, m_s), (new_v, v_s)):
        t.update(_small_unpack(packed, small_shapes))
    return (loss, dx.reshape(x.shape), *[grads[n] for n in WEIGHTS], *[delta[n] for n in WEIGHTS],
            *[new_m[n] for n in WEIGHTS], *[new_v[n] for n in WEIGHTS])
```

```python
import math

import jax
import jax.numpy as jnp
from jax import lax
from jax.experimental import pallas as pl
from jax.experimental.pallas import tpu as pltpu

f32, bf16 = jnp.float32, jnp.bfloat16

D_MODEL = 1024
DEPTH = 4
CHUNK = 64
FFN_HIDDEN = 2048
CONV_CH = 512
CONV_WIDTH = 31
HEADS = 4
HEAD_DIM = 128
MLA_NOPE = 128
MLA_ROPE = 64
MLA_QK = MLA_NOPE + MLA_ROPE
MLA_PAD = 256
MLA_Q_LORA = 256
MLA_KV_LORA = 256
N_BRANCH = 4
BRANCH_WIDTH = 512
ROPE_BASE = 10000.0
EPS = 1e-6
NEG_INF = -1e30
IN_WIDTH = 7744
U_WIDTH = 8192
_U_SEGS = (("gates", 3648, 4096, 0), ("conv", 0, 1024, 4096), ("sb", 1024, 1536, 5120), ("qlat", 2560, 256, 6656),
           ("kvlat", 2816, 256, 6912), ("memq", 3136, 512, 7168), ("krope", 3072, 64, 7680))
U_PAD_FROM = 7744

ADAM_LR, ADAM_B1, ADAM_B2, ADAM_EPS, ADAM_WD, ADAM_STEP = 0.001, 0.9, 0.999, 1e-08, 0.01, 10

VMEM_LIMIT = 48 * 1024 * 1024
MESH = pl.DeviceIdType.MESH

SHARDED = ("ffn1_w_in", "ffn1_w_out", "w_in", "conv_dw", "mla_w_uq", "mla_w_ukv", "mem_w_kv", "w_branch", "w_out",
           "ffn2_w_in", "ffn2_w_out")
SHARD_AXIS = {"ffn1_w_in": 2, "ffn1_w_out": 1, "w_in": 2, "conv_dw": 2, "mla_w_uq": 2, "mla_w_ukv": 2, "mem_w_kv": 1,
              "w_branch": 3, "w_out": 1, "ffn2_w_in": 2, "ffn2_w_out": 1}
SMALL = ("ffn1_norm", "mix_norm", "conv_b", "conv_ln_g", "conv_ln_b", "sb_q_hnorm", "sb_k_hnorm", "mla_q_norm",
         "mla_kv_norm", "mla_q_hnorm", "mla_k_hnorm", "mem_norm", "mem_q_hnorm", "mem_k_hnorm", "ffn2_norm")
WEIGHTS = ("ffn1_norm", "ffn1_w_in", "ffn1_w_out", "mix_norm", "w_in", "conv_dw", "conv_b", "conv_ln_g", "conv_ln_b",
           "sb_q_hnorm", "sb_k_hnorm", "mla_q_norm", "mla_w_uq", "mla_kv_norm", "mla_w_ukv", "mla_q_hnorm",
           "mla_k_hnorm", "mem_norm", "mem_w_kv", "mem_q_hnorm", "mem_k_hnorm", "w_branch", "w_out", "ffn2_norm",
           "ffn2_w_in", "ffn2_w_out")
PACK_GRAIN = 2 * 256 * 1024
SMALL_PAD = 8 * 128


def _params(sem, vmem=VMEM_LIMIT):
    return pltpu.CompilerParams(dimension_semantics=sem, vmem_limit_bytes=vmem)


def _pick(n, pref):
    for t in pref:
        if n % t == 0:
            return t
    return n


def mm(name, a, b, form, *, out_dtype=f32, alpha=1.0, res=None, tm=None, tn=None, tk=None):
    if form == "nn":
        (M, K), (K2, N) = a.shape, b.shape
    elif form == "nt":
        (M, K), (N, K2) = a.shape, b.shape
    else:
        (K, M), (K2, N) = a.shape, b.shape
    assert K == K2, (name, a.shape, b.shape)
    tm = tm or _pick(M, (1024, 512, 256, 128))
    tn = tn or _pick(N, (1024, 512, 256, 128))
    tk = tk or _pick(K, (2048, 1024, 512, 256))
    nk = K // tk
    if form == "nn":
        a_spec = pl.BlockSpec((tm, tk), lambda i, j, k: (i, k))
        b_spec = pl.BlockSpec((tk, tn), lambda i, j, k: (k, j))
        dims = (((1,), (0,)), ((), ()))
    elif form == "nt":
        a_spec = pl.BlockSpec((tm, tk), lambda i, j, k: (i, k))
        b_spec = pl.BlockSpec((tn, tk), lambda i, j, k: (j, k))
        dims = (((1,), (1,)), ((), ()))
    else:
        a_spec = pl.BlockSpec((tk, tm), lambda i, j, k: (k, i))
        b_spec = pl.BlockSpec((tk, tn), lambda i, j, k: (k, j))
        dims = (((0,), (0,)), ((), ()))
    o_spec = pl.BlockSpec((tm, tn), lambda i, j, k: (i, j))
    has_res = res is not None

    def body(a_ref, b_ref, *rest):
        if has_res:
            r_ref, o_ref, acc_ref = rest
        else:
            o_ref, acc_ref = rest
        k = pl.program_id(2)
        part = lax.dot_general(a_ref[...].astype(bf16), b_ref[...].astype(bf16), dims, preferred_element_type=f32)

        def finish(acc):
            r = acc if alpha == 1.0 else acc * alpha
            if has_res:
                r = r_ref[...].astype(f32) + r
            o_ref[...] = r.astype(o_ref.dtype)

        if nk == 1:
            finish(part)
        else:
            @pl.when(k == 0)
            def _():
                acc_ref[...] = part

            @pl.when(k > 0)
            def _():
                acc_ref[...] += part

            @pl.when(k == nk - 1)
            def _():
                finish(acc_ref[...])

    ins = [a, b] + ([res] if has_res else [])
    in_specs = [a_spec, b_spec] + ([o_spec] if has_res else [])
    return pl.pallas_call(
        body, name=name, grid=(M // tm, N // tn, nk), in_specs=in_specs, out_specs=o_spec,
        out_shape=jax.ShapeDtypeStruct((M, N), out_dtype), scratch_shapes=[pltpu.VMEM((tm, tn), f32)],
        compiler_params=_params(("parallel", "parallel", "arbitrary")))(*ins)


def _row_spec(T, w, off, st):
    return pl.BlockSpec((T, w), lambda i, h: (i, off + st * h))


def _full_spec(p):
    return pl.BlockSpec(p.shape, lambda i, h: (0,) * p.ndim)


def rowwise(name, fn, rows, params, outs, *, T=512, H=1):
    S = rows[0][0].shape[0]
    T = min(T, S)
    n_r, n_p = len(rows), len(params)

    def body(*refs):
        vals = [r[...] for r in refs[:n_r + n_p]]
        res = fn(*vals)
        for o_ref, r in zip(refs[n_r + n_p:], res):
            o_ref[...] = r.astype(o_ref.dtype)

    res = pl.pallas_call(
        body, name=name, grid=(S // T, H),
        in_specs=[_row_spec(T, w, off, st) for (_, w, off, st) in rows] + [_full_spec(p) for p in params],
        out_specs=[_row_spec(T, w, off, st) for (_, _, w, off, st) in outs],
        out_shape=[jax.ShapeDtypeStruct((S, tw), dt) for (tw, dt, _, _, _) in outs],
        compiler_params=_params(("parallel", "arbitrary")))(*[r[0] for r in rows], *params)
    return res


def rowwise_bwd(name, fn, rows, params, douts, drows, *, T=512, H=1, nondiff=(), merge=None):
    S = rows[0][0].shape[0]
    T = min(T, S)
    n_r, n_p, n_d = len(rows), len(params), len(douts)
    diff_idx = [k for k in range(n_r) if k not in nondiff]
    merge = merge or [(j,) for j in range(len(diff_idx))]
    shared = [H > 1 and rows[diff_idx[pos[0]]][3] == 0 for pos in merge]
    n_o = len(merge)
    assert n_o == len(drows)

    def body(*refs):
        i, h = pl.program_id(0), pl.program_id(1)
        row_vals = [r[...].astype(f32) for r in refs[:n_r]]
        p_vals = [r[...].astype(f32) for r in refs[n_r:n_r + n_p]]
        d_vals = [r[...].astype(f32) for r in refs[n_r + n_p:n_r + n_p + n_d]]
        out_refs = refs[n_r + n_p + n_d:]

        def f(*args):
            full = list(row_vals)
            for k, v in zip(diff_idx, args[:len(diff_idx)]):
                full[k] = v
            return tuple(fn(*full, *args[len(diff_idx):]))

        _, vjp = jax.vjp(f, *[row_vals[k] for k in diff_idx], *p_vals)
        cts = vjp(tuple(d_vals))
        row_cts = [cts[pos[0]] if len(pos) == 1 else jnp.concatenate([cts[j] for j in pos], axis=-1) for pos in merge]
        for o_ref, ct, sh in zip(out_refs[:n_o], row_cts, shared):
            if sh:
                @pl.when(h == 0)
                def _():
                    o_ref[...] = ct.astype(o_ref.dtype)

                @pl.when(h > 0)
                def _():
                    o_ref[...] += ct.astype(o_ref.dtype)
            else:
                o_ref[...] = ct.astype(o_ref.dtype)
        first = jnp.logical_and(i == 0, h == 0)
        for o_ref, ct in zip(out_refs[n_o:], cts[len(diff_idx):]):
            @pl.when(first)
            def _():
                o_ref[...] = ct

            @pl.when(jnp.logical_not(first))
            def _():
                o_ref[...] += ct

    res = pl.pallas_call(
        body, name=name, grid=(S // T, H),
        in_specs=[_row_spec(T, w, off, st) for (_, w, off, st) in rows] + [_full_spec(p) for p in params]
        + [_row_spec(T, w, off, st) for (_, w, off, st) in douts],
        out_specs=[_row_spec(T, w, off, st) for (_, _, w, off, st) in drows] + [_full_spec(p) for p in params],
        out_shape=[jax.ShapeDtypeStruct((S, tw), dt) for (tw, dt, _, _, _) in drows]
        + [jax.ShapeDtypeStruct(p.shape, f32) for p in params],
        compiler_params=_params(("arbitrary", "arbitrary")))(*[r[0] for r in rows], *params, *[d[0] for d in douts])
    return res[:n_o], res[n_o:]


def _rms(x, g, n=None):
    x = x.astype(f32)
    n = n or x.shape[-1]
    return x * lax.rsqrt(jnp.sum(x * x, axis=-1, keepdims=True) * (1.0 / n) + EPS) * g.astype(f32)


def _sigmoid(x):
    return 1.0 / (1.0 + jnp.exp(-x))


def _silu(x):
    return x * _sigmoid(x)


def fn_rms(x, g):
    return (_rms(x, g),)


def fn_rms_res(x, g):
    return (x.astype(f32), _rms(x, g))


def fn_swiglu(gate, up):
    return (_silu(gate.astype(f32)) * up.astype(f32),)


def fn_sb_prep(q, k, gq, gk):
    return (_rms(q, gq), _rms(k, gk))


def fn_ln_silu(y, b, g, beta):
    y = y.astype(f32) + b
    mu = jnp.mean(y, axis=-1, keepdims=True)
    var = jnp.mean(jnp.square(y - mu), axis=-1, keepdims=True)
    return (_silu((y - mu) * lax.rsqrt(var + EPS) * g + beta),)


def fn_merge(g0, g1, g2, g3, p0, p1, p2, p3):
    out = _sigmoid(g0.astype(f32)) * p0.astype(f32)
    for g, p in ((g1, p1), (g2, p2), (g3, p3)):
        out = out + _sigmoid(g.astype(f32)) * p.astype(f32)
    return (out,)


def _rot_fwd(x):
    z = jnp.zeros_like(x[:, :MLA_NOPE])
    h = MLA_ROPE // 2
    return jnp.concatenate([z, -x[:, MLA_NOPE + h:MLA_QK], x[:, MLA_NOPE:MLA_NOPE + h], z[:, :MLA_PAD - MLA_QK]], axis=-1)


def _rot_bwd(g):
    z = jnp.zeros_like(g[:, :MLA_NOPE])
    h = MLA_ROPE // 2
    return jnp.concatenate([z, g[:, MLA_NOPE + h:MLA_QK], -g[:, MLA_NOPE:MLA_NOPE + h], z[:, :MLA_PAD - MLA_QK]], axis=-1)


@jax.custom_vjp
def _rope(x, c, s):
    return x * c + _rot_fwd(x) * s


def _rope_f(x, c, s):
    return _rope(x, c, s), (c, s)


def _rope_b(res, g):
    c, s = res
    return g * c + _rot_bwd(g * s), jnp.zeros_like(c), jnp.zeros_like(s)


_rope.defvjp(_rope_f, _rope_b)


def fn_mla_q(q, c, s, gain):
    return (_rope(_rms(q, gain, MLA_QK), c, s),)


def fn_mla_k(kn, kr, c, s, gain):
    k = jnp.concatenate([kn.astype(f32), kr.astype(f32)], axis=-1)
    return (_rope(_rms(k, gain, MLA_QK), c, s),)


def fn_mla_k_v(kn, kr, v, c, s, gain):
    return (fn_mla_k(kn, kr, c, s, gain)[0], v.astype(f32))


def fn_mem_k_v(k, v, gain):
    return (_rms(k, gain), v.astype(f32))


def fn_adamw(w, g, m, v):
    m = ADAM_B1 * m + (1.0 - ADAM_B1) * g
    v = ADAM_B2 * v + (1.0 - ADAM_B2) * jnp.square(g)
    m_hat = m / (1.0 - ADAM_B1 ** ADAM_STEP)
    v_hat = v / (1.0 - ADAM_B2 ** ADAM_STEP)
    delta = -ADAM_LR * (m_hat / (jnp.sqrt(v_hat) + ADAM_EPS) + ADAM_WD * w)
    return delta, m, v


def _head_spec(rows, w, off, st):
    return pl.BlockSpec((rows, w), lambda h, i: (0, off + st * h))


def _qblk_spec(B, w, off, st):
    return pl.BlockSpec((B, w), lambda h, i: (i, off + st * h))


def _chunk_mask(tq, tk, d):
    r = lax.broadcasted_iota(jnp.int32, (tq, tk), 0) // CHUNK
    c = (d * tk + lax.broadcasted_iota(jnp.int32, (tq, tk), 1)) // CHUNK
    return c <= r


def _strict_mask(tq, tk, d):
    r = lax.broadcasted_iota(jnp.int32, (tq, tk), 0)
    c = d * tk + lax.broadcasted_iota(jnp.int32, (tq, tk), 1)
    return c < r


_NT = (((1,), (1,)), ((), ()))
_TN = (((0,), (0,)), ((), ()))


def _tiles(Sq, Sk, mask, tq, tk):
    tq = min(tq, Sq)
    if mask is None:
        return tq, Sk, 0
    tk = min(tk, tq)
    assert Sq == Sk and tq % tk == 0 and tk % CHUNK == 0
    return tq, tk, tq // tk


def attn_fwd(name, q, k, v, *, scale, mask, tq=1024, tk=512):
    Sq, Sk = q[0].shape[0], k[0].shape[0]
    tq, tk, nd = _tiles(Sq, Sk, mask, tq, tk)

    def body(q_ref, k_ref, v_ref, o_ref, lse_ref):
        i = pl.program_id(1)
        qv = q_ref[...]

        def block(off, carry, d):
            m, l, acc = carry
            kb, vb = k_ref[pl.ds(off, tk), :].astype(bf16), v_ref[pl.ds(off, tk), :].astype(bf16)
            s = lax.dot_general(qv, kb, _NT, preferred_element_type=f32) * scale
            if d is not None:
                s = jnp.where(_chunk_mask(tq, tk, d), s, NEG_INF)
            m_new = jnp.maximum(m, jnp.max(s, axis=-1, keepdims=True))
            p = jnp.exp(s - m_new)
            corr = jnp.exp(m - m_new)
            l = l * corr + jnp.sum(p, axis=-1, keepdims=True)
            acc = acc * corr + jnp.dot(p.astype(bf16), vb, preferred_element_type=f32)
            return m_new, l, acc

        carry = (jnp.full((tq, 1), NEG_INF, f32), jnp.zeros((tq, 1), f32), jnp.zeros((tq, HEAD_DIM), f32))
        if nd:
            carry = lax.fori_loop(0, i * nd, lambda j, c: block(pl.multiple_of(j * tk, tk), c, None), carry)
            for d in range(nd):
                carry = block(pl.multiple_of(i * tq + d * tk, tk), carry, d)
        else:
            carry = block(0, carry, None)
        m, l, acc = carry
        o_ref[...] = (acc / l).astype(o_ref.dtype)
        lse_ref[...] = jnp.broadcast_to(m + jnp.log(l), (tq, HEAD_DIM))

    return pl.pallas_call(
        body, name=name, grid=(HEADS, Sq // tq),
        in_specs=[_qblk_spec(tq, *q[1:]), _head_spec(Sk, *k[1:]), _head_spec(Sk, *v[1:])],
        out_specs=[_qblk_spec(tq, HEAD_DIM, 0, 1), _qblk_spec(tq, HEAD_DIM, 0, 1)],
        out_shape=[jax.ShapeDtypeStruct((Sq, HEADS * HEAD_DIM), f32), jax.ShapeDtypeStruct((Sq, HEADS * HEAD_DIM), f32)],
        compiler_params=_params(("parallel", "arbitrary")))(q[0], k[0], v[0])


def attn_bwd(name, q, k, v, o, do, lse, *, scale, mask, tq=1024, tk=512):
    Sq, Sk = q[0].shape[0], k[0].shape[0]
    tq, tk, nd = _tiles(Sq, Sk, mask, tq, tk)
    dq_w = q[1]

    def body(q_ref, k_ref, v_ref, o_ref, do_ref, lse_ref, dq_ref, dk_ref, dv_ref):
        i = pl.program_id(1)

        @pl.when(i == 0)
        def _():
            dk_ref[...] = jnp.zeros_like(dk_ref)
            dv_ref[...] = jnp.zeros_like(dv_ref)

        qv, dov = q_ref[...], do_ref[...].astype(bf16)
        delta = jnp.sum(do_ref[...].astype(f32) * o_ref[...].astype(f32), axis=-1, keepdims=True)
        lse_v = lse_ref[:, :1]

        def block(off, dq_acc, d):
            kb, vb = k_ref[pl.ds(off, tk), :].astype(bf16), v_ref[pl.ds(off, tk), :].astype(bf16)
            s = lax.dot_general(qv, kb, _NT, preferred_element_type=f32) * scale
            if d is not None:
                s = jnp.where(_chunk_mask(tq, tk, d), s, NEG_INF)
            p = jnp.exp(s - lse_v)
            dv_ref[pl.ds(off, tk), :] += lax.dot_general(p.astype(bf16), dov, _TN, preferred_element_type=f32)
            dp = lax.dot_general(dov, vb, _NT, preferred_element_type=f32)
            ds = (p * (dp - delta) * scale).astype(bf16)
            dk_ref[pl.ds(off, tk), :] += lax.dot_general(ds, qv, _TN, preferred_element_type=f32)
            return dq_acc + jnp.dot(ds, kb, preferred_element_type=f32)

        acc = jnp.zeros((tq, dq_w), f32)
        if nd:
            acc = lax.fori_loop(0, i * nd, lambda j, c: block(pl.multiple_of(j * tk, tk), c, None), acc)
            for d in range(nd):
                acc = block(pl.multiple_of(i * tq + d * tk, tk), acc, d)
        else:
            acc = block(0, acc, None)
        dq_ref[...] = acc.astype(dq_ref.dtype)

    hd = _qblk_spec(tq, HEAD_DIM, 0, 1)
    return pl.pallas_call(
        body, name=name, grid=(HEADS, Sq // tq),
        in_specs=[_qblk_spec(tq, *q[1:]), _head_spec(Sk, *k[1:]), _head_spec(Sk, *v[1:]), hd, hd, hd],
        out_specs=[_qblk_spec(tq, dq_w, 0, 1), _head_spec(Sk, dq_w, 0, 1), _head_spec(Sk, HEAD_DIM, 0, 1)],
        out_shape=[jax.ShapeDtypeStruct((Sq, HEADS * dq_w), f32), jax.ShapeDtypeStruct((Sk, HEADS * dq_w), f32),
                   jax.ShapeDtypeStruct((Sk, HEADS * HEAD_DIM), f32)],
        compiler_params=_params(("arbitrary", "arbitrary"), 56 * 1024 * 1024))(q[0], k[0], v[0], o, do, lse)


SB_LOG_ZERO = -104.0


def _tri(B, rel):
    r = lax.broadcasted_iota(jnp.int32, (B, B), 0)
    c = lax.broadcasted_iota(jnp.int32, (B, B), 1)
    return rel(r, c).astype(bf16)


def _sb_scores(qv, kb, scale):
    z = lax.dot_general(qv, kb, _NT, preferred_element_type=f32) * scale
    e = jnp.exp(-jnp.abs(z))
    log_keep = -(jnp.maximum(z, 0.0) + jnp.log(1.0 + e))
    return z, e, log_keep


def _split_dot(x, m):
    hi = x.astype(bf16)
    lo = (x - hi.astype(f32)).astype(bf16)
    return jnp.dot(hi, m, preferred_element_type=f32) + jnp.dot(lo, m, preferred_element_type=f32)


def sb_fwd(name, q, k, v, *, tq=512, tk=256):
    S = q[0].shape[0]
    tq, tk, nd = _tiles(S, S, "strict", tq, tk)
    scale = HEAD_DIM ** -0.5
    m_ex = _tri(tk, lambda j, s: j > s)

    def body(q_ref, k_ref, v_ref, mex_ref, o_ref, tot_ref, cnt_ref):
        i = pl.program_id(1)
        qv, mex = q_ref[...], mex_ref[...]

        def block(off, carry, d):
            later, acc = carry
            kb, vb = k_ref[pl.ds(off, tk), :].astype(bf16), v_ref[pl.ds(off, tk), :].astype(bf16)
            z, _, lk = _sb_scores(qv, kb, scale)
            if d is not None:
                lk = jnp.where(_strict_mask(tq, tk, d), lk, 0.0)
            a = jnp.exp(z + lk + _split_dot(lk, mex) + later)
            if d is not None:
                a = jnp.where(_strict_mask(tq, tk, d), a, 0.0)
            acc = acc + jnp.dot(a.astype(bf16), vb, preferred_element_type=f32)
            return later + jnp.sum(lk, axis=-1, keepdims=True), acc

        carry = (jnp.zeros((tq, 1), f32), jnp.zeros((tq, HEAD_DIM), f32))
        for d in reversed(range(nd)):
            carry = block(pl.multiple_of(i * tq + d * tk, tk), carry, d)
        n_full = i * nd

        def more(state):
            t, later, _ = state
            return jnp.logical_and(t < n_full, jnp.max(later) > SB_LOG_ZERO)

        def step(state):
            t, later, acc = state
            later, acc = block(pl.multiple_of((n_full - 1 - t) * tk, tk), (later, acc), None)
            return t + 1, later, acc

        done, total, acc = lax.while_loop(more, step, (jnp.int32(0),) + carry)
        o_ref[...] = acc.astype(o_ref.dtype)
        tot_ref[...] = jnp.broadcast_to(total, (tq, HEAD_DIM))
        cnt_ref[...] = jnp.full((8, HEAD_DIM), done, f32)

    hd = _qblk_spec(tq, HEAD_DIM, 0, 1)
    return pl.pallas_call(
        body, name=name, grid=(HEADS, S // tq),
        in_specs=[_qblk_spec(tq, *q[1:]), _head_spec(S, *k[1:]), _head_spec(S, *v[1:]),
                  pl.BlockSpec((tk, tk), lambda h, i: (0, 0))],
        out_specs=[hd, hd, _qblk_spec(8, HEAD_DIM, 0, 1)],
        out_shape=[jax.ShapeDtypeStruct((S, HEADS * HEAD_DIM), f32), jax.ShapeDtypeStruct((S, HEADS * HEAD_DIM), f32),
                   jax.ShapeDtypeStruct((8 * (S // tq), HEADS * HEAD_DIM), f32)],
        compiler_params=_params(("parallel", "arbitrary")))(q[0], k[0], v[0], m_ex)


def sb_bwd(name, q, k, v, tot, cnt, do, *, tq=512, tk=256):
    S = q[0].shape[0]
    tq, tk, nd = _tiles(S, S, "strict", tq, tk)
    scale = HEAD_DIM ** -0.5
    m_le, m_lt = _tri(tk, lambda j, s: j <= s), _tri(tk, lambda j, s: j < s)

    def body(q_ref, k_ref, v_ref, tot_ref, cnt_ref, do_ref, mle_ref, mlt_ref, dq_ref, dk_ref, dv_ref):
        i = pl.program_id(1)

        @pl.when(i == 0)
        def _():
            dk_ref[...] = jnp.zeros_like(dk_ref)
            dv_ref[...] = jnp.zeros_like(dv_ref)

        qv, dov, mle, mlt = q_ref[...], do_ref[...].astype(bf16), mle_ref[...], mlt_ref[...]
        total = tot_ref[:, :1]

        def block(off, carry, d):
            before, g_before, dq_acc = carry
            kb, vb = k_ref[pl.ds(off, tk), :].astype(bf16), v_ref[pl.ds(off, tk), :].astype(bf16)
            z, e, lk = _sb_scores(qv, kb, scale)
            sig = jnp.where(z >= 0, 1.0, e) / (1.0 + e)
            if d is not None:
                lk = jnp.where(_strict_mask(tq, tk, d), lk, 0.0)
            later = (total - before) - _split_dot(lk, mle)
            a = jnp.exp(z + lk + later)
            if d is not None:
                a = jnp.where(_strict_mask(tq, tk, d), a, 0.0)
            g = a * lax.dot_general(dov, vb, _NT, preferred_element_type=f32)
            prefix = g_before + jnp.dot(g.astype(bf16), mlt, preferred_element_type=f32)
            dz = (g * (1.0 - sig) - prefix * sig) * scale
            if d is not None:
                dz = jnp.where(_strict_mask(tq, tk, d), dz, 0.0)
            dzb = dz.astype(bf16)
            dk_ref[pl.ds(off, tk), :] += lax.dot_general(dzb, qv, _TN, preferred_element_type=f32)
            dv_ref[pl.ds(off, tk), :] += lax.dot_general(a.astype(bf16), dov, _TN, preferred_element_type=f32)
            return (before + jnp.sum(lk, axis=-1, keepdims=True), g_before + jnp.sum(g, axis=-1, keepdims=True),
                    dq_acc + jnp.dot(dzb, kb, preferred_element_type=f32))

        zero = jnp.zeros((tq, 1), f32)
        first = i * nd - jnp.max(cnt_ref[...]).astype(jnp.int32)
        carry = lax.fori_loop(first, i * nd, lambda j, c: block(pl.multiple_of(j * tk, tk), c, None),
                              (zero, zero, jnp.zeros((tq, HEAD_DIM), f32)))
        for d in range(nd):
            carry = block(pl.multiple_of(i * tq + d * tk, tk), carry, d)
        dq_ref[...] = carry[2].astype(dq_ref.dtype)

    hd = _qblk_spec(tq, HEAD_DIM, 0, 1)
    tri = pl.BlockSpec((tk, tk), lambda h, i: (0, 0))
    return pl.pallas_call(
        body, name=name, grid=(HEADS, S // tq),
        in_specs=[_qblk_spec(tq, *q[1:]), _head_spec(S, *k[1:]), _head_spec(S, *v[1:]), hd, _qblk_spec(8, HEAD_DIM, 0, 1),
                  hd, tri, tri],
        out_specs=[hd, _head_spec(S, HEAD_DIM, 0, 1), _head_spec(S, HEAD_DIM, 0, 1)],
        out_shape=[jax.ShapeDtypeStruct((S, HEADS * HEAD_DIM), f32), jax.ShapeDtypeStruct((S, HEADS * HEAD_DIM), f32),
                   jax.ShapeDtypeStruct((S, HEADS * HEAD_DIM), f32)],
        compiler_params=_params(("arbitrary", "arbitrary")))(q[0], k[0], v[0], tot, cnt, do, m_le, m_lt)


CONV_HALO = 32
CONV_A_BLK, CONV_G_BLK = 8, 9


def _glu(a, g):
    return a.astype(f32) * _sigmoid(g.astype(f32))


def conv_fwd(name, u, dw):
    S = u.shape[0]
    T = min(512, S)
    nT = S // T

    def body(a_ref, g_ref, ap_ref, gp_ref, dw_ref, y_ref, ext_ref):
        i = pl.program_id(0)
        prev = _glu(ap_ref[T - CONV_HALO:, :], gp_ref[T - CONV_HALO:, :])
        ext_ref[:CONV_HALO, :] = jnp.where(i > 0, prev, 0.0)
        ext_ref[CONV_HALO:, :] = _glu(a_ref[...], g_ref[...])
        acc = jnp.zeros((T, CONV_CH), f32)
        for w in range(CONV_WIDTH):
            acc = acc + dw_ref[w:w + 1, :] * ext_ref[pl.ds(w + CONV_HALO - (CONV_WIDTH - 1), T), :]
        y_ref[...] = acc

    cur = lambda blk: pl.BlockSpec((T, CONV_CH), lambda i: (i, blk))
    prv = lambda blk: pl.BlockSpec((T, CONV_CH), lambda i: (jnp.maximum(i - 1, 0), blk))
    return pl.pallas_call(
        body, name=name, grid=(nT,),
        in_specs=[cur(CONV_A_BLK), cur(CONV_G_BLK), prv(CONV_A_BLK), prv(CONV_G_BLK),
                  pl.BlockSpec(dw.shape, lambda i: (0, 0))],
        out_specs=pl.BlockSpec((T, CONV_CH), lambda i: (i, 0)),
        out_shape=jax.ShapeDtypeStruct((S, CONV_CH), f32),
        scratch_shapes=[pltpu.VMEM((T + CONV_HALO, CONV_CH), f32)],
        compiler_params=_params(("arbitrary",)))(u, u, u, u, dw)


def conv_bwd(name, u, dy, dw):
    S = u.shape[0]
    T = min(512, S)
    nT = S // T
    lead = CONV_HALO - (CONV_WIDTH - 1)

    def body(a_ref, g_ref, ap_ref, gp_ref, dy_ref, dyn_ref, dw_ref, du_ref, ddw_ref, ext_ref, dext_ref):
        i = pl.program_id(0)
        prev = _glu(ap_ref[T - CONV_HALO:, :], gp_ref[T - CONV_HALO:, :])
        ext_ref[:CONV_HALO, :] = jnp.where(i > 0, prev, 0.0)
        a, sg = a_ref[...].astype(f32), _sigmoid(g_ref[...].astype(f32))
        ext_ref[CONV_HALO:, :] = a * sg
        dyv = dy_ref[...]
        dext_ref[:T, :] = dyv
        dext_ref[T:, :] = jnp.where(i < nT - 1, dyn_ref[:CONV_HALO, :], 0.0)
        @pl.when(i == 0)
        def _():
            ddw_ref[...] = jnp.zeros_like(ddw_ref)

        dglu = jnp.zeros((T, CONV_CH), f32)
        for w in range(CONV_WIDTH):
            dglu = dglu + dw_ref[w:w + 1, :] * dext_ref[pl.ds(CONV_WIDTH - 1 - w, T), :]
            ddw_ref[w:w + 1, :] += jnp.sum(dyv * ext_ref[pl.ds(w + lead, T), :], axis=0, keepdims=True)

        du_ref[:, :CONV_CH] = (dglu * sg).astype(du_ref.dtype)
        du_ref[:, CONV_CH:] = (dglu * a * sg * (1.0 - sg)).astype(du_ref.dtype)

    cur = lambda blk: pl.BlockSpec((T, CONV_CH), lambda i: (i, blk))
    prv = lambda blk: pl.BlockSpec((T, CONV_CH), lambda i: (jnp.maximum(i - 1, 0), blk))
    return pl.pallas_call(
        body, name=name, grid=(nT,),
        in_specs=[cur(CONV_A_BLK), cur(CONV_G_BLK), prv(CONV_A_BLK), prv(CONV_G_BLK),
                  pl.BlockSpec((T, CONV_CH), lambda i: (i, 0)),
                  pl.BlockSpec((T, CONV_CH), lambda i: (jnp.minimum(i + 1, nT - 1), 0)),
                  pl.BlockSpec(dw.shape, lambda i: (0, 0))],
        out_specs=[pl.BlockSpec((T, 2 * CONV_CH), lambda i: (i, 0)), pl.BlockSpec(dw.shape, lambda i: (0, 0))],
        out_shape=[jax.ShapeDtypeStruct((S, 2 * CONV_CH), bf16), jax.ShapeDtypeStruct(dw.shape, f32)],
        scratch_shapes=[pltpu.VMEM((T + CONV_HALO, CONV_CH), f32), pltpu.VMEM((T + CONV_HALO, CONV_CH), f32)],
        compiler_params=_params(("arbitrary",)))(u, u, u, u, dy, dy, dw)


def rope_tables(pos_col):
    S = pos_col.shape[0]
    T = min(512, S)
    inv_freq = ROPE_BASE ** (-jnp.arange(0, MLA_ROPE, 2, dtype=f32) / MLA_ROPE)
    zeros = jnp.zeros((MLA_NOPE,), f32)
    inv_row = jnp.concatenate([zeros, inv_freq, inv_freq, zeros[:MLA_PAD - MLA_QK]]).reshape(1, MLA_PAD)

    def body(p_ref, f_ref, c_ref, s_ref):
        lane = lax.broadcasted_iota(jnp.int32, (T, MLA_PAD), 1)
        ang = p_ref[...].astype(f32) * f_ref[...]
        rot = jnp.logical_and(lane >= MLA_NOPE, lane < MLA_QK)
        c_ref[...] = jnp.where(rot, jnp.cos(ang), jnp.where(lane < MLA_NOPE, 1.0, 0.0))
        s_ref[...] = jnp.where(rot, jnp.sin(ang), 0.0)

    spec = pl.BlockSpec((T, MLA_PAD), lambda i: (i, 0))
    return pl.pallas_call(
        body, name="rope_tables", grid=(S // T,),
        in_specs=[pl.BlockSpec((T, 1), lambda i: (i, 0)), pl.BlockSpec((1, MLA_PAD), lambda i: (0, 0))],
        out_specs=[spec, spec], out_shape=[jax.ShapeDtypeStruct((S, MLA_PAD), f32)] * 2,
        compiler_params=_params(("parallel",)))(pos_col, inv_row)


def loss_head(y, target):
    S, D = y.shape
    T = min(512, S)

    def body(y_ref, t_ref, dy_ref, l_ref):
        i = pl.program_id(0)
        err = y_ref[...] - t_ref[...]
        dy_ref[...] = err * (1.0 / D)
        part = 0.5 * jnp.sum(jnp.sum(err * err, axis=-1, keepdims=True) * (1.0 / D), axis=0, keepdims=True)
        part = jnp.broadcast_to(part, l_ref.shape)

        @pl.when(i == 0)
        def _():
            l_ref[...] = part

        @pl.when(i > 0)
        def _():
            l_ref[...] += part

    spec = pl.BlockSpec((T, D), lambda i: (i, 0))
    return pl.pallas_call(
        body, name="loss_head", grid=(S // T,), in_specs=[spec, spec],
        out_specs=[spec, pl.BlockSpec((8, 128), lambda i: (0, 0))],
        out_shape=[jax.ShapeDtypeStruct((S, D), f32), jax.ShapeDtypeStruct((8, 128), f32)],
        compiler_params=_params(("arbitrary",)))(y, target)


def _row(a, w=None, off=0, st=0):
    return (a, w or a.shape[1], off, st)


def _out(tw, dt, w=None, off=0, st=0):
    return (tw, dt, w or tw, off, st)


ACT = bf16


def ffn_fwd(tag, x, g, w_in, w_out):
    (h,) = rowwise(f"{tag}_rms", fn_rms, [_row(x)], [g], [_out(D_MODEL, bf16)])
    u = mm(f"{tag}_in", h, w_in, "nn", out_dtype=ACT)
    (a,) = rowwise(f"{tag}_swiglu", fn_swiglu, [_row(u, FFN_HIDDEN, 0), _row(u, FFN_HIDDEN, 1)], [],
                   [_out(FFN_HIDDEN, bf16)])
    y = mm(f"{tag}_out", a, w_out, "nn", alpha=0.5, res=x)
    return y, (x, h, u, a)


def ffn_bwd(tag, saved, g, w_in, w_out, dy):
    x, h, u, a = saved
    d_w_out = mm(f"{tag}_dwout", a, dy, "tn", alpha=0.5)
    da = mm(f"{tag}_da", dy, w_out, "nt", alpha=0.5, out_dtype=ACT)
    (du,), _ = rowwise_bwd(f"{tag}_dswiglu", fn_swiglu, [_row(u, FFN_HIDDEN, 0), _row(u, FFN_HIDDEN, 1)], [],
                           [_row(da)], [_out(2 * FFN_HIDDEN, bf16)], merge=[(0, 1)], T=256)
    d_w_in = mm(f"{tag}_dwin", h, du, "tn")
    dh = mm(f"{tag}_dh", du, w_in, "nt", out_dtype=ACT)
    (dx,), (dg,) = rowwise_bwd(f"{tag}_drms", fn_rms_res, [_row(x)], [g], [_row(dy), _row(dh)], [_out(D_MODEL, f32)])
    return dx, dg, d_w_in, d_w_out


def _seg(name):
    for n, _, w, start in _U_SEGS:
        if n == name:
            return start, w
    raise KeyError(name)


def mix_fwd(tag, x, mem_n_in, tabs, p):
    cos_t, sin_t = tabs
    (h,) = rowwise(f"{tag}_rms", fn_rms, [_row(x)], [p["mix_norm"]], [_out(D_MODEL, bf16)])
    u = mm(f"{tag}_in", h, p["w_in"], "nn", out_dtype=ACT)
    yc = conv_fwd(f"{tag}_conv", u, p["conv_dw"])
    (br_a,) = rowwise(f"{tag}_lnsilu", fn_ln_silu, [_row(yc)], [p["conv_b"], p["conv_ln_g"], p["conv_ln_b"]],
                      [_out(BRANCH_WIDTH, bf16)])
    sb0 = _seg("sb")[0] // HEAD_DIM
    qs, ks = rowwise(f"{tag}_sbprep", fn_sb_prep, [_row(u, HEAD_DIM, sb0, 1), _row(u, HEAD_DIM, sb0 + HEADS, 1)],
                     [p["sb_q_hnorm"], p["sb_k_hnorm"]],
                     [_out(BRANCH_WIDTH, bf16, HEAD_DIM, 0, 1), _out(BRANCH_WIDTH, bf16, HEAD_DIM, 0, 1)], H=HEADS)
    sb_v = _row(u, HEAD_DIM, sb0 + 2 * HEADS, 1)
    br_b, tot_b, cnt_b = sb_fwd(f"{tag}_sb", _row(qs, HEAD_DIM, 0, 1), _row(ks, HEAD_DIM, 0, 1), sb_v)
    ql_n, kvl_n = rowwise(f"{tag}_latrms", lambda a, b, ga, gb: (_rms(a, ga), _rms(b, gb)),
                          [_row(u, MLA_Q_LORA, _seg("qlat")[0] // MLA_Q_LORA), _row(u, MLA_KV_LORA, _seg("kvlat")[0] // MLA_KV_LORA)],
                          [p["mla_q_norm"], p["mla_kv_norm"]], [_out(MLA_Q_LORA, bf16), _out(MLA_KV_LORA, bf16)])
    qfull = mm(f"{tag}_uq", ql_n, p["mla_w_uq"], "nn", out_dtype=ACT)
    kvfull = mm(f"{tag}_ukv", kvl_n, p["mla_w_ukv"], "nn", out_dtype=ACT)
    kr_row = _row(u, HEAD_DIM, _seg("krope")[0] // HEAD_DIM, 0)
    (qr,) = rowwise(f"{tag}_mlaq", fn_mla_q, [_row(qfull, MLA_PAD, 0, 1), _row(cos_t), _row(sin_t)], [p["mla_q_hnorm"]],
                    [_out(HEADS * MLA_PAD, bf16, MLA_PAD, 0, 1)], H=HEADS)
    (kr,) = rowwise(f"{tag}_mlak", fn_mla_k, [_row(kvfull, HEAD_DIM, 0, 2), kr_row, _row(cos_t), _row(sin_t)],
                    [p["mla_k_hnorm"]], [_out(HEADS * MLA_PAD, bf16, MLA_PAD, 0, 1)], H=HEADS)
    mla_v = _row(kvfull, HEAD_DIM, 1, 2)
    br_c, lse_c = attn_fwd(f"{tag}_mla", _row(qr, MLA_PAD, 0, 1), _row(kr, MLA_PAD, 0, 1), mla_v,
                           scale=MLA_QK ** -0.5, mask="chunk")
    (mem_n,) = rowwise(f"{tag}_memrms", fn_rms, [_row(mem_n_in)], [p["mem_norm"]], [_out(D_MODEL, bf16)])
    kvm = mm(f"{tag}_memkv", mem_n, p["mem_w_kv"], "nn", out_dtype=ACT)
    (km,) = rowwise(f"{tag}_memk", fn_rms, [_row(kvm, HEAD_DIM, 0, 1)], [p["mem_k_hnorm"]],
                    [_out(BRANCH_WIDTH, bf16, HEAD_DIM, 0, 1)], H=HEADS)
    mq0 = _seg("memq")[0] // HEAD_DIM
    (qm,) = rowwise(f"{tag}_memq", fn_rms, [_row(u, HEAD_DIM, mq0, 1)], [p["mem_q_hnorm"]],
                    [_out(BRANCH_WIDTH, bf16, HEAD_DIM, 0, 1)], H=HEADS)
    mem_v = _row(kvm, HEAD_DIM, HEADS, 1)
    br_d, lse_d = attn_fwd(f"{tag}_memattn", _row(qm, HEAD_DIM, 0, 1), _row(km, HEAD_DIM, 0, 1), mem_v,
                           scale=HEAD_DIM ** -0.5, mask=None)
    branches = (br_a, br_b, br_c, br_d)
    proj = [mm(f"{tag}_branch{b}", branches[b], p["w_branch"][b], "nn", out_dtype=ACT) for b in range(N_BRANCH)]
    gate_rows = [_row(u, D_MODEL, b) for b in range(N_BRANCH)]
    (merged,) = rowwise(f"{tag}_merge", fn_merge, gate_rows + [_row(t) for t in proj], [], [_out(D_MODEL, bf16)], T=256)
    y = mm(f"{tag}_out", merged, p["w_out"], "nn", res=x)
    saved = dict(x=x, h=h, u=u, yc=yc, qs=qs, ks=ks, ql_n=ql_n, kvl_n=kvl_n, qfull=qfull, kvfull=kvfull, qr=qr, kr=kr,
                 lse_c=lse_c, mem_n=mem_n, kvm=kvm, km=km, qm=qm, lse_d=lse_d, branches=branches, proj=proj,
                 merged=merged, tot_b=tot_b, cnt_b=cnt_b)
    return y, saved


def mix_bwd(tag, sv, mem_n_in, tabs, p, dy):
    cos_t, sin_t = tabs
    u, S = sv["u"], sv["u"].shape[0]
    g = {}
    g["w_out"] = mm(f"{tag}_dwout", sv["merged"], dy, "tn")
    dmerged = mm(f"{tag}_dmerged", dy, p["w_out"], "nt", out_dtype=ACT)
    gate_rows = [_row(u, D_MODEL, b) for b in range(N_BRANCH)]
    d_merge, _ = rowwise_bwd(f"{tag}_dmerge", fn_merge, gate_rows + [_row(t) for t in sv["proj"]], [], [_row(dmerged)],
                             [_out(N_BRANCH * D_MODEL, bf16)] + [_out(D_MODEL, bf16)] * N_BRANCH, T=256,
                             merge=[tuple(range(N_BRANCH))] + [(N_BRANCH + b,) for b in range(N_BRANCH)])
    d_gates, d_proj = d_merge[:1], d_merge[1:]
    g["w_branch"] = [mm(f"{tag}_dwbranch{b}", sv["branches"][b], d_proj[b], "tn") for b in range(N_BRANCH)]
    d_br = [mm(f"{tag}_dbranch{b}", d_proj[b], p["w_branch"][b], "nt", out_dtype=ACT) for b in range(N_BRANCH)]
    (dyc,), (g["conv_b"], g["conv_ln_g"], g["conv_ln_b"]) = rowwise_bwd(
        f"{tag}_dlnsilu", fn_ln_silu, [_row(sv["yc"])], [p["conv_b"], p["conv_ln_g"], p["conv_ln_b"]], [_row(d_br[0])],
        [_out(BRANCH_WIDTH, f32)])
    du_conv, g["conv_dw"] = conv_bwd(f"{tag}_dconv", u, dyc, p["conv_dw"])
    sb0 = _seg("sb")[0] // HEAD_DIM
    sb_v = _row(u, HEAD_DIM, sb0 + 2 * HEADS, 1)
    dqs, dks, dv_sb = sb_bwd(f"{tag}_dsb", _row(sv["qs"], HEAD_DIM, 0, 1), _row(sv["ks"], HEAD_DIM, 0, 1), sb_v,
                             sv["tot_b"], sv["cnt_b"], d_br[1])
    (du_sbq, du_sbk), (g["sb_q_hnorm"], g["sb_k_hnorm"]) = rowwise_bwd(
        f"{tag}_dsbprep", fn_sb_prep, [_row(u, HEAD_DIM, sb0, 1), _row(u, HEAD_DIM, sb0 + HEADS, 1)],
        [p["sb_q_hnorm"], p["sb_k_hnorm"]], [_row(dqs, HEAD_DIM, 0, 1), _row(dks, HEAD_DIM, 0, 1)],
        [_out(BRANCH_WIDTH, bf16, HEAD_DIM, 0, 1), _out(BRANCH_WIDTH, bf16, HEAD_DIM, 0, 1)], H=HEADS)
    mla_v = _row(sv["kvfull"], HEAD_DIM, 1, 2)
    dqr, dkr, dv_mla = attn_bwd(f"{tag}_dmla", _row(sv["qr"], MLA_PAD, 0, 1), _row(sv["kr"], MLA_PAD, 0, 1), mla_v,
                                sv["branches"][2], d_br[2], sv["lse_c"], scale=MLA_QK ** -0.5, mask="chunk")
    (dqfull,), (g["mla_q_hnorm"],) = rowwise_bwd(
        f"{tag}_dmlaq", fn_mla_q, [_row(sv["qfull"], MLA_PAD, 0, 1), _row(cos_t), _row(sin_t)], [p["mla_q_hnorm"]],
        [_row(dqr, MLA_PAD, 0, 1)], [_out(HEADS * MLA_PAD, bf16, MLA_PAD, 0, 1)], H=HEADS, nondiff=(1, 2))
    kr_row = _row(u, HEAD_DIM, _seg("krope")[0] // HEAD_DIM, 0)
    (dkn, du_krope, dvp), (g["mla_k_hnorm"],) = rowwise_bwd(
        f"{tag}_dmlak", fn_mla_k_v, [_row(sv["kvfull"], HEAD_DIM, 0, 2), kr_row, mla_v, _row(cos_t), _row(sin_t)],
        [p["mla_k_hnorm"]], [_row(dkr, MLA_PAD, 0, 1), _row(dv_mla, HEAD_DIM, 0, 1)],
        [_out(BRANCH_WIDTH, bf16, HEAD_DIM, 0, 1), _out(HEAD_DIM, f32), _out(BRANCH_WIDTH, bf16, HEAD_DIM, 0, 1)],
        H=HEADS, nondiff=(3, 4))
    dkvfull = _interleave(f"{tag}_dkvfull", dkn, dvp)
    g["mla_w_uq"] = mm(f"{tag}_dwuq", sv["ql_n"], dqfull, "tn")
    g["mla_w_ukv"] = mm(f"{tag}_dwukv", sv["kvl_n"], dkvfull, "tn")
    dql_n = mm(f"{tag}_dqln", dqfull, p["mla_w_uq"], "nt", out_dtype=ACT)
    dkvl_n = mm(f"{tag}_dkvln", dkvfull, p["mla_w_ukv"], "nt", out_dtype=ACT)
    (du_lat,), (g["mla_q_norm"], g["mla_kv_norm"]) = rowwise_bwd(
        f"{tag}_dlatrms", lambda a, b, ga, gb: (_rms(a, ga), _rms(b, gb)),
        [_row(u, MLA_Q_LORA, _seg("qlat")[0] // MLA_Q_LORA), _row(u, MLA_KV_LORA, _seg("kvlat")[0] // MLA_KV_LORA)],
        [p["mla_q_norm"], p["mla_kv_norm"]], [_row(dql_n), _row(dkvl_n)], [_out(MLA_Q_LORA + MLA_KV_LORA, bf16)],
        merge=[(0, 1)])
    mem_v = _row(sv["kvm"], HEAD_DIM, HEADS, 1)
    dqm, dkm, dvm = attn_bwd(f"{tag}_dmemattn", _row(sv["qm"], HEAD_DIM, 0, 1), _row(sv["km"], HEAD_DIM, 0, 1), mem_v,
                             sv["branches"][3], d_br[3], sv["lse_d"], scale=HEAD_DIM ** -0.5, mask=None)
    mq0 = _seg("memq")[0] // HEAD_DIM
    (du_memq,), (g["mem_q_hnorm"],) = rowwise_bwd(
        f"{tag}_dmemq", fn_rms, [_row(u, HEAD_DIM, mq0, 1)], [p["mem_q_hnorm"]], [_row(dqm, HEAD_DIM, 0, 1)],
        [_out(BRANCH_WIDTH, bf16, HEAD_DIM, 0, 1)], H=HEADS)
    (dkvm_k, dkvm_v), (g["mem_k_hnorm"],) = rowwise_bwd(
        f"{tag}_dmemk", fn_mem_k_v, [_row(sv["kvm"], HEAD_DIM, 0, 1), mem_v], [p["mem_k_hnorm"]],
        [_row(dkm, HEAD_DIM, 0, 1), _row(dvm, HEAD_DIM, 0, 1)],
        [_out(BRANCH_WIDTH, bf16, HEAD_DIM, 0, 1), _out(BRANCH_WIDTH, bf16, HEAD_DIM, 0, 1)], H=HEADS)
    dkvm = jnp.concatenate([dkvm_k, dkvm_v], axis=1)
    g["mem_w_kv"] = mm(f"{tag}_dwmemkv", sv["mem_n"], dkvm, "tn")
    dmem_n = mm(f"{tag}_dmemn", dkvm, p["mem_w_kv"], "nt", out_dtype=ACT)
    _, (g["mem_norm"],) = rowwise_bwd(f"{tag}_dmemrms", fn_rms, [_row(mem_n_in)], [p["mem_norm"]], [_row(dmem_n)],
                                      [_out(D_MODEL, bf16)])
    du_krope_b = du_krope.astype(bf16)
    du = jnp.concatenate(list(d_gates) + [du_conv, du_sbq, du_sbk, dv_sb.astype(bf16), du_lat, du_memq,
                                          du_krope_b, jnp.zeros((S, U_WIDTH - _seg("krope")[0] - HEAD_DIM), bf16)], axis=1)
    g["w_in"] = mm(f"{tag}_dwin", sv["h"], du, "tn")
    dh = mm(f"{tag}_dh", du, p["w_in"], "nt", out_dtype=ACT)
    (dx,), (g["mix_norm"],) = rowwise_bwd(f"{tag}_drms", fn_rms_res, [_row(sv["x"])], [p["mix_norm"]],
                                          [_row(dy), _row(dh)], [_out(D_MODEL, f32)])
    return dx, g


def _interleave(name, a, b):
    S, W = a.shape
    T = min(512, S)

    def body(a_ref, b_ref, o_ref):
        o_ref[:, :HEAD_DIM] = a_ref[...]
        o_ref[:, HEAD_DIM:] = b_ref[...]

    blk = pl.BlockSpec((T, HEAD_DIM), lambda i, h: (i, h))
    return pl.pallas_call(
        body, name=name, grid=(S // T, W // HEAD_DIM), in_specs=[blk, blk],
        out_specs=pl.BlockSpec((T, 2 * HEAD_DIM), lambda i, h: (i, h)),
        out_shape=jax.ShapeDtypeStruct((S, 2 * W), a.dtype), compiler_params=_params(("parallel", "parallel")))(a, b)


def _u_layout(w):
    parts, at = [], 0
    for _, src, width, start in _U_SEGS:
        assert start == at
        parts.append(w[..., src:src + width])
        at += width
    parts.append(jnp.zeros(w.shape[:-1] + (U_WIDTH - at,), w.dtype))
    return jnp.concatenate(parts, axis=-1)


def _u_layout_inv(g):
    order = sorted(_U_SEGS, key=lambda s: s[1])
    return jnp.concatenate([g[..., start:start + width] for _, _, width, start in order], axis=-1)


def _pad_heads(w, n=MLA_QK, to=MLA_PAD):
    w = w.reshape(w.shape[:-1] + (HEADS, n))
    w = jnp.pad(w, [(0, 0)] * (w.ndim - 1) + [(0, to - n)])
    return w.reshape(w.shape[:-2] + (HEADS * to,))


def _unpad_heads(g, n=MLA_QK, to=MLA_PAD):
    g = g.reshape(g.shape[:-1] + (HEADS, to))[..., :n]
    return g.reshape(g.shape[:-2] + (HEADS * n,))


def layer_params(W, l):
    row = lambda name: W[name][l].reshape(1, -1).astype(f32)
    p = {n: row(n) for n in SMALL if n != "mla_q_hnorm" and n != "mla_k_hnorm"}
    for n in ("mla_q_hnorm", "mla_k_hnorm"):
        p[n] = jnp.pad(row(n), ((0, 0), (0, MLA_PAD - MLA_QK)))
    for n in ("ffn1_w_in", "ffn1_w_out", "ffn2_w_in", "ffn2_w_out", "mla_w_ukv", "mem_w_kv", "w_out"):
        p[n] = W[n][l]
    p["w_branch"] = [W["w_branch"][l, b] for b in range(N_BRANCH)]
    p["w_in"] = _u_layout(W["w_in"][l])
    p["mla_w_uq"] = _pad_heads(W["mla_w_uq"][l])
    p["conv_dw"] = jnp.pad(W["conv_dw"][l].astype(f32), ((0, 1), (0, 0)))
    return p


def layer_grads_to_original(g):
    out = dict(g)
    out["w_in"] = _u_layout_inv(g["w_in"])
    out["mla_w_uq"] = _unpad_heads(g["mla_w_uq"])
    out["conv_dw"] = g["conv_dw"][:CONV_WIDTH]
    out["w_branch"] = jnp.stack(g["w_branch"])
    for n in ("mla_q_hnorm", "mla_k_hnorm"):
        out[n] = g[n][:, :MLA_QK]
    return {n: (out[n].reshape(-1) if n in SMALL else out[n]) for n in out}


def local_step(x, mem, pos_col, target, W):
    tabs = rope_tables(pos_col)
    params = [layer_params(W, l) for l in range(DEPTH)]
    saved = []
    for l, p in enumerate(params):
        x, s1 = ffn_fwd(f"l{l}_ffn1", x, p["ffn1_norm"], p["ffn1_w_in"], p["ffn1_w_out"])
        x, s2 = mix_fwd(f"l{l}_mix", x, mem, tabs, p)
        x, s3 = ffn_fwd(f"l{l}_ffn2", x, p["ffn2_norm"], p["ffn2_w_in"], p["ffn2_w_out"])
        saved.append((s1, s2, s3))
    dx, loss_blk = loss_head(x, target)
    grads = [None] * DEPTH
    for l in reversed(range(DEPTH)):
        p, (s1, s2, s3) = params[l], saved[l]
        dx, g_n2, g_in2, g_out2 = ffn_bwd(f"l{l}_ffn2", s3, p["ffn2_norm"], p["ffn2_w_in"], p["ffn2_w_out"], dx)
        dx, g = mix_bwd(f"l{l}_mix", s2, mem, tabs, p, dx)
        dx, g_n1, g_in1, g_out1 = ffn_bwd(f"l{l}_ffn1", s1, p["ffn1_norm"], p["ffn1_w_in"], p["ffn1_w_out"], dx)
        g.update(ffn1_norm=g_n1, ffn1_w_in=g_in1, ffn1_w_out=g_out1, ffn2_norm=g_n2, ffn2_w_in=g_in2, ffn2_w_out=g_out2)
        grads[l] = layer_grads_to_original(g)
    return loss_blk, dx, grads


_ANY = pl.BlockSpec(memory_space=pl.ANY)
_COMM = pltpu.CompilerParams(has_side_effects=True)


def _coords():
    return lax.axis_index("x"), lax.axis_index("y"), lax.axis_index("c")


def chip_exchange(name, src, scatter):
    shape = src.shape[1:] if scatter else src.shape

    def body(src_ref, out_ref, send_sems, recv_sems):
        x, y, c = _coords()
        copies = []
        for k, (px, py) in enumerate([(1 - x, y), (x, 1 - y), (1 - x, 1 - y)]):
            piece = src_ref.at[2 * px + py] if scatter else src_ref
            cp = pltpu.make_async_remote_copy(piece, out_ref.at[k], send_sems.at[k], recv_sems.at[k],
                                              device_id=(px, py, c), device_id_type=MESH)
            cp.start()
            copies.append(cp)
        for cp in copies:
            cp.wait()

    return pl.pallas_call(
        body, name=name, in_specs=[_ANY], out_specs=_ANY, out_shape=jax.ShapeDtypeStruct((3,) + shape, src.dtype),
        scratch_shapes=[pltpu.SemaphoreType.DMA((3,)), pltpu.SemaphoreType.DMA((3,))], compiler_params=_COMM)(src)


def sibling_exchange(name, src):
    def body(src_ref, out_ref, send_sem, recv_sem):
        x, y, c = _coords()
        cp = pltpu.make_async_remote_copy(src_ref, out_ref, send_sem, recv_sem, device_id=(x, y, 1 - c),
                                          device_id_type=MESH)
        cp.start()
        cp.wait()

    return pl.pallas_call(
        body, name=name, in_specs=[_ANY], out_specs=_ANY, out_shape=jax.ShapeDtypeStruct(src.shape, src.dtype),
        scratch_shapes=[pltpu.SemaphoreType.DMA, pltpu.SemaphoreType.DMA], compiler_params=_COMM)(src)


def all8_gather(name, src):
    def body(src_ref, out_ref, send_sems, recv_sems, local_sem):
        x, y, c = _coords()
        me = 4 * x + 2 * y + c
        mine = pltpu.make_async_copy(src_ref, out_ref.at[me], local_sem)
        mine.start()
        copies = []
        for k in range(1, 8):
            peer = (1 - x if k & 4 else x, 1 - y if k & 2 else y, 1 - c if k & 1 else c)
            cp = pltpu.make_async_remote_copy(src_ref, out_ref.at[me], send_sems.at[k - 1], recv_sems.at[k - 1],
                                              device_id=peer, device_id_type=MESH)
            cp.start()
            copies.append(cp)
        for cp in copies:
            cp.wait()
        mine.wait()

    return pl.pallas_call(
        body, name=name, in_specs=[_ANY], out_specs=_ANY, out_shape=jax.ShapeDtypeStruct((8,) + src.shape, src.dtype),
        scratch_shapes=[pltpu.SemaphoreType.DMA((7,)), pltpu.SemaphoreType.DMA((7,)), pltpu.SemaphoreType.DMA],
        compiler_params=_COMM)(src)


def sum8(name, g):
    def body(g_ref, o_ref):
        acc = g_ref[0]
        for k in range(1, 8):
            acc = acc + g_ref[k]
        o_ref[...] = acc

    return pl.pallas_call(body, name=name, out_shape=jax.ShapeDtypeStruct(g.shape[1:], g.dtype))(g)


PACK_COLS = 1024
PACKED = tuple(n for n in SHARDED if n != "conv_dw")
PACK_ROW_GRAIN = PACK_GRAIN // PACK_COLS


def _rows(shape):
    n = math.prod(shape)
    assert n % (16 * PACK_COLS) == 0, shape
    return n // PACK_COLS


def _pack_rows(pieces):
    rows = sum(p.shape[0] for p in pieces)
    pad = -rows % PACK_ROW_GRAIN
    if pad:
        pieces = pieces + [jnp.zeros((pad, PACK_COLS), pieces[0].dtype)]
    return jnp.concatenate(pieces, axis=0)


def _unpack_rows(packed, shard_shapes):
    out, at = {}, 0
    for n in PACKED:
        r = _rows(shard_shapes[n])
        out[n] = packed[at:at + r].reshape(shard_shapes[n])
        at += r
    return out


def gather_shards(name, own):
    n_chunks = 4
    rows = own.shape[1] // n_chunks
    assert own.shape[1] % (16 * n_chunks) == 0

    def body(own_ref, out_ref, send_sems, recv_sems, local_sem):
        x, y, c = _coords()
        sib = (x, y, 1 - c)
        mine = pltpu.make_async_copy(own_ref, out_ref.at[0], local_sem)
        mine.start()
        chips = [(2, (1 - x, y)), (1, (x, 1 - y)), (3, (1 - x, 1 - y))]

        def copy(k, j, src, m, half, to):
            sl = pl.ds(j * rows, rows)
            return pltpu.make_async_remote_copy(src.at[sl], out_ref.at[m, half, sl], send_sems.at[k * n_chunks + j],
                                                recv_sems.at[k * n_chunks + j], device_id=to, device_id_type=MESH)

        first = [[copy(k, j, own_ref.at[c], m, c, (px, py, c)) for j in range(n_chunks)]
                 for k, (m, (px, py)) in enumerate(chips)]
        for j in range(n_chunks):
            for k in range(3):
                first[k][j].start()
        passed = []
        for j in range(n_chunks):
            for k, (m, _) in enumerate(chips):
                first[k][j].wait_recv()
                cp = copy(3 + k, j, out_ref.at[m, c], m, c, sib)
                cp.start()
                passed.append(cp)
        for j in range(n_chunks):
            for k, (m, _) in enumerate(chips):
                copy(3 + k, j, out_ref.at[m, 1 - c], m, 1 - c, sib).wait_recv()
        for cp in [cp for per_chip in first for cp in per_chip] + passed:
            cp.wait_send()
        mine.wait()

    return pl.pallas_call(
        body, name=name, in_specs=[_ANY], out_specs=_ANY, out_shape=jax.ShapeDtypeStruct((4,) + own.shape, own.dtype),
        scratch_shapes=[pltpu.SemaphoreType.DMA((6 * n_chunks,)), pltpu.SemaphoreType.DMA((6 * n_chunks,)),
                        pltpu.SemaphoreType.DMA],
        compiler_params=_COMM)(own)


def gather_weights(w):
    x, y, _ = _coords()
    me = 2 * x + y
    own = _pack_rows([w[n].astype(bf16).reshape(-1, PACK_COLS) for n in PACKED])
    by_mask = gather_shards("ag_shards", own.reshape(2, -1, PACK_COLS))
    shapes = {n: w[n].shape for n in PACKED}
    pieces = [_unpack_rows(lax.dynamic_index_in_dim(by_mask, jnp.bitwise_xor(s, me), axis=0, keepdims=False)
                           .reshape(-1, PACK_COLS), shapes) for s in range(4)]
    full = {n: jnp.concatenate([pieces[s][n] for s in range(4)], axis=SHARD_AXIS[n]) for n in PACKED}
    dw = w["conv_dw"]
    rows = math.prod(dw.shape[:-1])
    every = all8_gather("ag_conv_dw", jnp.pad(dw.reshape(rows, -1), ((0, -rows % 8), (0, 0))))
    full["conv_dw"] = jnp.concatenate([every[2 * s, :rows].reshape(dw.shape) for s in range(4)], axis=SHARD_AXIS["conv_dw"])
    return full


def _add_streams(name, ins, selectors, out_dtypes, rows, T=256):
    n_streams = max([a.shape[sel.index("s")] for a, sel in zip(ins, selectors) if "s" in sel] + [1])

    def spec(sel):
        def index(s, i, pf):
            lead = tuple(s if e == "s" else (pf[e[1]] if isinstance(e, tuple) else e) for e in sel)
            return lead + (i, 0)
        return pl.BlockSpec((None,) * len(sel) + (T, PACK_COLS), index)

    def body(pf_ref, *refs):
        acc = refs[0][...].astype(f32)
        for r in refs[1:len(ins)]:
            acc = acc + r[...].astype(f32)
        for o in refs[len(ins):]:
            o[...] = acc.astype(o.dtype)

    def run(pf):
        grid_spec = pltpu.PrefetchScalarGridSpec(
            num_scalar_prefetch=1, grid=(n_streams, rows // T), in_specs=[spec(sel) for sel in selectors],
            out_specs=[spec(("s",)) for _ in out_dtypes])
        return pl.pallas_call(
            body, name=name, grid_spec=grid_spec,
            out_shape=[jax.ShapeDtypeStruct((n_streams, rows, PACK_COLS), dt) for dt in out_dtypes],
            compiler_params=_params(("parallel", "parallel")))(pf, *ins)
    return run


def reduce_grads(grads, shard_shapes):
    x, y, c = _coords()
    me = 2 * x + y
    streams = []
    for s in range(4):
        pieces = []
        for n in PACKED:
            ax = SHARD_AXIS[n] - 1
            for g in grads:
                width = g[n].shape[ax] // 4
                pieces.append(lax.slice_in_dim(g[n], s * width, (s + 1) * width, axis=ax).reshape(-1, PACK_COLS))
        streams.append(_pack_rows(pieces))
    R = streams[0].shape[0] // 2
    G = jnp.concatenate(streams, axis=0).reshape(4, 2, R, PACK_COLS)
    pf = jnp.stack([c, me]).astype(jnp.int32)
    theirs = lax.dynamic_index_in_dim(G, 1 - c, axis=1, keepdims=False).astype(bf16)
    from_sib = sibling_exchange("rs_sibling", theirs)
    chip_sum, chip_sum_b = _add_streams("rs_add_sibling", [G, from_sib], [("s", ("pf", 0)), ("s",)], [f32, bf16], R)(pf)
    got = chip_exchange("rs_chips", chip_sum_b, scatter=True)
    (half,) = _add_streams("rs_add_chips", [chip_sum, got, got, got], [(("pf", 1),), (0,), (1,), (2,)], [f32], R)(pf)
    half = half[0]
    other = sibling_exchange("rs_final", half)
    shard = jnp.concatenate([jnp.where(c == 0, half, other), jnp.where(c == 0, other, half)], axis=0)
    return _unpack_rows(shard, shard_shapes)


def _small_pack(t, names=SMALL):
    flat = jnp.concatenate([t[n].reshape(-1) for n in names])
    total = -(-flat.shape[0] // SMALL_PAD) * SMALL_PAD
    return jnp.pad(flat, (0, total - flat.shape[0])).reshape(-1, 128)


def _small_unpack(a, shapes, names=SMALL):
    flat, out, at = a.reshape(-1), {}, 0
    for n in names:
        k = math.prod(shapes[n])
        out[n] = flat[at:at + k].reshape(shapes[n])
        at += k
    return out


def adamw(name, w, g, m, v):
    shape = w.shape
    two = lambda a: a.reshape(-1, shape[-1])
    rows = two(w).shape[0]
    T = 256 if rows % 256 == 0 else rows
    outs = rowwise(name, fn_adamw, [_row(two(w)), _row(two(g)), _row(two(m)), _row(two(v))], [],
                   [_out(shape[-1], f32)] * 3, T=T)
    return [o.reshape(shape) for o in outs]


def kernel(x, mem, positions, ffn1_norm, ffn1_w_in, ffn1_w_out, mix_norm, w_in, conv_dw, conv_b, conv_ln_g, conv_ln_b, sb_q_hnorm, sb_k_hnorm, mla_q_norm, mla_w_uq, mla_kv_norm, mla_w_ukv, mla_q_hnorm, mla_k_hnorm, mem_norm, mem_w_kv, mem_q_hnorm, mem_k_hnorm, w_branch, w_out, ffn2_norm, ffn2_w_in, ffn2_w_out, loss_target, m_ffn1_norm, m_ffn1_w_in, m_ffn1_w_out, m_mix_norm, m_w_in, m_conv_dw, m_conv_b, m_conv_ln_g, m_conv_ln_b, m_sb_q_hnorm, m_sb_k_hnorm, m_mla_q_norm, m_mla_w_uq, m_mla_kv_norm, m_mla_w_ukv, m_mla_q_hnorm, m_mla_k_hnorm, m_mem_norm, m_mem_w_kv, m_mem_q_hnorm, m_mem_k_hnorm, m_w_branch, m_w_out, m_ffn2_norm, m_ffn2_w_in, m_ffn2_w_out, v_ffn1_norm, v_ffn1_w_in, v_ffn1_w_out, v_mix_norm, v_w_in, v_conv_dw, v_conv_b, v_conv_ln_g, v_conv_ln_b, v_sb_q_hnorm, v_sb_k_hnorm, v_mla_q_norm, v_mla_w_uq, v_mla_kv_norm, v_mla_w_ukv, v_mla_q_hnorm, v_mla_k_hnorm, v_mem_norm, v_mem_w_kv, v_mem_q_hnorm, v_mem_k_hnorm, v_w_branch, v_w_out, v_ffn2_norm, v_ffn2_w_in, v_ffn2_w_out):
    w = dict(zip(WEIGHTS, (ffn1_norm, ffn1_w_in, ffn1_w_out, mix_norm, w_in, conv_dw, conv_b, conv_ln_g, conv_ln_b, sb_q_hnorm, sb_k_hnorm, mla_q_norm, mla_w_uq, mla_kv_norm, mla_w_ukv, mla_q_hnorm, mla_k_hnorm, mem_norm, mem_w_kv, mem_q_hnorm, mem_k_hnorm, w_branch, w_out, ffn2_norm, ffn2_w_in, ffn2_w_out)))
    m = dict(zip(WEIGHTS, (m_ffn1_norm, m_ffn1_w_in, m_ffn1_w_out, m_mix_norm, m_w_in, m_conv_dw, m_conv_b, m_conv_ln_g, m_conv_ln_b, m_sb_q_hnorm, m_sb_k_hnorm, m_mla_q_norm, m_mla_w_uq, m_mla_kv_norm, m_mla_w_ukv, m_mla_q_hnorm, m_mla_k_hnorm, m_mem_norm, m_mem_w_kv, m_mem_q_hnorm, m_mem_k_hnorm, m_w_branch, m_w_out, m_ffn2_norm, m_ffn2_w_in, m_ffn2_w_out)))
    v = dict(zip(WEIGHTS, (v_ffn1_norm, v_ffn1_w_in, v_ffn1_w_out, v_mix_norm, v_w_in, v_conv_dw, v_conv_b, v_conv_ln_g, v_conv_ln_b, v_sb_q_hnorm, v_sb_k_hnorm, v_mla_q_norm, v_mla_w_uq, v_mla_kv_norm, v_mla_w_ukv, v_mla_q_hnorm, v_mla_k_hnorm, v_mem_norm, v_mem_w_kv, v_mem_q_hnorm, v_mem_k_hnorm, v_w_branch, v_w_out, v_ffn2_norm, v_ffn2_w_in, v_ffn2_w_out)))
    S = x.shape[1]
    full = gather_weights(w)
    full.update({n: w[n] for n in SMALL})
    loss_blk, dx, g = local_step(x[0], mem[0], positions.reshape(S, 1), loss_target[0], full)
    loss = lax.psum(loss_blk[0, 0], ("x", "y", "c"))
    grads = reduce_grads(g, {n: w[n].shape for n in PACKED})
    small_shapes = {n: w[n].shape for n in SMALL}
    reduced = SMALL + ("conv_dw",)
    reduced_shapes = dict(small_shapes, conv_dw=full["conv_dw"].shape)
    g_all = {n: jnp.stack([gl[n] for gl in g]) for n in reduced}
    g_all = sum8("small_sum", all8_gather("small_gather", _small_pack(g_all, reduced)))
    g_all = _small_unpack(g_all, reduced_shapes, reduced)
    width = w["conv_dw"].shape[-1]
    x_pos, y_pos, _ = _coords()
    grads["conv_dw"] = lax.dynamic_slice_in_dim(g_all.pop("conv_dw"), (2 * x_pos + y_pos) * width, width, axis=2)
    grads.update(g_all)
    g_small = _small_pack(grads)
    delta, new_m, new_v = {}, {}, {}
    for n in SHARDED:
        delta[n], new_m[n], new_v[n] = adamw(f"adamw_{n}", w[n], grads[n], m[n], v[n])
    d_s, m_s, v_s = adamw("adamw_small", _small_pack(w), g_small, _small_pack(m), _small_pack(v))
    for t, packed in ((delta, d_s), (new_m, m_s), (new_v, v_s)):
        t.update(_small_unpack(packed, small_shapes))
    return (loss, dx.reshape(x.shape), *[grads[n] for n in WEIGHTS], *[delta[n] for n in WEIGHTS],
            *[new_m[n] for n in WEIGHTS], *[new_v[n] for n in WEIGHTS])
```

```python
import math

import jax
import jax.numpy as jnp
from jax import lax
from jax.experimental import pallas as pl
from jax.experimental.pallas import tpu as pltpu

f32, bf16 = jnp.float32, jnp.bfloat16

D_MODEL = 1024
DEPTH = 4
CHUNK = 64
FFN_HIDDEN = 2048
CONV_CH = 512
CONV_WIDTH = 31
HEADS = 4
HEAD_DIM = 128
MLA_NOPE = 128
MLA_ROPE = 64
MLA_QK = MLA_NOPE + MLA_ROPE
MLA_PAD = 256
MLA_Q_LORA = 256
MLA_KV_LORA = 256
N_BRANCH = 4
BRANCH_WIDTH = 512
ROPE_BASE = 10000.0
EPS = 1e-6
NEG_INF = -1e30
IN_WIDTH = 7744
U_WIDTH = 8192
_U_SEGS = (("gates", 3648, 4096, 0), ("conv", 0, 1024, 4096), ("sb", 1024, 1536, 5120), ("qlat", 2560, 256, 6656),
           ("kvlat", 2816, 256, 6912), ("memq", 3136, 512, 7168), ("krope", 3072, 64, 7680))
U_PAD_FROM = 7744

ADAM_LR, ADAM_B1, ADAM_B2, ADAM_EPS, ADAM_WD, ADAM_STEP = 0.001, 0.9, 0.999, 1e-08, 0.01, 10

VMEM_LIMIT = 48 * 1024 * 1024
MESH = pl.DeviceIdType.MESH

SHARDED = ("ffn1_w_in", "ffn1_w_out", "w_in", "conv_dw", "mla_w_uq", "mla_w_ukv", "mem_w_kv", "w_branch", "w_out",
           "ffn2_w_in", "ffn2_w_out")
SHARD_AXIS = {"ffn1_w_in": 2, "ffn1_w_out": 1, "w_in": 2, "conv_dw": 2, "mla_w_uq": 2, "mla_w_ukv": 2, "mem_w_kv": 1,
              "w_branch": 3, "w_out": 1, "ffn2_w_in": 2, "ffn2_w_out": 1}
SMALL = ("ffn1_norm", "mix_norm", "conv_b", "conv_ln_g", "conv_ln_b", "sb_q_hnorm", "sb_k_hnorm", "mla_q_norm",
         "mla_kv_norm", "mla_q_hnorm", "mla_k_hnorm", "mem_norm", "mem_q_hnorm", "mem_k_hnorm", "ffn2_norm")
WEIGHTS = ("ffn1_norm", "ffn1_w_in", "ffn1_w_out", "mix_norm", "w_in", "conv_dw", "conv_b", "conv_ln_g", "conv_ln_b",
           "sb_q_hnorm", "sb_k_hnorm", "mla_q_norm", "mla_w_uq", "mla_kv_norm", "mla_w_ukv", "mla_q_hnorm",
           "mla_k_hnorm", "mem_norm", "mem_w_kv", "mem_q_hnorm", "mem_k_hnorm", "w_branch", "w_out", "ffn2_norm",
           "ffn2_w_in", "ffn2_w_out")
PACK_GRAIN = 2 * 256 * 1024
SMALL_PAD = 8 * 128


def _params(sem, vmem=VMEM_LIMIT):
    return pltpu.CompilerParams(dimension_semantics=sem, vmem_limit_bytes=vmem)


def _pick(n, pref):
    for t in pref:
        if n % t == 0:
            return t
    return n


def mm(name, a, b, form, *, out_dtype=f32, alpha=1.0, res=None, tm=None, tn=None, tk=None):
    if form == "nn":
        (M, K), (K2, N) = a.shape, b.shape
    elif form == "nt":
        (M, K), (N, K2) = a.shape, b.shape
    else:
        (K, M), (K2, N) = a.shape, b.shape
    assert K == K2, (name, a.shape, b.shape)
    tm = tm or _pick(M, (1024, 512, 256, 128))
    tn = tn or _pick(N, (1024, 512, 256, 128))
    tk = tk or _pick(K, (2048, 1024, 512, 256))
    nk = K // tk
    if form == "nn":
        a_spec = pl.BlockSpec((tm, tk), lambda i, j, k: (i, k))
        b_spec = pl.BlockSpec((tk, tn), lambda i, j, k: (k, j))
        dims = (((1,), (0,)), ((), ()))
    elif form == "nt":
        a_spec = pl.BlockSpec((tm, tk), lambda i, j, k: (i, k))
        b_spec = pl.BlockSpec((tn, tk), lambda i, j, k: (j, k))
        dims = (((1,), (1,)), ((), ()))
    else:
        a_spec = pl.BlockSpec((tk, tm), lambda i, j, k: (k, i))
        b_spec = pl.BlockSpec((tk, tn), lambda i, j, k: (k, j))
        dims = (((0,), (0,)), ((), ()))
    o_spec = pl.BlockSpec((tm, tn), lambda i, j, k: (i, j))
    has_res = res is not None

    def body(a_ref, b_ref, *rest):
        if has_res:
            r_ref, o_ref, acc_ref = rest
        else:
            o_ref, acc_ref = rest
        k = pl.program_id(2)
        part = lax.dot_general(a_ref[...].astype(bf16), b_ref[...].astype(bf16), dims, preferred_element_type=f32)

        def finish(acc):
            r = acc if alpha == 1.0 else acc * alpha
            if has_res:
                r = r_ref[...].astype(f32) + r
            o_ref[...] = r.astype(o_ref.dtype)

        if nk == 1:
            finish(part)
        else:
            @pl.when(k == 0)
            def _():
                acc_ref[...] = part

            @pl.when(k > 0)
            def _():
                acc_ref[...] += part

            @pl.when(k == nk - 1)
            def _():
                finish(acc_ref[...])

    ins = [a, b] + ([res] if has_res else [])
    in_specs = [a_spec, b_spec] + ([o_spec] if has_res else [])
    return pl.pallas_call(
        body, name=name, grid=(M // tm, N // tn, nk), in_specs=in_specs, out_specs=o_spec,
        out_shape=jax.ShapeDtypeStruct((M, N), out_dtype), scratch_shapes=[pltpu.VMEM((tm, tn), f32)],
        compiler_params=_params(("parallel", "parallel", "arbitrary")))(*ins)


class _Span:
    def __init__(self, w, off, st, H):
        self.w = w
        if st == 0 or H == 1:
            self.width, self.index, self.base, self.step = w, off, 0, 0
        else:
            self.width, self.index = st * H * w, off // (st * H)
            self.base, self.step = (off % (st * H)) * w, st * w

    def spec(self, T):
        return pl.BlockSpec((T, self.width), lambda i: (i, self.index))

    def lanes(self, h):
        return slice(self.base + h * self.step, self.base + h * self.step + self.w)


def _full_spec(p):
    return pl.BlockSpec(p.shape, lambda i: (0,) * p.ndim)


def rowwise(name, fn, rows, params, outs, *, T=512, H=1):
    S = rows[0][0].shape[0]
    T = min(T, S)
    n_r, n_p = len(rows), len(params)
    in_spans = [_Span(w, off, st, H) for (_, w, off, st) in rows]
    out_spans = [_Span(w, off, st, H) for (_, _, w, off, st) in outs]

    def body(*refs):
        p_vals = [r[...] for r in refs[n_r:n_r + n_p]]
        for h in range(H):
            vals = [r[:, sp.lanes(h)] for r, sp in zip(refs[:n_r], in_spans)]
            res = fn(*vals, *p_vals)
            for o_ref, sp, r in zip(refs[n_r + n_p:], out_spans, res):
                o_ref[:, sp.lanes(h)] = r.astype(o_ref.dtype)

    res = pl.pallas_call(
        body, name=name, grid=(S // T,),
        in_specs=[sp.spec(T) for sp in in_spans] + [_full_spec(p) for p in params],
        out_specs=[sp.spec(T) for sp in out_spans],
        out_shape=[jax.ShapeDtypeStruct((S, tw), dt) for (tw, dt, _, _, _) in outs],
        compiler_params=_params(("parallel",)))(*[r[0] for r in rows], *params)
    return res


def rowwise_bwd(name, fn, rows, params, douts, drows, *, T=512, H=1, nondiff=(), merge=None):
    S = rows[0][0].shape[0]
    T = min(T, S)
    n_r, n_p, n_d = len(rows), len(params), len(douts)
    diff_idx = [k for k in range(n_r) if k not in nondiff]
    merge = merge or [(j,) for j in range(len(diff_idx))]
    shared = [H > 1 and rows[diff_idx[pos[0]]][3] == 0 for pos in merge]
    n_o = len(merge)
    assert n_o == len(drows)
    in_spans = [_Span(w, off, st, H) for (_, w, off, st) in rows]
    d_spans = [_Span(w, off, st, H) for (_, w, off, st) in douts]
    out_spans = [_Span(w, off, st, H) for (_, _, w, off, st) in drows]

    def body(*refs):
        i = pl.program_id(0)
        p_vals = [r[...].astype(f32) for r in refs[n_r:n_r + n_p]]
        out_refs = refs[n_r + n_p + n_d:]
        shared_sum = [None] * n_o
        p_sum = [None] * n_p
        for h in range(H):
            row_vals = [r[:, sp.lanes(h)].astype(f32) for r, sp in zip(refs[:n_r], in_spans)]
            d_vals = [r[:, sp.lanes(h)].astype(f32) for r, sp in zip(refs[n_r + n_p:n_r + n_p + n_d], d_spans)]

            def f(*args):
                full = list(row_vals)
                for k, v in zip(diff_idx, args[:len(diff_idx)]):
                    full[k] = v
                return tuple(fn(*full, *args[len(diff_idx):]))

            _, vjp = jax.vjp(f, *[row_vals[k] for k in diff_idx], *p_vals)
            cts = vjp(tuple(d_vals))
            row_cts = [cts[pos[0]] if len(pos) == 1 else jnp.concatenate([cts[j] for j in pos], axis=-1) for pos in merge]
            for j, (o_ref, sp, ct) in enumerate(zip(out_refs[:n_o], out_spans, row_cts)):
                if shared[j]:
                    shared_sum[j] = ct if h == 0 else shared_sum[j] + ct
                else:
                    o_ref[:, sp.lanes(h)] = ct.astype(o_ref.dtype)
            for j, ct in enumerate(cts[len(diff_idx):]):
                p_sum[j] = ct if h == 0 else p_sum[j] + ct
        for j, o_ref in enumerate(out_refs[:n_o]):
            if shared[j]:
                o_ref[...] = shared_sum[j].astype(o_ref.dtype)
        for o_ref, ct in zip(out_refs[n_o:], p_sum):
            @pl.when(i == 0)
            def _():
                o_ref[...] = ct

            @pl.when(i > 0)
            def _():
                o_ref[...] += ct

    res = pl.pallas_call(
        body, name=name, grid=(S // T,),
        in_specs=[sp.spec(T) for sp in in_spans] + [_full_spec(p) for p in params] + [sp.spec(T) for sp in d_spans],
        out_specs=[sp.spec(T) for sp in out_spans] + [_full_spec(p) for p in params],
        out_shape=[jax.ShapeDtypeStruct((S, tw), dt) for (tw, dt, _, _, _) in drows]
        + [jax.ShapeDtypeStruct(p.shape, f32) for p in params],
        compiler_params=_params(("arbitrary",)))(*[r[0] for r in rows], *params, *[d[0] for d in douts])
    return res[:n_o], res[n_o:]


def _rms(x, g, n=None):
    x = x.astype(f32)
    n = n or x.shape[-1]
    return x * lax.rsqrt(jnp.sum(x * x, axis=-1, keepdims=True) * (1.0 / n) + EPS) * g.astype(f32)


def _sigmoid(x):
    return 1.0 / (1.0 + jnp.exp(-x))


def _silu(x):
    return x * _sigmoid(x)


def fn_rms(x, g):
    return (_rms(x, g),)


def fn_rms_res(x, g):
    return (x.astype(f32), _rms(x, g))


def fn_swiglu(gate, up):
    return (_silu(gate.astype(f32)) * up.astype(f32),)


def fn_sb_prep(q, k, gq, gk):
    return (_rms(q, gq), _rms(k, gk))


def fn_ln_silu(y, b, g, beta):
    y = y.astype(f32) + b
    mu = jnp.mean(y, axis=-1, keepdims=True)
    var = jnp.mean(jnp.square(y - mu), axis=-1, keepdims=True)
    return (_silu((y - mu) * lax.rsqrt(var + EPS) * g + beta),)


def fn_merge(g0, g1, g2, g3, p0, p1, p2, p3):
    out = _sigmoid(g0.astype(f32)) * p0.astype(f32)
    for g, p in ((g1, p1), (g2, p2), (g3, p3)):
        out = out + _sigmoid(g.astype(f32)) * p.astype(f32)
    return (out,)


def _rot_fwd(x):
    z = jnp.zeros_like(x[:, :MLA_NOPE])
    h = MLA_ROPE // 2
    return jnp.concatenate([z, -x[:, MLA_NOPE + h:MLA_QK], x[:, MLA_NOPE:MLA_NOPE + h], z[:, :MLA_PAD - MLA_QK]], axis=-1)


def _rot_bwd(g):
    z = jnp.zeros_like(g[:, :MLA_NOPE])
    h = MLA_ROPE // 2
    return jnp.concatenate([z, g[:, MLA_NOPE + h:MLA_QK], -g[:, MLA_NOPE:MLA_NOPE + h], z[:, :MLA_PAD - MLA_QK]], axis=-1)


@jax.custom_vjp
def _rope(x, c, s):
    return x * c + _rot_fwd(x) * s


def _rope_f(x, c, s):
    return _rope(x, c, s), (c, s)


def _rope_b(res, g):
    c, s = res
    return g * c + _rot_bwd(g * s), jnp.zeros_like(c), jnp.zeros_like(s)


_rope.defvjp(_rope_f, _rope_b)


def fn_mla_q(q, c, s, gain):
    return (_rope(_rms(q, gain, MLA_QK), c, s),)


def fn_mla_k(kn, kr, c, s, gain):
    k = jnp.concatenate([kn.astype(f32), kr.astype(f32)], axis=-1)
    return (_rope(_rms(k, gain, MLA_QK), c, s),)


def fn_mla_k_v(kn, kr, v, c, s, gain):
    return (fn_mla_k(kn, kr, c, s, gain)[0], v.astype(f32))


def fn_mem_k_v(k, v, gain):
    return (_rms(k, gain), v.astype(f32))


def fn_adamw(w, g, m, v):
    m = ADAM_B1 * m + (1.0 - ADAM_B1) * g
    v = ADAM_B2 * v + (1.0 - ADAM_B2) * jnp.square(g)
    m_hat = m / (1.0 - ADAM_B1 ** ADAM_STEP)
    v_hat = v / (1.0 - ADAM_B2 ** ADAM_STEP)
    delta = -ADAM_LR * (m_hat / (jnp.sqrt(v_hat) + ADAM_EPS) + ADAM_WD * w)
    return delta, m, v


def _head_spec(rows, w, off, st):
    return pl.BlockSpec((rows, w), lambda h, i: (0, off + st * h))


def _qblk_spec(B, w, off, st):
    return pl.BlockSpec((B, w), lambda h, i: (i, off + st * h))


def _chunk_mask(tq, tk, d):
    r = lax.broadcasted_iota(jnp.int32, (tq, tk), 0) // CHUNK
    c = (d * tk + lax.broadcasted_iota(jnp.int32, (tq, tk), 1)) // CHUNK
    return c <= r


def _strict_mask(tq, tk, d):
    r = lax.broadcasted_iota(jnp.int32, (tq, tk), 0)
    c = d * tk + lax.broadcasted_iota(jnp.int32, (tq, tk), 1)
    return c < r


_NT = (((1,), (1,)), ((), ()))
_TN = (((0,), (0,)), ((), ()))


def _tiles(Sq, Sk, mask, tq, tk):
    tq = min(tq, Sq)
    if mask is None:
        return tq, Sk, 0
    tk = min(tk, tq)
    assert Sq == Sk and tq % tk == 0 and tk % CHUNK == 0
    return tq, tk, tq // tk


def _gather_in_steps(own_ref, out_ref, send_sems, recv_sems, local_sem, h, i, last_h, last_i):
    x, y, c = lax.axis_index("x"), lax.axis_index("y"), lax.axis_index("c")
    sib = (x, y, 1 - c)
    chips = [(2, (1 - x, y)), (1, (x, 1 - y)), (3, (1 - x, 1 - y))]

    def copy(k, src, m, half, to):
        return pltpu.make_async_remote_copy(src, out_ref.at[m, half], send_sems.at[k], recv_sems.at[k],
                                            device_id=to, device_id_type=MESH)

    mine = pltpu.make_async_copy(own_ref, out_ref.at[0], local_sem)
    first = [copy(k, own_ref.at[c], m, c, (px, py, c)) for k, (m, (px, py)) in enumerate(chips)]
    passed = [copy(3 + k, out_ref.at[m, c], m, c, sib) for k, (m, _) in enumerate(chips)]
    landed = [copy(3 + k, out_ref.at[m, 1 - c], m, 1 - c, sib) for k, (m, _) in enumerate(chips)]

    @pl.when(jnp.logical_and(h == 0, i == 0))
    def _():
        mine.start()
        for cp in first:
            cp.start()

    @pl.when(jnp.logical_and(h == last_h, i == 0))
    def _():
        for k in range(3):
            first[k].wait_recv()
            passed[k].start()

    @pl.when(jnp.logical_and(h == last_h, i == last_i))
    def _():
        for cp in landed:
            cp.wait_recv()
        for cp in first + passed:
            cp.wait_send()
        mine.wait()


def attn_fwd(name, q, k, v, *, scale, mask, tq=1024, tk=512, ride=None):
    Sq, Sk = q[0].shape[0], k[0].shape[0]
    tq, tk, nd = _tiles(Sq, Sk, mask, tq, tk)
    n_q = Sq // tq

    def body(q_ref, k_ref, v_ref, *rest):
        if ride is None:
            o_ref, lse_ref = rest
        else:
            own_ref, o_ref, lse_ref, land_ref, send_sems, recv_sems, local_sem = rest
            _gather_in_steps(own_ref, land_ref, send_sems, recv_sems, local_sem, pl.program_id(0), pl.program_id(1),
                             HEADS - 1, n_q - 1)
        i = pl.program_id(1)
        qv = q_ref[...]

        def block(off, carry, d):
            m, l, acc = carry
            kb, vb = k_ref[pl.ds(off, tk), :].astype(bf16), v_ref[pl.ds(off, tk), :].astype(bf16)
            s = lax.dot_general(qv, kb, _NT, preferred_element_type=f32) * scale
            if d is not None:
                s = jnp.where(_chunk_mask(tq, tk, d), s, NEG_INF)
            m_new = jnp.maximum(m, jnp.max(s, axis=-1, keepdims=True))
            p = jnp.exp(s - m_new)
            corr = jnp.exp(m - m_new)
            l = l * corr + jnp.sum(p, axis=-1, keepdims=True)
            acc = acc * corr + jnp.dot(p.astype(bf16), vb, preferred_element_type=f32)
            return m_new, l, acc

        carry = (jnp.full((tq, 1), NEG_INF, f32), jnp.zeros((tq, 1), f32), jnp.zeros((tq, HEAD_DIM), f32))
        if nd:
            carry = lax.fori_loop(0, i * nd, lambda j, c: block(pl.multiple_of(j * tk, tk), c, None), carry)
            for d in range(nd):
                carry = block(pl.multiple_of(i * tq + d * tk, tk), carry, d)
        else:
            carry = block(0, carry, None)
        m, l, acc = carry
        o_ref[...] = (acc / l).astype(o_ref.dtype)
        lse_ref[...] = jnp.broadcast_to(m + jnp.log(l), (tq, HEAD_DIM))

    in_specs = [_qblk_spec(tq, *q[1:]), _head_spec(Sk, *k[1:]), _head_spec(Sk, *v[1:])]
    out_specs = [_qblk_spec(tq, HEAD_DIM, 0, 1), _qblk_spec(tq, HEAD_DIM, 0, 1)]
    out_shape = [jax.ShapeDtypeStruct((Sq, HEADS * HEAD_DIM), f32), jax.ShapeDtypeStruct((Sq, HEADS * HEAD_DIM), f32)]
    if ride is None:
        return pl.pallas_call(body, name=name, grid=(HEADS, n_q), in_specs=in_specs, out_specs=out_specs,
                              out_shape=out_shape, compiler_params=_params(("parallel", "arbitrary")))(q[0], k[0], v[0])
    any_space = pl.BlockSpec(memory_space=pl.ANY)
    return pl.pallas_call(
        body, name=name, grid=(HEADS, n_q), in_specs=in_specs + [any_space], out_specs=out_specs + [any_space],
        out_shape=out_shape + [jax.ShapeDtypeStruct((4,) + ride.shape, ride.dtype)],
        scratch_shapes=[pltpu.SemaphoreType.DMA((6,)), pltpu.SemaphoreType.DMA((6,)), pltpu.SemaphoreType.DMA],
        compiler_params=pltpu.CompilerParams(dimension_semantics=("arbitrary", "arbitrary"), vmem_limit_bytes=VMEM_LIMIT,
                                             has_side_effects=True))(q[0], k[0], v[0], ride)


def attn_bwd(name, q, k, v, o, do, lse, *, scale, mask, tq=1024, tk=512):
    Sq, Sk = q[0].shape[0], k[0].shape[0]
    tq, tk, nd = _tiles(Sq, Sk, mask, tq, tk)
    dq_w = q[1]

    def body(q_ref, k_ref, v_ref, o_ref, do_ref, lse_ref, dq_ref, dk_ref, dv_ref):
        i = pl.program_id(1)

        @pl.when(i == 0)
        def _():
            dk_ref[...] = jnp.zeros_like(dk_ref)
            dv_ref[...] = jnp.zeros_like(dv_ref)

        qv, dov = q_ref[...], do_ref[...].astype(bf16)
        delta = jnp.sum(do_ref[...].astype(f32) * o_ref[...].astype(f32), axis=-1, keepdims=True)
        lse_v = lse_ref[:, :1]

        def block(off, dq_acc, d):
            kb, vb = k_ref[pl.ds(off, tk), :].astype(bf16), v_ref[pl.ds(off, tk), :].astype(bf16)
            s = lax.dot_general(qv, kb, _NT, preferred_element_type=f32) * scale
            if d is not None:
                s = jnp.where(_chunk_mask(tq, tk, d), s, NEG_INF)
            p = jnp.exp(s - lse_v)
            dv_ref[pl.ds(off, tk), :] += lax.dot_general(p.astype(bf16), dov, _TN, preferred_element_type=f32)
            dp = lax.dot_general(dov, vb, _NT, preferred_element_type=f32)
            ds = (p * (dp - delta) * scale).astype(bf16)
            dk_ref[pl.ds(off, tk), :] += lax.dot_general(ds, qv, _TN, preferred_element_type=f32)
            return dq_acc + jnp.dot(ds, kb, preferred_element_type=f32)

        acc = jnp.zeros((tq, dq_w), f32)
        if nd:
            acc = lax.fori_loop(0, i * nd, lambda j, c: block(pl.multiple_of(j * tk, tk), c, None), acc)
            for d in range(nd):
                acc = block(pl.multiple_of(i * tq + d * tk, tk), acc, d)
        else:
            acc = block(0, acc, None)
        dq_ref[...] = acc.astype(dq_ref.dtype)

    hd = _qblk_spec(tq, HEAD_DIM, 0, 1)
    return pl.pallas_call(
        body, name=name, grid=(HEADS, Sq // tq),
        in_specs=[_qblk_spec(tq, *q[1:]), _head_spec(Sk, *k[1:]), _head_spec(Sk, *v[1:]), hd, hd, hd],
        out_specs=[_qblk_spec(tq, dq_w, 0, 1), _head_spec(Sk, dq_w, 0, 1), _head_spec(Sk, HEAD_DIM, 0, 1)],
        out_shape=[jax.ShapeDtypeStruct((Sq, HEADS * dq_w), f32), jax.ShapeDtypeStruct((Sk, HEADS * dq_w), f32),
                   jax.ShapeDtypeStruct((Sk, HEADS * HEAD_DIM), f32)],
        compiler_params=_params(("arbitrary", "arbitrary"), 56 * 1024 * 1024))(q[0], k[0], v[0], o, do, lse)


SB_LOG_ZERO = -104.0


def _tri(B, rel):
    r = lax.broadcasted_iota(jnp.int32, (B, B), 0)
    c = lax.broadcasted_iota(jnp.int32, (B, B), 1)
    return rel(r, c).astype(bf16)


def _sb_scores(qv, kb, scale):
    z = lax.dot_general(qv, kb, _NT, preferred_element_type=f32) * scale
    e = jnp.exp(-jnp.abs(z))
    log_keep = -(jnp.maximum(z, 0.0) + jnp.log(1.0 + e))
    return z, e, log_keep


def _split_dot(x, m):
    hi = x.astype(bf16)
    lo = (x - hi.astype(f32)).astype(bf16)
    return jnp.dot(hi, m, preferred_element_type=f32) + jnp.dot(lo, m, preferred_element_type=f32)


def sb_fwd(name, q, k, v, *, tq=512, tk=256):
    S = q[0].shape[0]
    tq, tk, nd = _tiles(S, S, "strict", tq, tk)
    scale = HEAD_DIM ** -0.5
    m_ex = _tri(tk, lambda j, s: j > s)

    def body(q_ref, k_ref, v_ref, mex_ref, o_ref, tot_ref, cnt_ref):
        i = pl.program_id(1)
        qv, mex = q_ref[...], mex_ref[...]

        def block(off, carry, d):
            later, acc = carry
            kb, vb = k_ref[pl.ds(off, tk), :].astype(bf16), v_ref[pl.ds(off, tk), :].astype(bf16)
            z, _, lk = _sb_scores(qv, kb, scale)
            if d is not None:
                lk = jnp.where(_strict_mask(tq, tk, d), lk, 0.0)
            a = jnp.exp(z + lk + _split_dot(lk, mex) + later)
            if d is not None:
                a = jnp.where(_strict_mask(tq, tk, d), a, 0.0)
            acc = acc + jnp.dot(a.astype(bf16), vb, preferred_element_type=f32)
            return later + jnp.sum(lk, axis=-1, keepdims=True), acc

        carry = (jnp.zeros((tq, 1), f32), jnp.zeros((tq, HEAD_DIM), f32))
        for d in reversed(range(nd)):
            carry = block(pl.multiple_of(i * tq + d * tk, tk), carry, d)
        n_full = i * nd

        def more(state):
            t, later, _ = state
            return jnp.logical_and(t < n_full, jnp.max(later) > SB_LOG_ZERO)

        def step(state):
            t, later, acc = state
            later, acc = block(pl.multiple_of((n_full - 1 - t) * tk, tk), (later, acc), None)
            return t + 1, later, acc

        done, total, acc = lax.while_loop(more, step, (jnp.int32(0),) + carry)
        o_ref[...] = acc.astype(o_ref.dtype)
        tot_ref[...] = jnp.broadcast_to(total, (tq, HEAD_DIM))
        cnt_ref[...] = jnp.full((8, HEAD_DIM), done, f32)

    hd = _qblk_spec(tq, HEAD_DIM, 0, 1)
    return pl.pallas_call(
        body, name=name, grid=(HEADS, S // tq),
        in_specs=[_qblk_spec(tq, *q[1:]), _head_spec(S, *k[1:]), _head_spec(S, *v[1:]),
                  pl.BlockSpec((tk, tk), lambda h, i: (0, 0))],
        out_specs=[hd, hd, _qblk_spec(8, HEAD_DIM, 0, 1)],
        out_shape=[jax.ShapeDtypeStruct((S, HEADS * HEAD_DIM), f32), jax.ShapeDtypeStruct((S, HEADS * HEAD_DIM), f32),
                   jax.ShapeDtypeStruct((8 * (S // tq), HEADS * HEAD_DIM), f32)],
        compiler_params=_params(("parallel", "arbitrary")))(q[0], k[0], v[0], m_ex)


def sb_bwd(name, q, k, v, tot, cnt, do, *, tq=512, tk=256):
    S = q[0].shape[0]
    tq, tk, nd = _tiles(S, S, "strict", tq, tk)
    scale = HEAD_DIM ** -0.5
    m_le, m_lt = _tri(tk, lambda j, s: j <= s), _tri(tk, lambda j, s: j < s)

    def body(q_ref, k_ref, v_ref, tot_ref, cnt_ref, do_ref, mle_ref, mlt_ref, dq_ref, dk_ref, dv_ref):
        i = pl.program_id(1)

        @pl.when(i == 0)
        def _():
            dk_ref[...] = jnp.zeros_like(dk_ref)
            dv_ref[...] = jnp.zeros_like(dv_ref)

        qv, dov, mle, mlt = q_ref[...], do_ref[...].astype(bf16), mle_ref[...], mlt_ref[...]
        total = tot_ref[:, :1]

        def block(off, carry, d):
            before, g_before, dq_acc = carry
            kb, vb = k_ref[pl.ds(off, tk), :].astype(bf16), v_ref[pl.ds(off, tk), :].astype(bf16)
            z, e, lk = _sb_scores(qv, kb, scale)
            sig = jnp.where(z >= 0, 1.0, e) / (1.0 + e)
            if d is not None:
                lk = jnp.where(_strict_mask(tq, tk, d), lk, 0.0)
            later = (total - before) - _split_dot(lk, mle)
            a = jnp.exp(z + lk + later)
            if d is not None:
                a = jnp.where(_strict_mask(tq, tk, d), a, 0.0)
            g = a * lax.dot_general(dov, vb, _NT, preferred_element_type=f32)
            prefix = g_before + jnp.dot(g.astype(bf16), mlt, preferred_element_type=f32)
            dz = (g * (1.0 - sig) - prefix * sig) * scale
            if d is not None:
                dz = jnp.where(_strict_mask(tq, tk, d), dz, 0.0)
            dzb = dz.astype(bf16)
            dk_ref[pl.ds(off, tk), :] += lax.dot_general(dzb, qv, _TN, preferred_element_type=f32)
            dv_ref[pl.ds(off, tk), :] += lax.dot_general(a.astype(bf16), dov, _TN, preferred_element_type=f32)
            return (before + jnp.sum(lk, axis=-1, keepdims=True), g_before + jnp.sum(g, axis=-1, keepdims=True),
                    dq_acc + jnp.dot(dzb, kb, preferred_element_type=f32))

        zero = jnp.zeros((tq, 1), f32)
        first = i * nd - jnp.max(cnt_ref[...]).astype(jnp.int32)
        carry = lax.fori_loop(first, i * nd, lambda j, c: block(pl.multiple_of(j * tk, tk), c, None),
                              (zero, zero, jnp.zeros((tq, HEAD_DIM), f32)))
        for d in range(nd):
            carry = block(pl.multiple_of(i * tq + d * tk, tk), carry, d)
        dq_ref[...] = carry[2].astype(dq_ref.dtype)

    hd = _qblk_spec(tq, HEAD_DIM, 0, 1)
    tri = pl.BlockSpec((tk, tk), lambda h, i: (0, 0))
    return pl.pallas_call(
        body, name=name, grid=(HEADS, S // tq),
        in_specs=[_qblk_spec(tq, *q[1:]), _head_spec(S, *k[1:]), _head_spec(S, *v[1:]), hd, _qblk_spec(8, HEAD_DIM, 0, 1),
                  hd, tri, tri],
        out_specs=[hd, _head_spec(S, HEAD_DIM, 0, 1), _head_spec(S, HEAD_DIM, 0, 1)],
        out_shape=[jax.ShapeDtypeStruct((S, HEADS * HEAD_DIM), f32), jax.ShapeDtypeStruct((S, HEADS * HEAD_DIM), f32),
                   jax.ShapeDtypeStruct((S, HEADS * HEAD_DIM), f32)],
        compiler_params=_params(("arbitrary", "arbitrary")))(q[0], k[0], v[0], tot, cnt, do, m_le, m_lt)


CONV_HALO = 32
CONV_A_BLK, CONV_G_BLK = 8, 9


def _glu(a, g):
    return a.astype(f32) * _sigmoid(g.astype(f32))


def conv_fwd(name, u, dw):
    S = u.shape[0]
    T = min(512, S)
    nT = S // T

    def body(a_ref, g_ref, ap_ref, gp_ref, dw_ref, y_ref, ext_ref):
        i = pl.program_id(0)
        prev = _glu(ap_ref[T - CONV_HALO:, :], gp_ref[T - CONV_HALO:, :])
        ext_ref[:CONV_HALO, :] = jnp.where(i > 0, prev, 0.0)
        ext_ref[CONV_HALO:, :] = _glu(a_ref[...], g_ref[...])
        acc = jnp.zeros((T, CONV_CH), f32)
        for w in range(CONV_WIDTH):
            acc = acc + dw_ref[w:w + 1, :] * ext_ref[pl.ds(w + CONV_HALO - (CONV_WIDTH - 1), T), :]
        y_ref[...] = acc

    cur = lambda blk: pl.BlockSpec((T, CONV_CH), lambda i: (i, blk))
    prv = lambda blk: pl.BlockSpec((T, CONV_CH), lambda i: (jnp.maximum(i - 1, 0), blk))
    return pl.pallas_call(
        body, name=name, grid=(nT,),
        in_specs=[cur(CONV_A_BLK), cur(CONV_G_BLK), prv(CONV_A_BLK), prv(CONV_G_BLK),
                  pl.BlockSpec(dw.shape, lambda i: (0, 0))],
        out_specs=pl.BlockSpec((T, CONV_CH), lambda i: (i, 0)),
        out_shape=jax.ShapeDtypeStruct((S, CONV_CH), f32),
        scratch_shapes=[pltpu.VMEM((T + CONV_HALO, CONV_CH), f32)],
        compiler_params=_params(("arbitrary",)))(u, u, u, u, dw)


def conv_bwd(name, u, dy, dw):
    S = u.shape[0]
    T = min(512, S)
    nT = S // T
    lead = CONV_HALO - (CONV_WIDTH - 1)

    def body(a_ref, g_ref, ap_ref, gp_ref, dy_ref, dyn_ref, dw_ref, du_ref, ddw_ref, ext_ref, dext_ref):
        i = pl.program_id(0)
        prev = _glu(ap_ref[T - CONV_HALO:, :], gp_ref[T - CONV_HALO:, :])
        ext_ref[:CONV_HALO, :] = jnp.where(i > 0, prev, 0.0)
        a, sg = a_ref[...].astype(f32), _sigmoid(g_ref[...].astype(f32))
        ext_ref[CONV_HALO:, :] = a * sg
        dyv = dy_ref[...]
        dext_ref[:T, :] = dyv
        dext_ref[T:, :] = jnp.where(i < nT - 1, dyn_ref[:CONV_HALO, :], 0.0)
        @pl.when(i == 0)
        def _():
            ddw_ref[...] = jnp.zeros_like(ddw_ref)

        dglu = jnp.zeros((T, CONV_CH), f32)
        for w in range(CONV_WIDTH):
            dglu = dglu + dw_ref[w:w + 1, :] * dext_ref[pl.ds(CONV_WIDTH - 1 - w, T), :]
            ddw_ref[w:w + 1, :] += jnp.sum(dyv * ext_ref[pl.ds(w + lead, T), :], axis=0, keepdims=True)

        du_ref[:, :CONV_CH] = (dglu * sg).astype(du_ref.dtype)
        du_ref[:, CONV_CH:] = (dglu * a * sg * (1.0 - sg)).astype(du_ref.dtype)

    cur = lambda blk: pl.BlockSpec((T, CONV_CH), lambda i: (i, blk))
    prv = lambda blk: pl.BlockSpec((T, CONV_CH), lambda i: (jnp.maximum(i - 1, 0), blk))
    return pl.pallas_call(
        body, name=name, grid=(nT,),
        in_specs=[cur(CONV_A_BLK), cur(CONV_G_BLK), prv(CONV_A_BLK), prv(CONV_G_BLK),
                  pl.BlockSpec((T, CONV_CH), lambda i: (i, 0)),
                  pl.BlockSpec((T, CONV_CH), lambda i: (jnp.minimum(i + 1, nT - 1), 0)),
                  pl.BlockSpec(dw.shape, lambda i: (0, 0))],
        out_specs=[pl.BlockSpec((T, 2 * CONV_CH), lambda i: (i, 0)), pl.BlockSpec(dw.shape, lambda i: (0, 0))],
        out_shape=[jax.ShapeDtypeStruct((S, 2 * CONV_CH), bf16), jax.ShapeDtypeStruct(dw.shape, f32)],
        scratch_shapes=[pltpu.VMEM((T + CONV_HALO, CONV_CH), f32), pltpu.VMEM((T + CONV_HALO, CONV_CH), f32)],
        compiler_params=_params(("arbitrary",)))(u, u, u, u, dy, dy, dw)


def rope_tables(pos_col):
    S = pos_col.shape[0]
    T = min(512, S)
    inv_freq = ROPE_BASE ** (-jnp.arange(0, MLA_ROPE, 2, dtype=f32) / MLA_ROPE)
    zeros = jnp.zeros((MLA_NOPE,), f32)
    inv_row = jnp.concatenate([zeros, inv_freq, inv_freq, zeros[:MLA_PAD - MLA_QK]]).reshape(1, MLA_PAD)

    def body(p_ref, f_ref, c_ref, s_ref):
        lane = lax.broadcasted_iota(jnp.int32, (T, MLA_PAD), 1)
        ang = p_ref[...].astype(f32) * f_ref[...]
        rot = jnp.logical_and(lane >= MLA_NOPE, lane < MLA_QK)
        c_ref[...] = jnp.where(rot, jnp.cos(ang), jnp.where(lane < MLA_NOPE, 1.0, 0.0))
        s_ref[...] = jnp.where(rot, jnp.sin(ang), 0.0)

    spec = pl.BlockSpec((T, MLA_PAD), lambda i: (i, 0))
    return pl.pallas_call(
        body, name="rope_tables", grid=(S // T,),
        in_specs=[pl.BlockSpec((T, 1), lambda i: (i, 0)), pl.BlockSpec((1, MLA_PAD), lambda i: (0, 0))],
        out_specs=[spec, spec], out_shape=[jax.ShapeDtypeStruct((S, MLA_PAD), f32)] * 2,
        compiler_params=_params(("parallel",)))(pos_col, inv_row)


def loss_head(y, target):
    S, D = y.shape
    T = min(512, S)

    def body(y_ref, t_ref, dy_ref, l_ref):
        i = pl.program_id(0)
        err = y_ref[...] - t_ref[...]
        dy_ref[...] = err * (1.0 / D)
        part = 0.5 * jnp.sum(jnp.sum(err * err, axis=-1, keepdims=True) * (1.0 / D), axis=0, keepdims=True)
        part = jnp.broadcast_to(part, l_ref.shape)

        @pl.when(i == 0)
        def _():
            l_ref[...] = part

        @pl.when(i > 0)
        def _():
            l_ref[...] += part

    spec = pl.BlockSpec((T, D), lambda i: (i, 0))
    return pl.pallas_call(
        body, name="loss_head", grid=(S // T,), in_specs=[spec, spec],
        out_specs=[spec, pl.BlockSpec((8, 128), lambda i: (0, 0))],
        out_shape=[jax.ShapeDtypeStruct((S, D), f32), jax.ShapeDtypeStruct((8, 128), f32)],
        compiler_params=_params(("arbitrary",)))(y, target)


def _row(a, w=None, off=0, st=0):
    return (a, w or a.shape[1], off, st)


def _out(tw, dt, w=None, off=0, st=0):
    return (tw, dt, w or tw, off, st)


ACT = bf16


def ffn_fwd(tag, x, g, w_in, w_out):
    (h,) = rowwise(f"{tag}_rms", fn_rms, [_row(x)], [g], [_out(D_MODEL, bf16)])
    u = mm(f"{tag}_in", h, w_in, "nn", out_dtype=ACT)
    (a,) = rowwise(f"{tag}_swiglu", fn_swiglu, [_row(u, FFN_HIDDEN, 0), _row(u, FFN_HIDDEN, 1)], [],
                   [_out(FFN_HIDDEN, bf16)])
    y = mm(f"{tag}_out", a, w_out, "nn", alpha=0.5, res=x)
    return y, (x, h, u, a)


def ffn_bwd(tag, saved, g, w_in, w_out, dy):
    x, h, u, a = saved
    d_w_out = mm(f"{tag}_dwout", a, dy, "tn", alpha=0.5)
    da = mm(f"{tag}_da", dy, w_out, "nt", alpha=0.5, out_dtype=ACT)
    (du,), _ = rowwise_bwd(f"{tag}_dswiglu", fn_swiglu, [_row(u, FFN_HIDDEN, 0), _row(u, FFN_HIDDEN, 1)], [],
                           [_row(da)], [_out(2 * FFN_HIDDEN, bf16)], merge=[(0, 1)], T=256)
    d_w_in = mm(f"{tag}_dwin", h, du, "tn")
    dh = mm(f"{tag}_dh", du, w_in, "nt", out_dtype=ACT)
    (dx,), (dg,) = rowwise_bwd(f"{tag}_drms", fn_rms_res, [_row(x)], [g], [_row(dy), _row(dh)], [_out(D_MODEL, f32)])
    return dx, dg, d_w_in, d_w_out


def _seg(name):
    for n, _, w, start in _U_SEGS:
        if n == name:
            return start, w
    raise KeyError(name)


def mix_fwd(tag, x, mem_n_in, tabs, p, ride=None):
    cos_t, sin_t = tabs
    (h,) = rowwise(f"{tag}_rms", fn_rms, [_row(x)], [p["mix_norm"]], [_out(D_MODEL, bf16)])
    u = mm(f"{tag}_in", h, p["w_in"], "nn", out_dtype=ACT)
    yc = conv_fwd(f"{tag}_conv", u, p["conv_dw"])
    (br_a,) = rowwise(f"{tag}_lnsilu", fn_ln_silu, [_row(yc)], [p["conv_b"], p["conv_ln_g"], p["conv_ln_b"]],
                      [_out(BRANCH_WIDTH, bf16)])
    sb0 = _seg("sb")[0] // HEAD_DIM
    qs, ks = rowwise(f"{tag}_sbprep", fn_sb_prep, [_row(u, HEAD_DIM, sb0, 1), _row(u, HEAD_DIM, sb0 + HEADS, 1)],
                     [p["sb_q_hnorm"], p["sb_k_hnorm"]],
                     [_out(BRANCH_WIDTH, bf16, HEAD_DIM, 0, 1), _out(BRANCH_WIDTH, bf16, HEAD_DIM, 0, 1)], H=HEADS)
    sb_v = _row(u, HEAD_DIM, sb0 + 2 * HEADS, 1)
    br_b, tot_b, cnt_b = sb_fwd(f"{tag}_sb", _row(qs, HEAD_DIM, 0, 1), _row(ks, HEAD_DIM, 0, 1), sb_v)
    ql_n, kvl_n = rowwise(f"{tag}_latrms", lambda a, b, ga, gb: (_rms(a, ga), _rms(b, gb)),
                          [_row(u, MLA_Q_LORA, _seg("qlat")[0] // MLA_Q_LORA), _row(u, MLA_KV_LORA, _seg("kvlat")[0] // MLA_KV_LORA)],
                          [p["mla_q_norm"], p["mla_kv_norm"]], [_out(MLA_Q_LORA, bf16), _out(MLA_KV_LORA, bf16)])
    qfull = mm(f"{tag}_uq", ql_n, p["mla_w_uq"], "nn", out_dtype=ACT)
    kvfull = mm(f"{tag}_ukv", kvl_n, p["mla_w_ukv"], "nn", out_dtype=ACT)
    kr_row = _row(u, HEAD_DIM, _seg("krope")[0] // HEAD_DIM, 0)
    (qr,) = rowwise(f"{tag}_mlaq", fn_mla_q, [_row(qfull, MLA_PAD, 0, 1), _row(cos_t), _row(sin_t)], [p["mla_q_hnorm"]],
                    [_out(HEADS * MLA_PAD, bf16, MLA_PAD, 0, 1)], H=HEADS)
    (kr,) = rowwise(f"{tag}_mlak", fn_mla_k, [_row(kvfull, HEAD_DIM, 0, 2), kr_row, _row(cos_t), _row(sin_t)],
                    [p["mla_k_hnorm"]], [_out(HEADS * MLA_PAD, bf16, MLA_PAD, 0, 1)], H=HEADS)
    mla_v = _row(kvfull, HEAD_DIM, 1, 2)
    br_c, lse_c, *landed = attn_fwd(f"{tag}_mla", _row(qr, MLA_PAD, 0, 1), _row(kr, MLA_PAD, 0, 1), mla_v,
                                    scale=MLA_QK ** -0.5, mask="chunk", ride=ride)
    (mem_n,) = rowwise(f"{tag}_memrms", fn_rms, [_row(mem_n_in)], [p["mem_norm"]], [_out(D_MODEL, bf16)])
    kvm = mm(f"{tag}_memkv", mem_n, p["mem_w_kv"], "nn", out_dtype=ACT)
    (km,) = rowwise(f"{tag}_memk", fn_rms, [_row(kvm, HEAD_DIM, 0, 1)], [p["mem_k_hnorm"]],
                    [_out(BRANCH_WIDTH, bf16, HEAD_DIM, 0, 1)], H=HEADS)
    mq0 = _seg("memq")[0] // HEAD_DIM
    (qm,) = rowwise(f"{tag}_memq", fn_rms, [_row(u, HEAD_DIM, mq0, 1)], [p["mem_q_hnorm"]],
                    [_out(BRANCH_WIDTH, bf16, HEAD_DIM, 0, 1)], H=HEADS)
    mem_v = _row(kvm, HEAD_DIM, HEADS, 1)
    br_d, lse_d = attn_fwd(f"{tag}_memattn", _row(qm, HEAD_DIM, 0, 1), _row(km, HEAD_DIM, 0, 1), mem_v,
                           scale=HEAD_DIM ** -0.5, mask=None)
    branches = (br_a, br_b, br_c, br_d)
    proj = [mm(f"{tag}_branch{b}", branches[b], p["w_branch"][b], "nn", out_dtype=ACT) for b in range(N_BRANCH)]
    gate_rows = [_row(u, D_MODEL, b) for b in range(N_BRANCH)]
    (merged,) = rowwise(f"{tag}_merge", fn_merge, gate_rows + [_row(t) for t in proj], [], [_out(D_MODEL, bf16)], T=256)
    y = mm(f"{tag}_out", merged, p["w_out"], "nn", res=x)
    saved = dict(x=x, h=h, u=u, yc=yc, qs=qs, ks=ks, ql_n=ql_n, kvl_n=kvl_n, qfull=qfull, kvfull=kvfull, qr=qr, kr=kr,
                 lse_c=lse_c, mem_n=mem_n, kvm=kvm, km=km, qm=qm, lse_d=lse_d, branches=branches, proj=proj,
                 merged=merged, tot_b=tot_b, cnt_b=cnt_b)
    return y, saved, (landed[0] if landed else None)


def mix_bwd(tag, sv, mem_n_in, tabs, p, dy):
    cos_t, sin_t = tabs
    u, S = sv["u"], sv["u"].shape[0]
    g = {}
    g["w_out"] = mm(f"{tag}_dwout", sv["merged"], dy, "tn")
    dmerged = mm(f"{tag}_dmerged", dy, p["w_out"], "nt", out_dtype=ACT)
    gate_rows = [_row(u, D_MODEL, b) for b in range(N_BRANCH)]
    d_merge, _ = rowwise_bwd(f"{tag}_dmerge", fn_merge, gate_rows + [_row(t) for t in sv["proj"]], [], [_row(dmerged)],
                             [_out(N_BRANCH * D_MODEL, bf16)] + [_out(D_MODEL, bf16)] * N_BRANCH, T=256,
                             merge=[tuple(range(N_BRANCH))] + [(N_BRANCH + b,) for b in range(N_BRANCH)])
    d_gates, d_proj = d_merge[:1], d_merge[1:]
    g["w_branch"] = [mm(f"{tag}_dwbranch{b}", sv["branches"][b], d_proj[b], "tn") for b in range(N_BRANCH)]
    d_br = [mm(f"{tag}_dbranch{b}", d_proj[b], p["w_branch"][b], "nt", out_dtype=ACT) for b in range(N_BRANCH)]
    (dyc,), (g["conv_b"], g["conv_ln_g"], g["conv_ln_b"]) = rowwise_bwd(
        f"{tag}_dlnsilu", fn_ln_silu, [_row(sv["yc"])], [p["conv_b"], p["conv_ln_g"], p["conv_ln_b"]], [_row(d_br[0])],
        [_out(BRANCH_WIDTH, f32)])
    du_conv, g["conv_dw"] = conv_bwd(f"{tag}_dconv", u, dyc, p["conv_dw"])
    sb0 = _seg("sb")[0] // HEAD_DIM
    sb_v = _row(u, HEAD_DIM, sb0 + 2 * HEADS, 1)
    dqs, dks, dv_sb = sb_bwd(f"{tag}_dsb", _row(sv["qs"], HEAD_DIM, 0, 1), _row(sv["ks"], HEAD_DIM, 0, 1), sb_v,
                             sv["tot_b"], sv["cnt_b"], d_br[1])
    (du_sbq, du_sbk), (g["sb_q_hnorm"], g["sb_k_hnorm"]) = rowwise_bwd(
        f"{tag}_dsbprep", fn_sb_prep, [_row(u, HEAD_DIM, sb0, 1), _row(u, HEAD_DIM, sb0 + HEADS, 1)],
        [p["sb_q_hnorm"], p["sb_k_hnorm"]], [_row(dqs, HEAD_DIM, 0, 1), _row(dks, HEAD_DIM, 0, 1)],
        [_out(BRANCH_WIDTH, bf16, HEAD_DIM, 0, 1), _out(BRANCH_WIDTH, bf16, HEAD_DIM, 0, 1)], H=HEADS)
    mla_v = _row(sv["kvfull"], HEAD_DIM, 1, 2)
    dqr, dkr, dv_mla = attn_bwd(f"{tag}_dmla", _row(sv["qr"], MLA_PAD, 0, 1), _row(sv["kr"], MLA_PAD, 0, 1), mla_v,
                                sv["branches"][2], d_br[2], sv["lse_c"], scale=MLA_QK ** -0.5, mask="chunk")
    (dqfull,), (g["mla_q_hnorm"],) = rowwise_bwd(
        f"{tag}_dmlaq", fn_mla_q, [_row(sv["qfull"], MLA_PAD, 0, 1), _row(cos_t), _row(sin_t)], [p["mla_q_hnorm"]],
        [_row(dqr, MLA_PAD, 0, 1)], [_out(HEADS * MLA_PAD, bf16, MLA_PAD, 0, 1)], H=HEADS, nondiff=(1, 2))
    kr_row = _row(u, HEAD_DIM, _seg("krope")[0] // HEAD_DIM, 0)
    (dkn, du_krope, dvp), (g["mla_k_hnorm"],) = rowwise_bwd(
        f"{tag}_dmlak", fn_mla_k_v, [_row(sv["kvfull"], HEAD_DIM, 0, 2), kr_row, mla_v, _row(cos_t), _row(sin_t)],
        [p["mla_k_hnorm"]], [_row(dkr, MLA_PAD, 0, 1), _row(dv_mla, HEAD_DIM, 0, 1)],
        [_out(BRANCH_WIDTH, bf16, HEAD_DIM, 0, 1), _out(HEAD_DIM, f32), _out(BRANCH_WIDTH, bf16, HEAD_DIM, 0, 1)],
        H=HEADS, nondiff=(3, 4))
    dkvfull = _interleave(f"{tag}_dkvfull", dkn, dvp)
    g["mla_w_uq"] = mm(f"{tag}_dwuq", sv["ql_n"], dqfull, "tn")
    g["mla_w_ukv"] = mm(f"{tag}_dwukv", sv["kvl_n"], dkvfull, "tn")
    dql_n = mm(f"{tag}_dqln", dqfull, p["mla_w_uq"], "nt", out_dtype=ACT)
    dkvl_n = mm(f"{tag}_dkvln", dkvfull, p["mla_w_ukv"], "nt", out_dtype=ACT)
    (du_lat,), (g["mla_q_norm"], g["mla_kv_norm"]) = rowwise_bwd(
        f"{tag}_dlatrms", lambda a, b, ga, gb: (_rms(a, ga), _rms(b, gb)),
        [_row(u, MLA_Q_LORA, _seg("qlat")[0] // MLA_Q_LORA), _row(u, MLA_KV_LORA, _seg("kvlat")[0] // MLA_KV_LORA)],
        [p["mla_q_norm"], p["mla_kv_norm"]], [_row(dql_n), _row(dkvl_n)], [_out(MLA_Q_LORA + MLA_KV_LORA, bf16)],
        merge=[(0, 1)])
    mem_v = _row(sv["kvm"], HEAD_DIM, HEADS, 1)
    dqm, dkm, dvm = attn_bwd(f"{tag}_dmemattn", _row(sv["qm"], HEAD_DIM, 0, 1), _row(sv["km"], HEAD_DIM, 0, 1), mem_v,
                             sv["branches"][3], d_br[3], sv["lse_d"], scale=HEAD_DIM ** -0.5, mask=None)
    mq0 = _seg("memq")[0] // HEAD_DIM
    (du_memq,), (g["mem_q_hnorm"],) = rowwise_bwd(
        f"{tag}_dmemq", fn_rms, [_row(u, HEAD_DIM, mq0, 1)], [p["mem_q_hnorm"]], [_row(dqm, HEAD_DIM, 0, 1)],
        [_out(BRANCH_WIDTH, bf16, HEAD_DIM, 0, 1)], H=HEADS)
    (dkvm_k, dkvm_v), (g["mem_k_hnorm"],) = rowwise_bwd(
        f"{tag}_dmemk", fn_mem_k_v, [_row(sv["kvm"], HEAD_DIM, 0, 1), mem_v], [p["mem_k_hnorm"]],
        [_row(dkm, HEAD_DIM, 0, 1), _row(dvm, HEAD_DIM, 0, 1)],
        [_out(BRANCH_WIDTH, bf16, HEAD_DIM, 0, 1), _out(BRANCH_WIDTH, bf16, HEAD_DIM, 0, 1)], H=HEADS)
    dkvm = jnp.concatenate([dkvm_k, dkvm_v], axis=1)
    g["mem_w_kv"] = mm(f"{tag}_dwmemkv", sv["mem_n"], dkvm, "tn")
    dmem_n = mm(f"{tag}_dmemn", dkvm, p["mem_w_kv"], "nt", out_dtype=ACT)
    _, (g["mem_norm"],) = rowwise_bwd(f"{tag}_dmemrms", fn_rms, [_row(mem_n_in)], [p["mem_norm"]], [_row(dmem_n)],
                                      [_out(D_MODEL, bf16)])
    du_krope_b = du_krope.astype(bf16)
    du = jnp.concatenate(list(d_gates) + [du_conv, du_sbq, du_sbk, dv_sb.astype(bf16), du_lat, du_memq,
                                          du_krope_b, jnp.zeros((S, U_WIDTH - _seg("krope")[0] - HEAD_DIM), bf16)], axis=1)
    g["w_in"] = mm(f"{tag}_dwin", sv["h"], du, "tn")
    dh = mm(f"{tag}_dh", du, p["w_in"], "nt", out_dtype=ACT)
    (dx,), (g["mix_norm"],) = rowwise_bwd(f"{tag}_drms", fn_rms_res, [_row(sv["x"])], [p["mix_norm"]],
                                          [_row(dy), _row(dh)], [_out(D_MODEL, f32)])
    return dx, g


def _interleave(name, a, b):
    S, W = a.shape
    T = min(512, S)

    def body(a_ref, b_ref, o_ref):
        o_ref[:, :HEAD_DIM] = a_ref[...]
        o_ref[:, HEAD_DIM:] = b_ref[...]

    blk = pl.BlockSpec((T, HEAD_DIM), lambda i, h: (i, h))
    return pl.pallas_call(
        body, name=name, grid=(S // T, W // HEAD_DIM), in_specs=[blk, blk],
        out_specs=pl.BlockSpec((T, 2 * HEAD_DIM), lambda i, h: (i, h)),
        out_shape=jax.ShapeDtypeStruct((S, 2 * W), a.dtype), compiler_params=_params(("parallel", "parallel")))(a, b)


def _u_layout(w):
    parts, at = [], 0
    for _, src, width, start in _U_SEGS:
        assert start == at
        parts.append(w[..., src:src + width])
        at += width
    parts.append(jnp.zeros(w.shape[:-1] + (U_WIDTH - at,), w.dtype))
    return jnp.concatenate(parts, axis=-1)


def _u_layout_inv(g):
    order = sorted(_U_SEGS, key=lambda s: s[1])
    return jnp.concatenate([g[..., start:start + width] for _, _, width, start in order], axis=-1)


def _pad_heads(w, n=MLA_QK, to=MLA_PAD):
    w = w.reshape(w.shape[:-1] + (HEADS, n))
    w = jnp.pad(w, [(0, 0)] * (w.ndim - 1) + [(0, to - n)])
    return w.reshape(w.shape[:-2] + (HEADS * to,))


def _unpad_heads(g, n=MLA_QK, to=MLA_PAD):
    g = g.reshape(g.shape[:-1] + (HEADS, to))[..., :n]
    return g.reshape(g.shape[:-2] + (HEADS * n,))


def layer_params(W):
    row = lambda name: W[name].reshape(1, -1).astype(f32)
    p = {n: row(n) for n in SMALL if n != "mla_q_hnorm" and n != "mla_k_hnorm"}
    for n in ("mla_q_hnorm", "mla_k_hnorm"):
        p[n] = jnp.pad(row(n), ((0, 0), (0, MLA_PAD - MLA_QK)))
    for n in ("ffn1_w_in", "ffn1_w_out", "ffn2_w_in", "ffn2_w_out", "mla_w_ukv", "mem_w_kv", "w_out"):
        p[n] = W[n]
    p["w_branch"] = [W["w_branch"][b] for b in range(N_BRANCH)]
    p["w_in"] = _u_layout(W["w_in"])
    p["mla_w_uq"] = _pad_heads(W["mla_w_uq"])
    p["conv_dw"] = jnp.pad(W["conv_dw"].astype(f32), ((0, 1), (0, 0)))
    return p


def layer_grads_to_original(g):
    out = dict(g)
    out["w_in"] = _u_layout_inv(g["w_in"])
    out["mla_w_uq"] = _unpad_heads(g["mla_w_uq"])
    out["conv_dw"] = g["conv_dw"][:CONV_WIDTH]
    out["w_branch"] = jnp.stack(g["w_branch"])
    for n in ("mla_q_hnorm", "mla_k_hnorm"):
        out[n] = g[n][:, :MLA_QK]
    return {n: (out[n].reshape(-1) if n in SMALL else out[n]) for n in out}


def local_step(x, mem, pos_col, target, weights_of, ride_of=lambda l: None, deliver=lambda l, landed: None):
    tabs = rope_tables(pos_col)
    params, saved = [], []
    for l in range(DEPTH):
        p = layer_params(weights_of(l))
        params.append(p)
        x, s1 = ffn_fwd(f"l{l}_ffn1", x, p["ffn1_norm"], p["ffn1_w_in"], p["ffn1_w_out"])
        x, s2, landed = mix_fwd(f"l{l}_mix", x, mem, tabs, p, ride=ride_of(l))
        deliver(l, landed)
        x, s3 = ffn_fwd(f"l{l}_ffn2", x, p["ffn2_norm"], p["ffn2_w_in"], p["ffn2_w_out"])
        saved.append((s1, s2, s3))
    dx, loss_blk = loss_head(x, target)
    grads = [None] * DEPTH
    for l in reversed(range(DEPTH)):
        p, (s1, s2, s3) = params[l], saved[l]
        dx, g_n2, g_in2, g_out2 = ffn_bwd(f"l{l}_ffn2", s3, p["ffn2_norm"], p["ffn2_w_in"], p["ffn2_w_out"], dx)
        dx, g = mix_bwd(f"l{l}_mix", s2, mem, tabs, p, dx)
        dx, g_n1, g_in1, g_out1 = ffn_bwd(f"l{l}_ffn1", s1, p["ffn1_norm"], p["ffn1_w_in"], p["ffn1_w_out"], dx)
        g.update(ffn1_norm=g_n1, ffn1_w_in=g_in1, ffn1_w_out=g_out1, ffn2_norm=g_n2, ffn2_w_in=g_in2, ffn2_w_out=g_out2)
        grads[l] = layer_grads_to_original(g)
    return loss_blk, dx, grads


_ANY = pl.BlockSpec(memory_space=pl.ANY)
_COMM = pltpu.CompilerParams(has_side_effects=True)


def _coords():
    return lax.axis_index("x"), lax.axis_index("y"), lax.axis_index("c")


def chip_exchange(name, src, scatter):
    shape = src.shape[1:] if scatter else src.shape

    def body(src_ref, out_ref, send_sems, recv_sems):
        x, y, c = _coords()
        copies = []
        for k, (px, py) in enumerate([(1 - x, y), (x, 1 - y), (1 - x, 1 - y)]):
            piece = src_ref.at[2 * px + py] if scatter else src_ref
            cp = pltpu.make_async_remote_copy(piece, out_ref.at[k], send_sems.at[k], recv_sems.at[k],
                                              device_id=(px, py, c), device_id_type=MESH)
            cp.start()
            copies.append(cp)
        for cp in copies:
            cp.wait()

    return pl.pallas_call(
        body, name=name, in_specs=[_ANY], out_specs=_ANY, out_shape=jax.ShapeDtypeStruct((3,) + shape, src.dtype),
        scratch_shapes=[pltpu.SemaphoreType.DMA((3,)), pltpu.SemaphoreType.DMA((3,))], compiler_params=_COMM)(src)


def sibling_exchange(name, src):
    def body(src_ref, out_ref, send_sem, recv_sem):
        x, y, c = _coords()
        cp = pltpu.make_async_remote_copy(src_ref, out_ref, send_sem, recv_sem, device_id=(x, y, 1 - c),
                                          device_id_type=MESH)
        cp.start()
        cp.wait()

    return pl.pallas_call(
        body, name=name, in_specs=[_ANY], out_specs=_ANY, out_shape=jax.ShapeDtypeStruct(src.shape, src.dtype),
        scratch_shapes=[pltpu.SemaphoreType.DMA, pltpu.SemaphoreType.DMA], compiler_params=_COMM)(src)


def all8_gather(name, src):
    def body(src_ref, out_ref, send_sems, recv_sems, local_sem):
        x, y, c = _coords()
        me = 4 * x + 2 * y + c
        mine = pltpu.make_async_copy(src_ref, out_ref.at[me], local_sem)
        mine.start()
        copies = []
        for k in range(1, 8):
            peer = (1 - x if k & 4 else x, 1 - y if k & 2 else y, 1 - c if k & 1 else c)
            cp = pltpu.make_async_remote_copy(src_ref, out_ref.at[me], send_sems.at[k - 1], recv_sems.at[k - 1],
                                              device_id=peer, device_id_type=MESH)
            cp.start()
            copies.append(cp)
        for cp in copies:
            cp.wait()
        mine.wait()

    return pl.pallas_call(
        body, name=name, in_specs=[_ANY], out_specs=_ANY, out_shape=jax.ShapeDtypeStruct((8,) + src.shape, src.dtype),
        scratch_shapes=[pltpu.SemaphoreType.DMA((7,)), pltpu.SemaphoreType.DMA((7,)), pltpu.SemaphoreType.DMA],
        compiler_params=_COMM)(src)


def sum8(name, g):
    def body(g_ref, o_ref):
        acc = g_ref[0]
        for k in range(1, 8):
            acc = acc + g_ref[k]
        o_ref[...] = acc

    return pl.pallas_call(body, name=name, out_shape=jax.ShapeDtypeStruct(g.shape[1:], g.dtype))(g)


PACK_COLS = 1024
PACKED = tuple(n for n in SHARDED if n != "conv_dw")
PACK_ROW_GRAIN = PACK_GRAIN // PACK_COLS


def _rows(shape):
    n = math.prod(shape)
    assert n % (16 * PACK_COLS) == 0, shape
    return n // PACK_COLS


def _pack_rows(pieces):
    rows = sum(p.shape[0] for p in pieces)
    pad = -rows % PACK_ROW_GRAIN
    if pad:
        pieces = pieces + [jnp.zeros((pad, PACK_COLS), pieces[0].dtype)]
    return jnp.concatenate(pieces, axis=0)


def _unpack_rows(packed, shard_shapes):
    out, at = {}, 0
    for n in PACKED:
        r = _rows(shard_shapes[n])
        out[n] = packed[at:at + r].reshape(shard_shapes[n])
        at += r
    return out


def gather_shards(name, own):
    n_chunks = 4
    rows = own.shape[1] // n_chunks
    assert own.shape[1] % (16 * n_chunks) == 0

    def body(own_ref, out_ref, send_sems, recv_sems, local_sem):
        x, y, c = _coords()
        sib = (x, y, 1 - c)
        mine = pltpu.make_async_copy(own_ref, out_ref.at[0], local_sem)
        mine.start()
        chips = [(2, (1 - x, y)), (1, (x, 1 - y)), (3, (1 - x, 1 - y))]

        def copy(k, j, src, m, half, to):
            sl = pl.ds(j * rows, rows)
            return pltpu.make_async_remote_copy(src.at[sl], out_ref.at[m, half, sl], send_sems.at[k * n_chunks + j],
                                                recv_sems.at[k * n_chunks + j], device_id=to, device_id_type=MESH)

        first = [[copy(k, j, own_ref.at[c], m, c, (px, py, c)) for j in range(n_chunks)]
                 for k, (m, (px, py)) in enumerate(chips)]
        for j in range(n_chunks):
            for k in range(3):
                first[k][j].start()
        passed = []
        for j in range(n_chunks):
            for k, (m, _) in enumerate(chips):
                first[k][j].wait_recv()
                cp = copy(3 + k, j, out_ref.at[m, c], m, c, sib)
                cp.start()
                passed.append(cp)
        for j in range(n_chunks):
            for k, (m, _) in enumerate(chips):
                copy(3 + k, j, out_ref.at[m, 1 - c], m, 1 - c, sib).wait_recv()
        for cp in [cp for per_chip in first for cp in per_chip] + passed:
            cp.wait_send()
        mine.wait()

    return pl.pallas_call(
        body, name=name, in_specs=[_ANY], out_specs=_ANY, out_shape=jax.ShapeDtypeStruct((4,) + own.shape, own.dtype),
        scratch_shapes=[pltpu.SemaphoreType.DMA((6 * n_chunks,)), pltpu.SemaphoreType.DMA((6 * n_chunks,)),
                        pltpu.SemaphoreType.DMA],
        compiler_params=_COMM)(own)


def pack_layer_shard(w, l):
    return _pack_rows([w[n][l].astype(bf16).reshape(-1, PACK_COLS) for n in PACKED]).reshape(2, -1, PACK_COLS)


def unpack_layer(by_mask, w):
    x, y, _ = _coords()
    me = 2 * x + y
    shapes = {n: w[n].shape[1:] for n in PACKED}
    pieces = [_unpack_rows(lax.dynamic_index_in_dim(by_mask, jnp.bitwise_xor(s, me), axis=0, keepdims=False)
                           .reshape(-1, PACK_COLS), shapes) for s in range(4)]
    return {n: jnp.concatenate([pieces[s][n] for s in range(4)], axis=SHARD_AXIS[n] - 1) for n in PACKED}


def gather_conv_dw(dw):
    rows = math.prod(dw.shape[:-1])
    every = all8_gather("ag_conv_dw", jnp.pad(dw.reshape(rows, -1), ((0, -rows % 8), (0, 0))))
    return jnp.concatenate([every[2 * s, :rows].reshape(dw.shape) for s in range(4)], axis=SHARD_AXIS["conv_dw"])


def _add_streams(name, ins, selectors, out_dtypes, rows, T=256):
    n_streams = max([a.shape[sel.index("s")] for a, sel in zip(ins, selectors) if "s" in sel] + [1])

    def spec(sel):
        def index(s, i, pf):
            lead = tuple(s if e == "s" else (pf[e[1]] if isinstance(e, tuple) else e) for e in sel)
            return lead + (i, 0)
        return pl.BlockSpec((None,) * len(sel) + (T, PACK_COLS), index)

    def body(pf_ref, *refs):
        acc = refs[0][...].astype(f32)
        for r in refs[1:len(ins)]:
            acc = acc + r[...].astype(f32)
        for o in refs[len(ins):]:
            o[...] = acc.astype(o.dtype)

    def run(pf):
        grid_spec = pltpu.PrefetchScalarGridSpec(
            num_scalar_prefetch=1, grid=(n_streams, rows // T), in_specs=[spec(sel) for sel in selectors],
            out_specs=[spec(("s",)) for _ in out_dtypes])
        return pl.pallas_call(
            body, name=name, grid_spec=grid_spec,
            out_shape=[jax.ShapeDtypeStruct((n_streams, rows, PACK_COLS), dt) for dt in out_dtypes],
            compiler_params=_params(("parallel", "parallel")))(pf, *ins)
    return run


def reduce_grads(grads, shard_shapes):
    x, y, c = _coords()
    me = 2 * x + y
    streams = []
    for s in range(4):
        pieces = []
        for n in PACKED:
            ax = SHARD_AXIS[n] - 1
            for g in grads:
                width = g[n].shape[ax] // 4
                pieces.append(lax.slice_in_dim(g[n], s * width, (s + 1) * width, axis=ax).reshape(-1, PACK_COLS))
        streams.append(_pack_rows(pieces))
    R = streams[0].shape[0] // 2
    G = jnp.concatenate(streams, axis=0).reshape(4, 2, R, PACK_COLS)
    pf = jnp.stack([c, me]).astype(jnp.int32)
    theirs = lax.dynamic_index_in_dim(G, 1 - c, axis=1, keepdims=False).astype(bf16)
    from_sib = sibling_exchange("rs_sibling", theirs)
    chip_sum, chip_sum_b = _add_streams("rs_add_sibling", [G, from_sib], [("s", ("pf", 0)), ("s",)], [f32, bf16], R)(pf)
    got = chip_exchange("rs_chips", chip_sum_b, scatter=True)
    (half,) = _add_streams("rs_add_chips", [chip_sum, got, got, got], [(("pf", 1),), (0,), (1,), (2,)], [f32], R)(pf)
    half = half[0]
    other = sibling_exchange("rs_final", half)
    shard = jnp.concatenate([jnp.where(c == 0, half, other), jnp.where(c == 0, other, half)], axis=0)
    return _unpack_rows(shard, shard_shapes)


def _small_pack(t, names=SMALL):
    flat = jnp.concatenate([t[n].reshape(-1) for n in names])
    total = -(-flat.shape[0] // SMALL_PAD) * SMALL_PAD
    return jnp.pad(flat, (0, total - flat.shape[0])).reshape(-1, 128)


def _small_unpack(a, shapes, names=SMALL):
    flat, out, at = a.reshape(-1), {}, 0
    for n in names:
        k = math.prod(shapes[n])
        out[n] = flat[at:at + k].reshape(shapes[n])
        at += k
    return out


def adamw(name, w, g, m, v):
    shape = w.shape
    two = lambda a: a.reshape(-1, shape[-1])
    rows = two(w).shape[0]
    T = 256 if rows % 256 == 0 else rows
    outs = rowwise(name, fn_adamw, [_row(two(w)), _row(two(g)), _row(two(m)), _row(two(v))], [],
                   [_out(shape[-1], f32)] * 3, T=T)
    return [o.reshape(shape) for o in outs]


def kernel(x, mem, positions, ffn1_norm, ffn1_w_in, ffn1_w_out, mix_norm, w_in, conv_dw, conv_b, conv_ln_g, conv_ln_b, sb_q_hnorm, sb_k_hnorm, mla_q_norm, mla_w_uq, mla_kv_norm, mla_w_ukv, mla_q_hnorm, mla_k_hnorm, mem_norm, mem_w_kv, mem_q_hnorm, mem_k_hnorm, w_branch, w_out, ffn2_norm, ffn2_w_in, ffn2_w_out, loss_target, m_ffn1_norm, m_ffn1_w_in, m_ffn1_w_out, m_mix_norm, m_w_in, m_conv_dw, m_conv_b, m_conv_ln_g, m_conv_ln_b, m_sb_q_hnorm, m_sb_k_hnorm, m_mla_q_norm, m_mla_w_uq, m_mla_kv_norm, m_mla_w_ukv, m_mla_q_hnorm, m_mla_k_hnorm, m_mem_norm, m_mem_w_kv, m_mem_q_hnorm, m_mem_k_hnorm, m_w_branch, m_w_out, m_ffn2_norm, m_ffn2_w_in, m_ffn2_w_out, v_ffn1_norm, v_ffn1_w_in, v_ffn1_w_out, v_mix_norm, v_w_in, v_conv_dw, v_conv_b, v_conv_ln_g, v_conv_ln_b, v_sb_q_hnorm, v_sb_k_hnorm, v_mla_q_norm, v_mla_w_uq, v_mla_kv_norm, v_mla_w_ukv, v_mla_q_hnorm, v_mla_k_hnorm, v_mem_norm, v_mem_w_kv, v_mem_q_hnorm, v_mem_k_hnorm, v_w_branch, v_w_out, v_ffn2_norm, v_ffn2_w_in, v_ffn2_w_out):
    w = dict(zip(WEIGHTS, (ffn1_norm, ffn1_w_in, ffn1_w_out, mix_norm, w_in, conv_dw, conv_b, conv_ln_g, conv_ln_b, sb_q_hnorm, sb_k_hnorm, mla_q_norm, mla_w_uq, mla_kv_norm, mla_w_ukv, mla_q_hnorm, mla_k_hnorm, mem_norm, mem_w_kv, mem_q_hnorm, mem_k_hnorm, w_branch, w_out, ffn2_norm, ffn2_w_in, ffn2_w_out)))
    m = dict(zip(WEIGHTS, (m_ffn1_norm, m_ffn1_w_in, m_ffn1_w_out, m_mix_norm, m_w_in, m_conv_dw, m_conv_b, m_conv_ln_g, m_conv_ln_b, m_sb_q_hnorm, m_sb_k_hnorm, m_mla_q_norm, m_mla_w_uq, m_mla_kv_norm, m_mla_w_ukv, m_mla_q_hnorm, m_mla_k_hnorm, m_mem_norm, m_mem_w_kv, m_mem_q_hnorm, m_mem_k_hnorm, m_w_branch, m_w_out, m_ffn2_norm, m_ffn2_w_in, m_ffn2_w_out)))
    v = dict(zip(WEIGHTS, (v_ffn1_norm, v_ffn1_w_in, v_ffn1_w_out, v_mix_norm, v_w_in, v_conv_dw, v_conv_b, v_conv_ln_g, v_conv_ln_b, v_sb_q_hnorm, v_sb_k_hnorm, v_mla_q_norm, v_mla_w_uq, v_mla_kv_norm, v_mla_w_ukv, v_mla_q_hnorm, v_mla_k_hnorm, v_mem_norm, v_mem_w_kv, v_mem_q_hnorm, v_mem_k_hnorm, v_w_branch, v_w_out, v_ffn2_norm, v_ffn2_w_in, v_ffn2_w_out)))
    S = x.shape[1]
    packed = [pack_layer_shard(w, l) for l in range(DEPTH)]
    conv_dw_full = gather_conv_dw(w["conv_dw"])
    gathered = {0: gather_shards("ag_shards_l0", packed[0])}

    def weights_of(l):
        W = unpack_layer(gathered[l], w)
        W["conv_dw"] = conv_dw_full[l]
        W.update({n: w[n][l] for n in SMALL})
        return W

    def deliver(l, landed):
        if landed is not None:
            gathered[l + 1] = landed

    loss_blk, dx, g = local_step(x[0], mem[0], positions.reshape(S, 1), loss_target[0], weights_of,
                                 lambda l: packed[l + 1] if l + 1 < DEPTH else None, deliver)
    loss = lax.psum(loss_blk[0, 0], ("x", "y", "c"))
    grads = reduce_grads(g, {n: w[n].shape for n in PACKED})
    small_shapes = {n: w[n].shape for n in SMALL}
    reduced = SMALL + ("conv_dw",)
    reduced_shapes = dict(small_shapes, conv_dw=conv_dw_full.shape)
    g_all = {n: jnp.stack([gl[n] for gl in g]) for n in reduced}
    g_all = sum8("small_sum", all8_gather("small_gather", _small_pack(g_all, reduced)))
    g_all = _small_unpack(g_all, reduced_shapes, reduced)
    width = w["conv_dw"].shape[-1]
    x_pos, y_pos, _ = _coords()
    grads["conv_dw"] = lax.dynamic_slice_in_dim(g_all.pop("conv_dw"), (2 * x_pos + y_pos) * width, width, axis=2)
    grads.update(g_all)
    g_small = _small_pack(grads)
    delta, new_m, new_v = {}, {}, {}
    for n in SHARDED:
        delta[n], new_m[n], new_v[n] = adamw(f"adamw_{n}", w[n], grads[n], m[n], v[n])
    d_s, m_s, v_s = adamw("adamw_small", _small_pack(w), g_small, _small_pack(m), _small_pack(v))
    for t, packed in ((delta, d_s), (new_m, m_s), (new_v, v_s)):
        t.update(_small_unpack(packed, small_shapes))
    return (loss, dx.reshape(x.shape), *[grads[n] for n in WEIGHTS], *[delta[n] for n in WEIGHTS],
            *[new_m[n] for n in WEIGHTS], *[new_v[n] for n in WEIGHTS])
```

```python
import math

import jax
import jax.numpy as jnp
from jax import lax
from jax.experimental import pallas as pl
from jax.experimental.pallas import tpu as pltpu

f32, bf16 = jnp.float32, jnp.bfloat16

D_MODEL = 1024
DEPTH = 4
CHUNK = 64
FFN_HIDDEN = 2048
CONV_CH = 512
CONV_WIDTH = 31
HEADS = 4
HEAD_DIM = 128
MLA_NOPE = 128
MLA_ROPE = 64
MLA_QK = MLA_NOPE + MLA_ROPE
MLA_PAD = 256
MLA_Q_LORA = 256
MLA_KV_LORA = 256
N_BRANCH = 4
BRANCH_WIDTH = 512
ROPE_BASE = 10000.0
EPS = 1e-6
NEG_INF = -1e30
IN_WIDTH = 7744
U_WIDTH = 8192
_U_SEGS = (("gates", 3648, 4096, 0), ("conv", 0, 1024, 4096), ("sb", 1024, 1536, 5120), ("qlat", 2560, 256, 6656),
           ("kvlat", 2816, 256, 6912), ("memq", 3136, 512, 7168), ("krope", 3072, 64, 7680))
U_PAD_FROM = 7744

ADAM_LR, ADAM_B1, ADAM_B2, ADAM_EPS, ADAM_WD, ADAM_STEP = 0.001, 0.9, 0.999, 1e-08, 0.01, 10

VMEM_LIMIT = 48 * 1024 * 1024
MESH = pl.DeviceIdType.MESH

SHARDED = ("ffn1_w_in", "ffn1_w_out", "w_in", "conv_dw", "mla_w_uq", "mla_w_ukv", "mem_w_kv", "w_branch", "w_out",
           "ffn2_w_in", "ffn2_w_out")
SHARD_AXIS = {"ffn1_w_in": 2, "ffn1_w_out": 1, "w_in": 2, "conv_dw": 2, "mla_w_uq": 2, "mla_w_ukv": 2, "mem_w_kv": 1,
              "w_branch": 3, "w_out": 1, "ffn2_w_in": 2, "ffn2_w_out": 1}
SMALL = ("ffn1_norm", "mix_norm", "conv_b", "conv_ln_g", "conv_ln_b", "sb_q_hnorm", "sb_k_hnorm", "mla_q_norm",
         "mla_kv_norm", "mla_q_hnorm", "mla_k_hnorm", "mem_norm", "mem_q_hnorm", "mem_k_hnorm", "ffn2_norm")
WEIGHTS = ("ffn1_norm", "ffn1_w_in", "ffn1_w_out", "mix_norm", "w_in", "conv_dw", "conv_b", "conv_ln_g", "conv_ln_b",
           "sb_q_hnorm", "sb_k_hnorm", "mla_q_norm", "mla_w_uq", "mla_kv_norm", "mla_w_ukv", "mla_q_hnorm",
           "mla_k_hnorm", "mem_norm", "mem_w_kv", "mem_q_hnorm", "mem_k_hnorm", "w_branch", "w_out", "ffn2_norm",
           "ffn2_w_in", "ffn2_w_out")
PACK_GRAIN = 2 * 256 * 1024
SMALL_PAD = 8 * 128


def _params(sem, vmem=VMEM_LIMIT):
    return pltpu.CompilerParams(dimension_semantics=sem, vmem_limit_bytes=vmem)


def _pick(n, pref):
    for t in pref:
        if n % t == 0:
            return t
    return n


def mm(name, a, b, form, *, out_dtype=f32, alpha=1.0, res=None, tm=None, tn=None, tk=None):
    if form == "nn":
        (M, K), (K2, N) = a.shape, b.shape
    elif form == "nt":
        (M, K), (N, K2) = a.shape, b.shape
    else:
        (K, M), (K2, N) = a.shape, b.shape
    assert K == K2, (name, a.shape, b.shape)
    tm = tm or _pick(M, (1024, 512, 256, 128))
    tn = tn or _pick(N, (1024, 512, 256, 128))
    tk = tk or _pick(K, (2048, 1024, 512, 256))
    nk = K // tk
    if form == "nn":
        a_spec = pl.BlockSpec((tm, tk), lambda i, j, k: (i, k))
        b_spec = pl.BlockSpec((tk, tn), lambda i, j, k: (k, j))
        dims = (((1,), (0,)), ((), ()))
    elif form == "nt":
        a_spec = pl.BlockSpec((tm, tk), lambda i, j, k: (i, k))
        b_spec = pl.BlockSpec((tn, tk), lambda i, j, k: (j, k))
        dims = (((1,), (1,)), ((), ()))
    else:
        a_spec = pl.BlockSpec((tk, tm), lambda i, j, k: (k, i))
        b_spec = pl.BlockSpec((tk, tn), lambda i, j, k: (k, j))
        dims = (((0,), (0,)), ((), ()))
    o_spec = pl.BlockSpec((tm, tn), lambda i, j, k: (i, j))
    has_res = res is not None

    def body(a_ref, b_ref, *rest):
        if has_res:
            r_ref, o_ref, acc_ref = rest
        else:
            o_ref, acc_ref = rest
        k = pl.program_id(2)
        part = lax.dot_general(a_ref[...].astype(bf16), b_ref[...].astype(bf16), dims, preferred_element_type=f32)

        def finish(acc):
            r = acc if alpha == 1.0 else acc * alpha
            if has_res:
                r = r_ref[...].astype(f32) + r
            o_ref[...] = r.astype(o_ref.dtype)

        if nk == 1:
            finish(part)
        else:
            @pl.when(k == 0)
            def _():
                acc_ref[...] = part

            @pl.when(k > 0)
            def _():
                acc_ref[...] += part

            @pl.when(k == nk - 1)
            def _():
                finish(acc_ref[...])

    ins = [a, b] + ([res] if has_res else [])
    in_specs = [a_spec, b_spec] + ([o_spec] if has_res else [])
    return pl.pallas_call(
        body, name=name, grid=(M // tm, N // tn, nk), in_specs=in_specs, out_specs=o_spec,
        out_shape=jax.ShapeDtypeStruct((M, N), out_dtype), scratch_shapes=[pltpu.VMEM((tm, tn), f32)],
        compiler_params=_params(("parallel", "parallel", "arbitrary")))(*ins)


class _Span:
    def __init__(self, w, off, st, H):
        self.w = w
        if st == 0 or H == 1:
            self.width, self.index, self.base, self.step = w, off, 0, 0
        else:
            self.width, self.index = st * H * w, off // (st * H)
            self.base, self.step = (off % (st * H)) * w, st * w

    def spec(self, T):
        return pl.BlockSpec((T, self.width), lambda i: (i, self.index))

    def lanes(self, h):
        return slice(self.base + h * self.step, self.base + h * self.step + self.w)


def _full_spec(p):
    return pl.BlockSpec(p.shape, lambda i: (0,) * p.ndim)


def rowwise(name, fn, rows, params, outs, *, T=512, H=1):
    S = rows[0][0].shape[0]
    T = min(T, S)
    n_r, n_p = len(rows), len(params)
    in_spans = [_Span(w, off, st, H) for (_, w, off, st) in rows]
    out_spans = [_Span(w, off, st, H) for (_, _, w, off, st) in outs]

    def body(*refs):
        p_vals = [r[...] for r in refs[n_r:n_r + n_p]]
        for h in range(H):
            vals = [r[:, sp.lanes(h)] for r, sp in zip(refs[:n_r], in_spans)]
            res = fn(*vals, *p_vals)
            for o_ref, sp, r in zip(refs[n_r + n_p:], out_spans, res):
                o_ref[:, sp.lanes(h)] = r.astype(o_ref.dtype)

    res = pl.pallas_call(
        body, name=name, grid=(S // T,),
        in_specs=[sp.spec(T) for sp in in_spans] + [_full_spec(p) for p in params],
        out_specs=[sp.spec(T) for sp in out_spans],
        out_shape=[jax.ShapeDtypeStruct((S, tw), dt) for (tw, dt, _, _, _) in outs],
        compiler_params=_params(("parallel",)))(*[r[0] for r in rows], *params)
    return res


def rowwise_bwd(name, fn, rows, params, douts, drows, *, T=512, H=1, nondiff=(), merge=None):
    S = rows[0][0].shape[0]
    T = min(T, S)
    n_r, n_p, n_d = len(rows), len(params), len(douts)
    diff_idx = [k for k in range(n_r) if k not in nondiff]
    merge = merge or [(j,) for j in range(len(diff_idx))]
    shared = [H > 1 and rows[diff_idx[pos[0]]][3] == 0 for pos in merge]
    n_o = len(merge)
    assert n_o == len(drows)
    in_spans = [_Span(w, off, st, H) for (_, w, off, st) in rows]
    d_spans = [_Span(w, off, st, H) for (_, w, off, st) in douts]
    out_spans = [_Span(w, off, st, H) for (_, _, w, off, st) in drows]

    def body(*refs):
        i = pl.program_id(0)
        p_vals = [r[...].astype(f32) for r in refs[n_r:n_r + n_p]]
        out_refs = refs[n_r + n_p + n_d:]
        shared_sum = [None] * n_o
        p_sum = [None] * n_p
        for h in range(H):
            row_vals = [r[:, sp.lanes(h)].astype(f32) for r, sp in zip(refs[:n_r], in_spans)]
            d_vals = [r[:, sp.lanes(h)].astype(f32) for r, sp in zip(refs[n_r + n_p:n_r + n_p + n_d], d_spans)]

            def f(*args):
                full = list(row_vals)
                for k, v in zip(diff_idx, args[:len(diff_idx)]):
                    full[k] = v
                return tuple(fn(*full, *args[len(diff_idx):]))

            _, vjp = jax.vjp(f, *[row_vals[k] for k in diff_idx], *p_vals)
            cts = vjp(tuple(d_vals))
            row_cts = [cts[pos[0]] if len(pos) == 1 else jnp.concatenate([cts[j] for j in pos], axis=-1) for pos in merge]
            for j, (o_ref, sp, ct) in enumerate(zip(out_refs[:n_o], out_spans, row_cts)):
                if shared[j]:
                    shared_sum[j] = ct if h == 0 else shared_sum[j] + ct
                else:
                    o_ref[:, sp.lanes(h)] = ct.astype(o_ref.dtype)
            for j, ct in enumerate(cts[len(diff_idx):]):
                p_sum[j] = ct if h == 0 else p_sum[j] + ct
        for j, o_ref in enumerate(out_refs[:n_o]):
            if shared[j]:
                o_ref[...] = shared_sum[j].astype(o_ref.dtype)
        for o_ref, ct in zip(out_refs[n_o:], p_sum):
            @pl.when(i == 0)
            def _():
                o_ref[...] = ct

            @pl.when(i > 0)
            def _():
                o_ref[...] += ct

    res = pl.pallas_call(
        body, name=name, grid=(S // T,),
        in_specs=[sp.spec(T) for sp in in_spans] + [_full_spec(p) for p in params] + [sp.spec(T) for sp in d_spans],
        out_specs=[sp.spec(T) for sp in out_spans] + [_full_spec(p) for p in params],
        out_shape=[jax.ShapeDtypeStruct((S, tw), dt) for (tw, dt, _, _, _) in drows]
        + [jax.ShapeDtypeStruct(p.shape, f32) for p in params],
        compiler_params=_params(("arbitrary",)))(*[r[0] for r in rows], *params, *[d[0] for d in douts])
    return res[:n_o], res[n_o:]


def _rms(x, g, n=None):
    x = x.astype(f32)
    n = n or x.shape[-1]
    return x * lax.rsqrt(jnp.sum(x * x, axis=-1, keepdims=True) * (1.0 / n) + EPS) * g.astype(f32)


def _sigmoid(x):
    return 1.0 / (1.0 + jnp.exp(-x))


def _silu(x):
    return x * _sigmoid(x)


def fn_rms(x, g):
    return (_rms(x, g),)


def fn_rms_res(x, g):
    return (x.astype(f32), _rms(x, g))


def fn_swiglu(gate, up):
    return (_silu(gate.astype(f32)) * up.astype(f32),)


def fn_sb_prep(q, k, gq, gk):
    return (_rms(q, gq), _rms(k, gk))


def fn_ln_silu(y, b, g, beta):
    y = y.astype(f32) + b
    mu = jnp.mean(y, axis=-1, keepdims=True)
    var = jnp.mean(jnp.square(y - mu), axis=-1, keepdims=True)
    return (_silu((y - mu) * lax.rsqrt(var + EPS) * g + beta),)


def fn_merge(g0, g1, g2, g3, p0, p1, p2, p3):
    out = _sigmoid(g0.astype(f32)) * p0.astype(f32)
    for g, p in ((g1, p1), (g2, p2), (g3, p3)):
        out = out + _sigmoid(g.astype(f32)) * p.astype(f32)
    return (out,)


def _rot_fwd(x):
    z = jnp.zeros_like(x[:, :MLA_NOPE])
    h = MLA_ROPE // 2
    return jnp.concatenate([z, -x[:, MLA_NOPE + h:MLA_QK], x[:, MLA_NOPE:MLA_NOPE + h], z[:, :MLA_PAD - MLA_QK]], axis=-1)


def _rot_bwd(g):
    z = jnp.zeros_like(g[:, :MLA_NOPE])
    h = MLA_ROPE // 2
    return jnp.concatenate([z, g[:, MLA_NOPE + h:MLA_QK], -g[:, MLA_NOPE:MLA_NOPE + h], z[:, :MLA_PAD - MLA_QK]], axis=-1)


@jax.custom_vjp
def _rope(x, c, s):
    return x * c + _rot_fwd(x) * s


def _rope_f(x, c, s):
    return _rope(x, c, s), (c, s)


def _rope_b(res, g):
    c, s = res
    return g * c + _rot_bwd(g * s), jnp.zeros_like(c), jnp.zeros_like(s)


_rope.defvjp(_rope_f, _rope_b)


def fn_mla_q(q, c, s, gain):
    return (_rope(_rms(q, gain, MLA_QK), c, s),)


def fn_mla_k(kn, kr, c, s, gain):
    k = jnp.concatenate([kn.astype(f32), kr.astype(f32)], axis=-1)
    return (_rope(_rms(k, gain, MLA_QK), c, s),)


def fn_mla_k_v(kn, kr, v, c, s, gain):
    return (fn_mla_k(kn, kr, c, s, gain)[0], v.astype(f32))


def fn_mem_k_v(k, v, gain):
    return (_rms(k, gain), v.astype(f32))


def fn_adamw(w, g, m, v):
    m = ADAM_B1 * m + (1.0 - ADAM_B1) * g
    v = ADAM_B2 * v + (1.0 - ADAM_B2) * jnp.square(g)
    m_hat = m / (1.0 - ADAM_B1 ** ADAM_STEP)
    v_hat = v / (1.0 - ADAM_B2 ** ADAM_STEP)
    delta = -ADAM_LR * (m_hat / (jnp.sqrt(v_hat) + ADAM_EPS) + ADAM_WD * w)
    return delta, m, v


def _head_spec(rows, w, off, st):
    return pl.BlockSpec((rows, w), lambda h, i: (0, off + st * h))


def _qblk_spec(B, w, off, st):
    return pl.BlockSpec((B, w), lambda h, i: (i, off + st * h))


def _chunk_mask(tq, tk, d):
    r = lax.broadcasted_iota(jnp.int32, (tq, tk), 0) // CHUNK
    c = (d * tk + lax.broadcasted_iota(jnp.int32, (tq, tk), 1)) // CHUNK
    return c <= r


def _strict_mask(tq, tk, d):
    r = lax.broadcasted_iota(jnp.int32, (tq, tk), 0)
    c = d * tk + lax.broadcasted_iota(jnp.int32, (tq, tk), 1)
    return c < r


_NT = (((1,), (1,)), ((), ()))
_TN = (((0,), (0,)), ((), ()))


def _tiles(Sq, Sk, mask, tq, tk):
    tq = min(tq, Sq)
    if mask is None:
        return tq, Sk, 0
    tk = min(tk, tq)
    assert Sq == Sk and tq % tk == 0 and tk % CHUNK == 0
    return tq, tk, tq // tk


def _gather_in_steps(own_ref, out_ref, send_sems, recv_sems, local_sem, h, i, last_h, last_i):
    x, y, c = lax.axis_index("x"), lax.axis_index("y"), lax.axis_index("c")
    sib = (x, y, 1 - c)
    chips = [(2, (1 - x, y)), (1, (x, 1 - y)), (3, (1 - x, 1 - y))]

    def copy(k, src, m, half, to):
        return pltpu.make_async_remote_copy(src, out_ref.at[m, half], send_sems.at[k], recv_sems.at[k],
                                            device_id=to, device_id_type=MESH)

    mine = pltpu.make_async_copy(own_ref, out_ref.at[0], local_sem)
    first = [copy(k, own_ref.at[c], m, c, (px, py, c)) for k, (m, (px, py)) in enumerate(chips)]
    passed = [copy(3 + k, out_ref.at[m, c], m, c, sib) for k, (m, _) in enumerate(chips)]
    landed = [copy(3 + k, out_ref.at[m, 1 - c], m, 1 - c, sib) for k, (m, _) in enumerate(chips)]

    @pl.when(jnp.logical_and(h == 0, i == 0))
    def _():
        mine.start()
        for cp in first:
            cp.start()

    @pl.when(jnp.logical_and(h == last_h, i == 0))
    def _():
        for k in range(3):
            first[k].wait_recv()
            passed[k].start()

    @pl.when(jnp.logical_and(h == last_h, i == last_i))
    def _():
        for cp in landed:
            cp.wait_recv()
        for cp in first + passed:
            cp.wait_send()
        mine.wait()


def attn_fwd(name, q, k, v, *, scale, mask, tq=1024, tk=512, ride=None):
    Sq, Sk = q[0].shape[0], k[0].shape[0]
    tq, tk, nd = _tiles(Sq, Sk, mask, tq, tk)
    n_q = Sq // tq

    def body(q_ref, k_ref, v_ref, *rest):
        if ride is None:
            o_ref, lse_ref = rest
        else:
            own_ref, o_ref, lse_ref, land_ref, send_sems, recv_sems, local_sem = rest
            _gather_in_steps(own_ref, land_ref, send_sems, recv_sems, local_sem, pl.program_id(0), pl.program_id(1),
                             HEADS - 1, n_q - 1)
        i = pl.program_id(1)
        qv = q_ref[...]

        def block(off, carry, d):
            m, l, acc = carry
            kb, vb = k_ref[pl.ds(off, tk), :].astype(bf16), v_ref[pl.ds(off, tk), :].astype(bf16)
            s = lax.dot_general(qv, kb, _NT, preferred_element_type=f32) * scale
            if d is not None:
                s = jnp.where(_chunk_mask(tq, tk, d), s, NEG_INF)
            m_new = jnp.maximum(m, jnp.max(s, axis=-1, keepdims=True))
            p = jnp.exp(s - m_new)
            corr = jnp.exp(m - m_new)
            l = l * corr + jnp.sum(p, axis=-1, keepdims=True)
            acc = acc * corr + jnp.dot(p.astype(bf16), vb, preferred_element_type=f32)
            return m_new, l, acc

        carry = (jnp.full((tq, 1), NEG_INF, f32), jnp.zeros((tq, 1), f32), jnp.zeros((tq, HEAD_DIM), f32))
        if nd:
            carry = lax.fori_loop(0, i * nd, lambda j, c: block(pl.multiple_of(j * tk, tk), c, None), carry)
            for d in range(nd):
                carry = block(pl.multiple_of(i * tq + d * tk, tk), carry, d)
        else:
            carry = block(0, carry, None)
        m, l, acc = carry
        o_ref[...] = (acc / l).astype(o_ref.dtype)
        lse_ref[...] = jnp.broadcast_to(m + jnp.log(l), (tq, HEAD_DIM))

    in_specs = [_qblk_spec(tq, *q[1:]), _head_spec(Sk, *k[1:]), _head_spec(Sk, *v[1:])]
    out_specs = [_qblk_spec(tq, HEAD_DIM, 0, 1), _qblk_spec(tq, HEAD_DIM, 0, 1)]
    out_shape = [jax.ShapeDtypeStruct((Sq, HEADS * HEAD_DIM), f32), jax.ShapeDtypeStruct((Sq, HEADS * HEAD_DIM), f32)]
    if ride is None:
        return pl.pallas_call(body, name=name, grid=(HEADS, n_q), in_specs=in_specs, out_specs=out_specs,
                              out_shape=out_shape, compiler_params=_params(("parallel", "arbitrary")))(q[0], k[0], v[0])
    any_space = pl.BlockSpec(memory_space=pl.ANY)
    return pl.pallas_call(
        body, name=name, grid=(HEADS, n_q), in_specs=in_specs + [any_space], out_specs=out_specs + [any_space],
        out_shape=out_shape + [jax.ShapeDtypeStruct((4,) + ride.shape, ride.dtype)],
        scratch_shapes=[pltpu.SemaphoreType.DMA((6,)), pltpu.SemaphoreType.DMA((6,)), pltpu.SemaphoreType.DMA],
        compiler_params=pltpu.CompilerParams(dimension_semantics=("arbitrary", "arbitrary"), vmem_limit_bytes=VMEM_LIMIT,
                                             has_side_effects=True))(q[0], k[0], v[0], ride)


def _exchange_in_steps(kind, src_ref, out_ref, send_sems, recv_sems, is_first, is_last):
    x, y, c = lax.axis_index("x"), lax.axis_index("y"), lax.axis_index("c")
    if kind == "sibling":
        copies = [pltpu.make_async_remote_copy(src_ref, out_ref, send_sems.at[0], recv_sems.at[0],
                                               device_id=(x, y, 1 - c), device_id_type=MESH)]
    else:
        copies = [pltpu.make_async_remote_copy(src_ref.at[2 * px + py], out_ref.at[k], send_sems.at[k], recv_sems.at[k],
                                               device_id=(px, py, c), device_id_type=MESH)
                  for k, (px, py) in enumerate([(1 - x, y), (x, 1 - y), (1 - x, 1 - y)])]

    @pl.when(is_first)
    def _():
        for cp in copies:
            cp.start()

    @pl.when(is_last)
    def _():
        for cp in copies:
            cp.wait()


def _ride_call(body, name, grid, in_specs, out_specs, out_shape, operands, ride, vmem=VMEM_LIMIT):
    if ride is None:
        return pl.pallas_call(body, name=name, grid=grid, in_specs=in_specs, out_specs=out_specs, out_shape=out_shape,
                              compiler_params=_params(("arbitrary",) * len(grid), vmem))(*operands)
    kind, src = ride
    n = 1 if kind == "sibling" else 3
    landing = jax.ShapeDtypeStruct(src.shape if kind == "sibling" else (3,) + src.shape[1:], src.dtype)
    any_space = pl.BlockSpec(memory_space=pl.ANY)
    return pl.pallas_call(
        body, name=name, grid=grid, in_specs=in_specs + [any_space], out_specs=out_specs + [any_space],
        out_shape=out_shape + [landing],
        scratch_shapes=[pltpu.SemaphoreType.DMA((n,)), pltpu.SemaphoreType.DMA((n,))],
        compiler_params=pltpu.CompilerParams(dimension_semantics=("arbitrary",) * len(grid), vmem_limit_bytes=vmem,
                                             has_side_effects=True))(*operands, src)


def attn_bwd(name, q, k, v, o, do, lse, *, scale, mask, tq=1024, tk=512, ride=None):
    Sq, Sk = q[0].shape[0], k[0].shape[0]
    tq, tk, nd = _tiles(Sq, Sk, mask, tq, tk)
    dq_w = q[1]
    n_q = Sq // tq

    def body(q_ref, k_ref, v_ref, o_ref, do_ref, lse_ref, *rest):
        if ride is None:
            dq_ref, dk_ref, dv_ref = rest
        else:
            src_ref, dq_ref, dk_ref, dv_ref, land_ref, send_sems, recv_sems = rest
            h, i = pl.program_id(0), pl.program_id(1)
            _exchange_in_steps(ride[0], src_ref, land_ref, send_sems, recv_sems, jnp.logical_and(h == 0, i == 0),
                               jnp.logical_and(h == HEADS - 1, i == n_q - 1))
        i = pl.program_id(1)

        @pl.when(i == 0)
        def _():
            dk_ref[...] = jnp.zeros_like(dk_ref)
            dv_ref[...] = jnp.zeros_like(dv_ref)

        qv, dov = q_ref[...], do_ref[...].astype(bf16)
        delta = jnp.sum(do_ref[...].astype(f32) * o_ref[...].astype(f32), axis=-1, keepdims=True)
        lse_v = lse_ref[:, :1]

        def block(off, dq_acc, d):
            kb, vb = k_ref[pl.ds(off, tk), :].astype(bf16), v_ref[pl.ds(off, tk), :].astype(bf16)
            s = lax.dot_general(qv, kb, _NT, preferred_element_type=f32) * scale
            if d is not None:
                s = jnp.where(_chunk_mask(tq, tk, d), s, NEG_INF)
            p = jnp.exp(s - lse_v)
            dv_ref[pl.ds(off, tk), :] += lax.dot_general(p.astype(bf16), dov, _TN, preferred_element_type=f32)
            dp = lax.dot_general(dov, vb, _NT, preferred_element_type=f32)
            ds = (p * (dp - delta) * scale).astype(bf16)
            dk_ref[pl.ds(off, tk), :] += lax.dot_general(ds, qv, _TN, preferred_element_type=f32)
            return dq_acc + jnp.dot(ds, kb, preferred_element_type=f32)

        acc = jnp.zeros((tq, dq_w), f32)
        if nd:
            acc = lax.fori_loop(0, i * nd, lambda j, c: block(pl.multiple_of(j * tk, tk), c, None), acc)
            for d in range(nd):
                acc = block(pl.multiple_of(i * tq + d * tk, tk), acc, d)
        else:
            acc = block(0, acc, None)
        dq_ref[...] = acc.astype(dq_ref.dtype)

    hd = _qblk_spec(tq, HEAD_DIM, 0, 1)
    return _ride_call(
        body, name, (HEADS, n_q),
        [_qblk_spec(tq, *q[1:]), _head_spec(Sk, *k[1:]), _head_spec(Sk, *v[1:]), hd, hd, hd],
        [_qblk_spec(tq, dq_w, 0, 1), _head_spec(Sk, dq_w, 0, 1), _head_spec(Sk, HEAD_DIM, 0, 1)],
        [jax.ShapeDtypeStruct((Sq, HEADS * dq_w), f32), jax.ShapeDtypeStruct((Sk, HEADS * dq_w), f32),
         jax.ShapeDtypeStruct((Sk, HEADS * HEAD_DIM), f32)],
        (q[0], k[0], v[0], o, do, lse), ride, 56 * 1024 * 1024)


SB_LOG_ZERO = -104.0


def _tri(B, rel):
    r = lax.broadcasted_iota(jnp.int32, (B, B), 0)
    c = lax.broadcasted_iota(jnp.int32, (B, B), 1)
    return rel(r, c).astype(bf16)


def _sb_scores(qv, kb, scale):
    z = lax.dot_general(qv, kb, _NT, preferred_element_type=f32) * scale
    e = jnp.exp(-jnp.abs(z))
    log_keep = -(jnp.maximum(z, 0.0) + jnp.log(1.0 + e))
    return z, e, log_keep


def _split_dot(x, m):
    hi = x.astype(bf16)
    lo = (x - hi.astype(f32)).astype(bf16)
    return jnp.dot(hi, m, preferred_element_type=f32) + jnp.dot(lo, m, preferred_element_type=f32)


def sb_fwd(name, q, k, v, *, tq=512, tk=256):
    S = q[0].shape[0]
    tq, tk, nd = _tiles(S, S, "strict", tq, tk)
    scale = HEAD_DIM ** -0.5
    m_ex = _tri(tk, lambda j, s: j > s)

    def body(q_ref, k_ref, v_ref, mex_ref, o_ref, tot_ref, cnt_ref):
        i = pl.program_id(1)
        qv, mex = q_ref[...], mex_ref[...]

        def block(off, carry, d):
            later, acc = carry
            kb, vb = k_ref[pl.ds(off, tk), :].astype(bf16), v_ref[pl.ds(off, tk), :].astype(bf16)
            z, _, lk = _sb_scores(qv, kb, scale)
            if d is not None:
                lk = jnp.where(_strict_mask(tq, tk, d), lk, 0.0)
            a = jnp.exp(z + lk + _split_dot(lk, mex) + later)
            if d is not None:
                a = jnp.where(_strict_mask(tq, tk, d), a, 0.0)
            acc = acc + jnp.dot(a.astype(bf16), vb, preferred_element_type=f32)
            return later + jnp.sum(lk, axis=-1, keepdims=True), acc

        carry = (jnp.zeros((tq, 1), f32), jnp.zeros((tq, HEAD_DIM), f32))
        for d in reversed(range(nd)):
            carry = block(pl.multiple_of(i * tq + d * tk, tk), carry, d)
        n_full = i * nd

        def more(state):
            t, later, _ = state
            return jnp.logical_and(t < n_full, jnp.max(later) > SB_LOG_ZERO)

        def step(state):
            t, later, acc = state
            later, acc = block(pl.multiple_of((n_full - 1 - t) * tk, tk), (later, acc), None)
            return t + 1, later, acc

        done, total, acc = lax.while_loop(more, step, (jnp.int32(0),) + carry)
        o_ref[...] = acc.astype(o_ref.dtype)
        tot_ref[...] = jnp.broadcast_to(total, (tq, HEAD_DIM))
        cnt_ref[...] = jnp.full((8, HEAD_DIM), done, f32)

    hd = _qblk_spec(tq, HEAD_DIM, 0, 1)
    return pl.pallas_call(
        body, name=name, grid=(HEADS, S // tq),
        in_specs=[_qblk_spec(tq, *q[1:]), _head_spec(S, *k[1:]), _head_spec(S, *v[1:]),
                  pl.BlockSpec((tk, tk), lambda h, i: (0, 0))],
        out_specs=[hd, hd, _qblk_spec(8, HEAD_DIM, 0, 1)],
        out_shape=[jax.ShapeDtypeStruct((S, HEADS * HEAD_DIM), f32), jax.ShapeDtypeStruct((S, HEADS * HEAD_DIM), f32),
                   jax.ShapeDtypeStruct((8 * (S // tq), HEADS * HEAD_DIM), f32)],
        compiler_params=_params(("parallel", "arbitrary")))(q[0], k[0], v[0], m_ex)


def sb_bwd(name, q, k, v, tot, cnt, do, *, tq=512, tk=256, ride=None):
    S = q[0].shape[0]
    tq, tk, nd = _tiles(S, S, "strict", tq, tk)
    scale = HEAD_DIM ** -0.5
    m_le, m_lt = _tri(tk, lambda j, s: j <= s), _tri(tk, lambda j, s: j < s)

    n_q = S // tq

    def body(q_ref, k_ref, v_ref, tot_ref, cnt_ref, do_ref, mle_ref, mlt_ref, *rest):
        if ride is None:
            dq_ref, dk_ref, dv_ref = rest
        else:
            src_ref, dq_ref, dk_ref, dv_ref, land_ref, send_sems, recv_sems = rest
            h, i = pl.program_id(0), pl.program_id(1)
            _exchange_in_steps(ride[0], src_ref, land_ref, send_sems, recv_sems, jnp.logical_and(h == 0, i == 0),
                               jnp.logical_and(h == HEADS - 1, i == n_q - 1))
        i = pl.program_id(1)

        @pl.when(i == 0)
        def _():
            dk_ref[...] = jnp.zeros_like(dk_ref)
            dv_ref[...] = jnp.zeros_like(dv_ref)

        qv, dov, mle, mlt = q_ref[...], do_ref[...].astype(bf16), mle_ref[...], mlt_ref[...]
        total = tot_ref[:, :1]

        def block(off, carry, d):
            before, g_before, dq_acc = carry
            kb, vb = k_ref[pl.ds(off, tk), :].astype(bf16), v_ref[pl.ds(off, tk), :].astype(bf16)
            z, e, lk = _sb_scores(qv, kb, scale)
            sig = jnp.where(z >= 0, 1.0, e) / (1.0 + e)
            if d is not None:
                lk = jnp.where(_strict_mask(tq, tk, d), lk, 0.0)
            later = (total - before) - _split_dot(lk, mle)
            a = jnp.exp(z + lk + later)
            if d is not None:
                a = jnp.where(_strict_mask(tq, tk, d), a, 0.0)
            g = a * lax.dot_general(dov, vb, _NT, preferred_element_type=f32)
            prefix = g_before + jnp.dot(g.astype(bf16), mlt, preferred_element_type=f32)
            dz = (g * (1.0 - sig) - prefix * sig) * scale
            if d is not None:
                dz = jnp.where(_strict_mask(tq, tk, d), dz, 0.0)
            dzb = dz.astype(bf16)
            dk_ref[pl.ds(off, tk), :] += lax.dot_general(dzb, qv, _TN, preferred_element_type=f32)
            dv_ref[pl.ds(off, tk), :] += lax.dot_general(a.astype(bf16), dov, _TN, preferred_element_type=f32)
            return (before + jnp.sum(lk, axis=-1, keepdims=True), g_before + jnp.sum(g, axis=-1, keepdims=True),
                    dq_acc + jnp.dot(dzb, kb, preferred_element_type=f32))

        zero = jnp.zeros((tq, 1), f32)
        first = i * nd - jnp.max(cnt_ref[...]).astype(jnp.int32)
        carry = lax.fori_loop(first, i * nd, lambda j, c: block(pl.multiple_of(j * tk, tk), c, None),
                              (zero, zero, jnp.zeros((tq, HEAD_DIM), f32)))
        for d in range(nd):
            carry = block(pl.multiple_of(i * tq + d * tk, tk), carry, d)
        dq_ref[...] = carry[2].astype(dq_ref.dtype)

    hd = _qblk_spec(tq, HEAD_DIM, 0, 1)
    tri = pl.BlockSpec((tk, tk), lambda h, i: (0, 0))
    return _ride_call(
        body, name, (HEADS, n_q),
        [_qblk_spec(tq, *q[1:]), _head_spec(S, *k[1:]), _head_spec(S, *v[1:]), hd, _qblk_spec(8, HEAD_DIM, 0, 1),
         hd, tri, tri],
        [hd, _head_spec(S, HEAD_DIM, 0, 1), _head_spec(S, HEAD_DIM, 0, 1)],
        [jax.ShapeDtypeStruct((S, HEADS * HEAD_DIM), f32), jax.ShapeDtypeStruct((S, HEADS * HEAD_DIM), f32),
         jax.ShapeDtypeStruct((S, HEADS * HEAD_DIM), f32)],
        (q[0], k[0], v[0], tot, cnt, do, m_le, m_lt), ride)


CONV_HALO = 32
CONV_A_BLK, CONV_G_BLK = 8, 9


def _glu(a, g):
    return a.astype(f32) * _sigmoid(g.astype(f32))


def conv_fwd(name, u, dw):
    S = u.shape[0]
    T = min(512, S)
    nT = S // T

    def body(a_ref, g_ref, ap_ref, gp_ref, dw_ref, y_ref, ext_ref):
        i = pl.program_id(0)
        prev = _glu(ap_ref[T - CONV_HALO:, :], gp_ref[T - CONV_HALO:, :])
        ext_ref[:CONV_HALO, :] = jnp.where(i > 0, prev, 0.0)
        ext_ref[CONV_HALO:, :] = _glu(a_ref[...], g_ref[...])
        acc = jnp.zeros((T, CONV_CH), f32)
        for w in range(CONV_WIDTH):
            acc = acc + dw_ref[w:w + 1, :] * ext_ref[pl.ds(w + CONV_HALO - (CONV_WIDTH - 1), T), :]
        y_ref[...] = acc

    cur = lambda blk: pl.BlockSpec((T, CONV_CH), lambda i: (i, blk))
    prv = lambda blk: pl.BlockSpec((T, CONV_CH), lambda i: (jnp.maximum(i - 1, 0), blk))
    return pl.pallas_call(
        body, name=name, grid=(nT,),
        in_specs=[cur(CONV_A_BLK), cur(CONV_G_BLK), prv(CONV_A_BLK), prv(CONV_G_BLK),
                  pl.BlockSpec(dw.shape, lambda i: (0, 0))],
        out_specs=pl.BlockSpec((T, CONV_CH), lambda i: (i, 0)),
        out_shape=jax.ShapeDtypeStruct((S, CONV_CH), f32),
        scratch_shapes=[pltpu.VMEM((T + CONV_HALO, CONV_CH), f32)],
        compiler_params=_params(("arbitrary",)))(u, u, u, u, dw)


def conv_bwd(name, u, dy, dw):
    S = u.shape[0]
    T = min(512, S)
    nT = S // T
    lead = CONV_HALO - (CONV_WIDTH - 1)

    def body(a_ref, g_ref, ap_ref, gp_ref, dy_ref, dyn_ref, dw_ref, du_ref, ddw_ref, ext_ref, dext_ref):
        i = pl.program_id(0)
        prev = _glu(ap_ref[T - CONV_HALO:, :], gp_ref[T - CONV_HALO:, :])
        ext_ref[:CONV_HALO, :] = jnp.where(i > 0, prev, 0.0)
        a, sg = a_ref[...].astype(f32), _sigmoid(g_ref[...].astype(f32))
        ext_ref[CONV_HALO:, :] = a * sg
        dyv = dy_ref[...]
        dext_ref[:T, :] = dyv
        dext_ref[T:, :] = jnp.where(i < nT - 1, dyn_ref[:CONV_HALO, :], 0.0)
        @pl.when(i == 0)
        def _():
            ddw_ref[...] = jnp.zeros_like(ddw_ref)

        dglu = jnp.zeros((T, CONV_CH), f32)
        for w in range(CONV_WIDTH):
            dglu = dglu + dw_ref[w:w + 1, :] * dext_ref[pl.ds(CONV_WIDTH - 1 - w, T), :]
            ddw_ref[w:w + 1, :] += jnp.sum(dyv * ext_ref[pl.ds(w + lead, T), :], axis=0, keepdims=True)

        du_ref[:, :CONV_CH] = (dglu * sg).astype(du_ref.dtype)
        du_ref[:, CONV_CH:] = (dglu * a * sg * (1.0 - sg)).astype(du_ref.dtype)

    cur = lambda blk: pl.BlockSpec((T, CONV_CH), lambda i: (i, blk))
    prv = lambda blk: pl.BlockSpec((T, CONV_CH), lambda i: (jnp.maximum(i - 1, 0), blk))
    return pl.pallas_call(
        body, name=name, grid=(nT,),
        in_specs=[cur(CONV_A_BLK), cur(CONV_G_BLK), prv(CONV_A_BLK), prv(CONV_G_BLK),
                  pl.BlockSpec((T, CONV_CH), lambda i: (i, 0)),
                  pl.BlockSpec((T, CONV_CH), lambda i: (jnp.minimum(i + 1, nT - 1), 0)),
                  pl.BlockSpec(dw.shape, lambda i: (0, 0))],
        out_specs=[pl.BlockSpec((T, 2 * CONV_CH), lambda i: (i, 0)), pl.BlockSpec(dw.shape, lambda i: (0, 0))],
        out_shape=[jax.ShapeDtypeStruct((S, 2 * CONV_CH), bf16), jax.ShapeDtypeStruct(dw.shape, f32)],
        scratch_shapes=[pltpu.VMEM((T + CONV_HALO, CONV_CH), f32), pltpu.VMEM((T + CONV_HALO, CONV_CH), f32)],
        compiler_params=_params(("arbitrary",)))(u, u, u, u, dy, dy, dw)


def rope_tables(pos_col):
    S = pos_col.shape[0]
    T = min(512, S)
    inv_freq = ROPE_BASE ** (-jnp.arange(0, MLA_ROPE, 2, dtype=f32) / MLA_ROPE)
    zeros = jnp.zeros((MLA_NOPE,), f32)
    inv_row = jnp.concatenate([zeros, inv_freq, inv_freq, zeros[:MLA_PAD - MLA_QK]]).reshape(1, MLA_PAD)

    def body(p_ref, f_ref, c_ref, s_ref):
        lane = lax.broadcasted_iota(jnp.int32, (T, MLA_PAD), 1)
        ang = p_ref[...].astype(f32) * f_ref[...]
        rot = jnp.logical_and(lane >= MLA_NOPE, lane < MLA_QK)
        c_ref[...] = jnp.where(rot, jnp.cos(ang), jnp.where(lane < MLA_NOPE, 1.0, 0.0))
        s_ref[...] = jnp.where(rot, jnp.sin(ang), 0.0)

    spec = pl.BlockSpec((T, MLA_PAD), lambda i: (i, 0))
    return pl.pallas_call(
        body, name="rope_tables", grid=(S // T,),
        in_specs=[pl.BlockSpec((T, 1), lambda i: (i, 0)), pl.BlockSpec((1, MLA_PAD), lambda i: (0, 0))],
        out_specs=[spec, spec], out_shape=[jax.ShapeDtypeStruct((S, MLA_PAD), f32)] * 2,
        compiler_params=_params(("parallel",)))(pos_col, inv_row)


def loss_head(y, target):
    S, D = y.shape
    T = min(512, S)

    def body(y_ref, t_ref, dy_ref, l_ref):
        i = pl.program_id(0)
        err = y_ref[...] - t_ref[...]
        dy_ref[...] = err * (1.0 / D)
        part = 0.5 * jnp.sum(jnp.sum(err * err, axis=-1, keepdims=True) * (1.0 / D), axis=0, keepdims=True)
        part = jnp.broadcast_to(part, l_ref.shape)

        @pl.when(i == 0)
        def _():
            l_ref[...] = part

        @pl.when(i > 0)
        def _():
            l_ref[...] += part

    spec = pl.BlockSpec((T, D), lambda i: (i, 0))
    return pl.pallas_call(
        body, name="loss_head", grid=(S // T,), in_specs=[spec, spec],
        out_specs=[spec, pl.BlockSpec((8, 128), lambda i: (0, 0))],
        out_shape=[jax.ShapeDtypeStruct((S, D), f32), jax.ShapeDtypeStruct((8, 128), f32)],
        compiler_params=_params(("arbitrary",)))(y, target)


def _row(a, w=None, off=0, st=0):
    return (a, w or a.shape[1], off, st)


def _out(tw, dt, w=None, off=0, st=0):
    return (tw, dt, w or tw, off, st)


ACT = bf16


def ffn_fwd(tag, x, g, w_in, w_out):
    (h,) = rowwise(f"{tag}_rms", fn_rms, [_row(x)], [g], [_out(D_MODEL, bf16)])
    u = mm(f"{tag}_in", h, w_in, "nn", out_dtype=ACT)
    (a,) = rowwise(f"{tag}_swiglu", fn_swiglu, [_row(u, FFN_HIDDEN, 0), _row(u, FFN_HIDDEN, 1)], [],
                   [_out(FFN_HIDDEN, bf16)])
    y = mm(f"{tag}_out", a, w_out, "nn", alpha=0.5, res=x)
    return y, (x, h, u, a)


def ffn_bwd(tag, saved, g, w_in, w_out, dy):
    x, h, u, a = saved
    d_w_out = mm(f"{tag}_dwout", a, dy, "tn", alpha=0.5)
    da = mm(f"{tag}_da", dy, w_out, "nt", alpha=0.5, out_dtype=ACT)
    (du,), _ = rowwise_bwd(f"{tag}_dswiglu", fn_swiglu, [_row(u, FFN_HIDDEN, 0), _row(u, FFN_HIDDEN, 1)], [],
                           [_row(da)], [_out(2 * FFN_HIDDEN, bf16)], merge=[(0, 1)], T=256)
    d_w_in = mm(f"{tag}_dwin", h, du, "tn")
    dh = mm(f"{tag}_dh", du, w_in, "nt", out_dtype=ACT)
    (dx,), (dg,) = rowwise_bwd(f"{tag}_drms", fn_rms_res, [_row(x)], [g], [_row(dy), _row(dh)], [_out(D_MODEL, f32)])
    return dx, dg, d_w_in, d_w_out


def _seg(name):
    for n, _, w, start in _U_SEGS:
        if n == name:
            return start, w
    raise KeyError(name)


def mix_fwd(tag, x, mem_n_in, tabs, p, ride=None):
    cos_t, sin_t = tabs
    (h,) = rowwise(f"{tag}_rms", fn_rms, [_row(x)], [p["mix_norm"]], [_out(D_MODEL, bf16)])
    u = mm(f"{tag}_in", h, p["w_in"], "nn", out_dtype=ACT)
    yc = conv_fwd(f"{tag}_conv", u, p["conv_dw"])
    (br_a,) = rowwise(f"{tag}_lnsilu", fn_ln_silu, [_row(yc)], [p["conv_b"], p["conv_ln_g"], p["conv_ln_b"]],
                      [_out(BRANCH_WIDTH, bf16)])
    sb0 = _seg("sb")[0] // HEAD_DIM
    qs, ks = rowwise(f"{tag}_sbprep", fn_sb_prep, [_row(u, HEAD_DIM, sb0, 1), _row(u, HEAD_DIM, sb0 + HEADS, 1)],
                     [p["sb_q_hnorm"], p["sb_k_hnorm"]],
                     [_out(BRANCH_WIDTH, bf16, HEAD_DIM, 0, 1), _out(BRANCH_WIDTH, bf16, HEAD_DIM, 0, 1)], H=HEADS)
    sb_v = _row(u, HEAD_DIM, sb0 + 2 * HEADS, 1)
    br_b, tot_b, cnt_b = sb_fwd(f"{tag}_sb", _row(qs, HEAD_DIM, 0, 1), _row(ks, HEAD_DIM, 0, 1), sb_v)
    ql_n, kvl_n = rowwise(f"{tag}_latrms", lambda a, b, ga, gb: (_rms(a, ga), _rms(b, gb)),
                          [_row(u, MLA_Q_LORA, _seg("qlat")[0] // MLA_Q_LORA), _row(u, MLA_KV_LORA, _seg("kvlat")[0] // MLA_KV_LORA)],
                          [p["mla_q_norm"], p["mla_kv_norm"]], [_out(MLA_Q_LORA, bf16), _out(MLA_KV_LORA, bf16)])
    qfull = mm(f"{tag}_uq", ql_n, p["mla_w_uq"], "nn", out_dtype=ACT)
    kvfull = mm(f"{tag}_ukv", kvl_n, p["mla_w_ukv"], "nn", out_dtype=ACT)
    kr_row = _row(u, HEAD_DIM, _seg("krope")[0] // HEAD_DIM, 0)
    (qr,) = rowwise(f"{tag}_mlaq", fn_mla_q, [_row(qfull, MLA_PAD, 0, 1), _row(cos_t), _row(sin_t)], [p["mla_q_hnorm"]],
                    [_out(HEADS * MLA_PAD, bf16, MLA_PAD, 0, 1)], H=HEADS)
    (kr,) = rowwise(f"{tag}_mlak", fn_mla_k, [_row(kvfull, HEAD_DIM, 0, 2), kr_row, _row(cos_t), _row(sin_t)],
                    [p["mla_k_hnorm"]], [_out(HEADS * MLA_PAD, bf16, MLA_PAD, 0, 1)], H=HEADS)
    mla_v = _row(kvfull, HEAD_DIM, 1, 2)
    br_c, lse_c, *landed = attn_fwd(f"{tag}_mla", _row(qr, MLA_PAD, 0, 1), _row(kr, MLA_PAD, 0, 1), mla_v,
                                    scale=MLA_QK ** -0.5, mask="chunk", ride=ride)
    (mem_n,) = rowwise(f"{tag}_memrms", fn_rms, [_row(mem_n_in)], [p["mem_norm"]], [_out(D_MODEL, bf16)])
    kvm = mm(f"{tag}_memkv", mem_n, p["mem_w_kv"], "nn", out_dtype=ACT)
    (km,) = rowwise(f"{tag}_memk", fn_rms, [_row(kvm, HEAD_DIM, 0, 1)], [p["mem_k_hnorm"]],
                    [_out(BRANCH_WIDTH, bf16, HEAD_DIM, 0, 1)], H=HEADS)
    mq0 = _seg("memq")[0] // HEAD_DIM
    (qm,) = rowwise(f"{tag}_memq", fn_rms, [_row(u, HEAD_DIM, mq0, 1)], [p["mem_q_hnorm"]],
                    [_out(BRANCH_WIDTH, bf16, HEAD_DIM, 0, 1)], H=HEADS)
    mem_v = _row(kvm, HEAD_DIM, HEADS, 1)
    br_d, lse_d = attn_fwd(f"{tag}_memattn", _row(qm, HEAD_DIM, 0, 1), _row(km, HEAD_DIM, 0, 1), mem_v,
                           scale=HEAD_DIM ** -0.5, mask=None)
    branches = (br_a, br_b, br_c, br_d)
    proj = [mm(f"{tag}_branch{b}", branches[b], p["w_branch"][b], "nn", out_dtype=ACT) for b in range(N_BRANCH)]
    gate_rows = [_row(u, D_MODEL, b) for b in range(N_BRANCH)]
    (merged,) = rowwise(f"{tag}_merge", fn_merge, gate_rows + [_row(t) for t in proj], [], [_out(D_MODEL, bf16)], T=256)
    y = mm(f"{tag}_out", merged, p["w_out"], "nn", res=x)
    saved = dict(x=x, h=h, u=u, yc=yc, qs=qs, ks=ks, ql_n=ql_n, kvl_n=kvl_n, qfull=qfull, kvfull=kvfull, qr=qr, kr=kr,
                 lse_c=lse_c, mem_n=mem_n, kvm=kvm, km=km, qm=qm, lse_d=lse_d, branches=branches, proj=proj,
                 merged=merged, tot_b=tot_b, cnt_b=cnt_b)
    return y, saved, (landed[0] if landed else None)


def mix_bwd(tag, sv, mem_n_in, tabs, p, dy, reducer=None):
    cos_t, sin_t = tabs
    u, S = sv["u"], sv["u"].shape[0]
    g = {}
    g["w_out"] = mm(f"{tag}_dwout", sv["merged"], dy, "tn")
    dmerged = mm(f"{tag}_dmerged", dy, p["w_out"], "nt", out_dtype=ACT)
    gate_rows = [_row(u, D_MODEL, b) for b in range(N_BRANCH)]
    d_merge, _ = rowwise_bwd(f"{tag}_dmerge", fn_merge, gate_rows + [_row(t) for t in sv["proj"]], [], [_row(dmerged)],
                             [_out(N_BRANCH * D_MODEL, bf16)] + [_out(D_MODEL, bf16)] * N_BRANCH, T=256,
                             merge=[tuple(range(N_BRANCH))] + [(N_BRANCH + b,) for b in range(N_BRANCH)])
    d_gates, d_proj = d_merge[:1], d_merge[1:]
    g["w_branch"] = [mm(f"{tag}_dwbranch{b}", sv["branches"][b], d_proj[b], "tn") for b in range(N_BRANCH)]
    d_br = [mm(f"{tag}_dbranch{b}", d_proj[b], p["w_branch"][b], "nt", out_dtype=ACT) for b in range(N_BRANCH)]
    (dyc,), (g["conv_b"], g["conv_ln_g"], g["conv_ln_b"]) = rowwise_bwd(
        f"{tag}_dlnsilu", fn_ln_silu, [_row(sv["yc"])], [p["conv_b"], p["conv_ln_g"], p["conv_ln_b"]], [_row(d_br[0])],
        [_out(BRANCH_WIDTH, f32)])
    du_conv, g["conv_dw"] = conv_bwd(f"{tag}_dconv", u, dyc, p["conv_dw"])
    sb0 = _seg("sb")[0] // HEAD_DIM
    sb_v = _row(u, HEAD_DIM, sb0 + 2 * HEADS, 1)
    dqs, dks, dv_sb, *landed = sb_bwd(f"{tag}_dsb", _row(sv["qs"], HEAD_DIM, 0, 1), _row(sv["ks"], HEAD_DIM, 0, 1), sb_v,
                                      sv["tot_b"], sv["cnt_b"], d_br[1],
                                      ride=("sibling", reducer.sibling_src()) if reducer else None)
    scatter_src = reducer.after_sibling(landed[0]) if reducer else None
    (du_sbq, du_sbk), (g["sb_q_hnorm"], g["sb_k_hnorm"]) = rowwise_bwd(
        f"{tag}_dsbprep", fn_sb_prep, [_row(u, HEAD_DIM, sb0, 1), _row(u, HEAD_DIM, sb0 + HEADS, 1)],
        [p["sb_q_hnorm"], p["sb_k_hnorm"]], [_row(dqs, HEAD_DIM, 0, 1), _row(dks, HEAD_DIM, 0, 1)],
        [_out(BRANCH_WIDTH, bf16, HEAD_DIM, 0, 1), _out(BRANCH_WIDTH, bf16, HEAD_DIM, 0, 1)], H=HEADS)
    mla_v = _row(sv["kvfull"], HEAD_DIM, 1, 2)
    dqr, dkr, dv_mla, *landed = attn_bwd(f"{tag}_dmla", _row(sv["qr"], MLA_PAD, 0, 1), _row(sv["kr"], MLA_PAD, 0, 1), mla_v,
                                         sv["branches"][2], d_br[2], sv["lse_c"], scale=MLA_QK ** -0.5, mask="chunk",
                                         ride=("scatter", scatter_src) if reducer else None)
    if reducer:
        reducer.after_scatter(landed[0])
    (dqfull,), (g["mla_q_hnorm"],) = rowwise_bwd(
        f"{tag}_dmlaq", fn_mla_q, [_row(sv["qfull"], MLA_PAD, 0, 1), _row(cos_t), _row(sin_t)], [p["mla_q_hnorm"]],
        [_row(dqr, MLA_PAD, 0, 1)], [_out(HEADS * MLA_PAD, bf16, MLA_PAD, 0, 1)], H=HEADS, nondiff=(1, 2))
    kr_row = _row(u, HEAD_DIM, _seg("krope")[0] // HEAD_DIM, 0)
    (dkn, du_krope, dvp), (g["mla_k_hnorm"],) = rowwise_bwd(
        f"{tag}_dmlak", fn_mla_k_v, [_row(sv["kvfull"], HEAD_DIM, 0, 2), kr_row, mla_v, _row(cos_t), _row(sin_t)],
        [p["mla_k_hnorm"]], [_row(dkr, MLA_PAD, 0, 1), _row(dv_mla, HEAD_DIM, 0, 1)],
        [_out(BRANCH_WIDTH, bf16, HEAD_DIM, 0, 1), _out(HEAD_DIM, f32), _out(BRANCH_WIDTH, bf16, HEAD_DIM, 0, 1)],
        H=HEADS, nondiff=(3, 4))
    dkvfull = _interleave(f"{tag}_dkvfull", dkn, dvp)
    g["mla_w_uq"] = mm(f"{tag}_dwuq", sv["ql_n"], dqfull, "tn")
    g["mla_w_ukv"] = mm(f"{tag}_dwukv", sv["kvl_n"], dkvfull, "tn")
    dql_n = mm(f"{tag}_dqln", dqfull, p["mla_w_uq"], "nt", out_dtype=ACT)
    dkvl_n = mm(f"{tag}_dkvln", dkvfull, p["mla_w_ukv"], "nt", out_dtype=ACT)
    (du_lat,), (g["mla_q_norm"], g["mla_kv_norm"]) = rowwise_bwd(
        f"{tag}_dlatrms", lambda a, b, ga, gb: (_rms(a, ga), _rms(b, gb)),
        [_row(u, MLA_Q_LORA, _seg("qlat")[0] // MLA_Q_LORA), _row(u, MLA_KV_LORA, _seg("kvlat")[0] // MLA_KV_LORA)],
        [p["mla_q_norm"], p["mla_kv_norm"]], [_row(dql_n), _row(dkvl_n)], [_out(MLA_Q_LORA + MLA_KV_LORA, bf16)],
        merge=[(0, 1)])
    mem_v = _row(sv["kvm"], HEAD_DIM, HEADS, 1)
    dqm, dkm, dvm = attn_bwd(f"{tag}_dmemattn", _row(sv["qm"], HEAD_DIM, 0, 1), _row(sv["km"], HEAD_DIM, 0, 1), mem_v,
                             sv["branches"][3], d_br[3], sv["lse_d"], scale=HEAD_DIM ** -0.5, mask=None)
    mq0 = _seg("memq")[0] // HEAD_DIM
    (du_memq,), (g["mem_q_hnorm"],) = rowwise_bwd(
        f"{tag}_dmemq", fn_rms, [_row(u, HEAD_DIM, mq0, 1)], [p["mem_q_hnorm"]], [_row(dqm, HEAD_DIM, 0, 1)],
        [_out(BRANCH_WIDTH, bf16, HEAD_DIM, 0, 1)], H=HEADS)
    (dkvm_k, dkvm_v), (g["mem_k_hnorm"],) = rowwise_bwd(
        f"{tag}_dmemk", fn_mem_k_v, [_row(sv["kvm"], HEAD_DIM, 0, 1), mem_v], [p["mem_k_hnorm"]],
        [_row(dkm, HEAD_DIM, 0, 1), _row(dvm, HEAD_DIM, 0, 1)],
        [_out(BRANCH_WIDTH, bf16, HEAD_DIM, 0, 1), _out(BRANCH_WIDTH, bf16, HEAD_DIM, 0, 1)], H=HEADS)
    dkvm = jnp.concatenate([dkvm_k, dkvm_v], axis=1)
    g["mem_w_kv"] = mm(f"{tag}_dwmemkv", sv["mem_n"], dkvm, "tn")
    dmem_n = mm(f"{tag}_dmemn", dkvm, p["mem_w_kv"], "nt", out_dtype=ACT)
    _, (g["mem_norm"],) = rowwise_bwd(f"{tag}_dmemrms", fn_rms, [_row(mem_n_in)], [p["mem_norm"]], [_row(dmem_n)],
                                      [_out(D_MODEL, bf16)])
    du_krope_b = du_krope.astype(bf16)
    du = jnp.concatenate(list(d_gates) + [du_conv, du_sbq, du_sbk, dv_sb.astype(bf16), du_lat, du_memq,
                                          du_krope_b, jnp.zeros((S, U_WIDTH - _seg("krope")[0] - HEAD_DIM), bf16)], axis=1)
    g["w_in"] = mm(f"{tag}_dwin", sv["h"], du, "tn")
    dh = mm(f"{tag}_dh", du, p["w_in"], "nt", out_dtype=ACT)
    (dx,), (g["mix_norm"],) = rowwise_bwd(f"{tag}_drms", fn_rms_res, [_row(sv["x"])], [p["mix_norm"]],
                                          [_row(dy), _row(dh)], [_out(D_MODEL, f32)])
    return dx, g


def _interleave(name, a, b):
    S, W = a.shape
    T = min(512, S)

    def body(a_ref, b_ref, o_ref):
        o_ref[:, :HEAD_DIM] = a_ref[...]
        o_ref[:, HEAD_DIM:] = b_ref[...]

    blk = pl.BlockSpec((T, HEAD_DIM), lambda i, h: (i, h))
    return pl.pallas_call(
        body, name=name, grid=(S // T, W // HEAD_DIM), in_specs=[blk, blk],
        out_specs=pl.BlockSpec((T, 2 * HEAD_DIM), lambda i, h: (i, h)),
        out_shape=jax.ShapeDtypeStruct((S, 2 * W), a.dtype), compiler_params=_params(("parallel", "parallel")))(a, b)


def _u_layout(w):
    parts, at = [], 0
    for _, src, width, start in _U_SEGS:
        assert start == at
        parts.append(w[..., src:src + width])
        at += width
    parts.append(jnp.zeros(w.shape[:-1] + (U_WIDTH - at,), w.dtype))
    return jnp.concatenate(parts, axis=-1)


def _u_layout_inv(g):
    order = sorted(_U_SEGS, key=lambda s: s[1])
    return jnp.concatenate([g[..., start:start + width] for _, _, width, start in order], axis=-1)


def _pad_heads(w, n=MLA_QK, to=MLA_PAD):
    w = w.reshape(w.shape[:-1] + (HEADS, n))
    w = jnp.pad(w, [(0, 0)] * (w.ndim - 1) + [(0, to - n)])
    return w.reshape(w.shape[:-2] + (HEADS * to,))


def _unpad_heads(g, n=MLA_QK, to=MLA_PAD):
    g = g.reshape(g.shape[:-1] + (HEADS, to))[..., :n]
    return g.reshape(g.shape[:-2] + (HEADS * n,))


def layer_params(W):
    row = lambda name: W[name].reshape(1, -1).astype(f32)
    p = {n: row(n) for n in SMALL if n != "mla_q_hnorm" and n != "mla_k_hnorm"}
    for n in ("mla_q_hnorm", "mla_k_hnorm"):
        p[n] = jnp.pad(row(n), ((0, 0), (0, MLA_PAD - MLA_QK)))
    for n in ("ffn1_w_in", "ffn1_w_out", "ffn2_w_in", "ffn2_w_out", "mla_w_ukv", "mem_w_kv", "w_out"):
        p[n] = W[n]
    p["w_branch"] = [W["w_branch"][b] for b in range(N_BRANCH)]
    p["w_in"] = _u_layout(W["w_in"])
    p["mla_w_uq"] = _pad_heads(W["mla_w_uq"])
    p["conv_dw"] = jnp.pad(W["conv_dw"].astype(f32), ((0, 1), (0, 0)))
    return p


def layer_grads_to_original(g):
    out = dict(g)
    out["w_in"] = _u_layout_inv(g["w_in"])
    out["mla_w_uq"] = _unpad_heads(g["mla_w_uq"])
    out["conv_dw"] = g["conv_dw"][:CONV_WIDTH]
    out["w_branch"] = jnp.stack(g["w_branch"])
    for n in ("mla_q_hnorm", "mla_k_hnorm"):
        out[n] = g[n][:, :MLA_QK]
    return {n: (out[n].reshape(-1) if n in SMALL else out[n]) for n in out}


def local_step(x, mem, pos_col, target, weights_of, ride_of=lambda l: None, deliver=lambda l, landed: None,
               reducer_of=lambda l, grads: None):
    tabs = rope_tables(pos_col)
    params, saved = [], []
    for l in range(DEPTH):
        p = layer_params(weights_of(l))
        params.append(p)
        x, s1 = ffn_fwd(f"l{l}_ffn1", x, p["ffn1_norm"], p["ffn1_w_in"], p["ffn1_w_out"])
        x, s2, landed = mix_fwd(f"l{l}_mix", x, mem, tabs, p, ride=ride_of(l))
        deliver(l, landed)
        x, s3 = ffn_fwd(f"l{l}_ffn2", x, p["ffn2_norm"], p["ffn2_w_in"], p["ffn2_w_out"])
        saved.append((s1, s2, s3))
    dx, loss_blk = loss_head(x, target)
    grads = [None] * DEPTH
    reducer = None
    for l in reversed(range(DEPTH)):
        p, (s1, s2, s3) = params[l], saved[l]
        dx, g_n2, g_in2, g_out2 = ffn_bwd(f"l{l}_ffn2", s3, p["ffn2_norm"], p["ffn2_w_in"], p["ffn2_w_out"], dx)
        dx, g = mix_bwd(f"l{l}_mix", s2, mem, tabs, p, dx, reducer)
        dx, g_n1, g_in1, g_out1 = ffn_bwd(f"l{l}_ffn1", s1, p["ffn1_norm"], p["ffn1_w_in"], p["ffn1_w_out"], dx)
        g.update(ffn1_norm=g_n1, ffn1_w_in=g_in1, ffn1_w_out=g_out1, ffn2_norm=g_n2, ffn2_w_in=g_in2, ffn2_w_out=g_out2)
        grads[l] = layer_grads_to_original(g)
        reducer = reducer_of(l, grads[l])
    if reducer:
        reducer.alone()
    return loss_blk, dx, grads


_ANY = pl.BlockSpec(memory_space=pl.ANY)
_COMM = pltpu.CompilerParams(has_side_effects=True)


def _coords():
    return lax.axis_index("x"), lax.axis_index("y"), lax.axis_index("c")


def chip_exchange(name, src, scatter):
    shape = src.shape[1:] if scatter else src.shape

    def body(src_ref, out_ref, send_sems, recv_sems):
        x, y, c = _coords()
        copies = []
        for k, (px, py) in enumerate([(1 - x, y), (x, 1 - y), (1 - x, 1 - y)]):
            piece = src_ref.at[2 * px + py] if scatter else src_ref
            cp = pltpu.make_async_remote_copy(piece, out_ref.at[k], send_sems.at[k], recv_sems.at[k],
                                              device_id=(px, py, c), device_id_type=MESH)
            cp.start()
            copies.append(cp)
        for cp in copies:
            cp.wait()

    return pl.pallas_call(
        body, name=name, in_specs=[_ANY], out_specs=_ANY, out_shape=jax.ShapeDtypeStruct((3,) + shape, src.dtype),
        scratch_shapes=[pltpu.SemaphoreType.DMA((3,)), pltpu.SemaphoreType.DMA((3,))], compiler_params=_COMM)(src)


def sibling_exchange(name, src):
    def body(src_ref, out_ref, send_sem, recv_sem):
        x, y, c = _coords()
        cp = pltpu.make_async_remote_copy(src_ref, out_ref, send_sem, recv_sem, device_id=(x, y, 1 - c),
                                          device_id_type=MESH)
        cp.start()
        cp.wait()

    return pl.pallas_call(
        body, name=name, in_specs=[_ANY], out_specs=_ANY, out_shape=jax.ShapeDtypeStruct(src.shape, src.dtype),
        scratch_shapes=[pltpu.SemaphoreType.DMA, pltpu.SemaphoreType.DMA], compiler_params=_COMM)(src)


def all8_gather(name, src):
    def body(src_ref, out_ref, send_sems, recv_sems, local_sem):
        x, y, c = _coords()
        me = 4 * x + 2 * y + c
        mine = pltpu.make_async_copy(src_ref, out_ref.at[me], local_sem)
        mine.start()
        copies = []
        for k in range(1, 8):
            peer = (1 - x if k & 4 else x, 1 - y if k & 2 else y, 1 - c if k & 1 else c)
            cp = pltpu.make_async_remote_copy(src_ref, out_ref.at[me], send_sems.at[k - 1], recv_sems.at[k - 1],
                                              device_id=peer, device_id_type=MESH)
            cp.start()
            copies.append(cp)
        for cp in copies:
            cp.wait()
        mine.wait()

    return pl.pallas_call(
        body, name=name, in_specs=[_ANY], out_specs=_ANY, out_shape=jax.ShapeDtypeStruct((8,) + src.shape, src.dtype),
        scratch_shapes=[pltpu.SemaphoreType.DMA((7,)), pltpu.SemaphoreType.DMA((7,)), pltpu.SemaphoreType.DMA],
        compiler_params=_COMM)(src)


def sum8(name, g):
    def body(g_ref, o_ref):
        acc = g_ref[0]
        for k in range(1, 8):
            acc = acc + g_ref[k]
        o_ref[...] = acc

    return pl.pallas_call(body, name=name, out_shape=jax.ShapeDtypeStruct(g.shape[1:], g.dtype))(g)


PACK_COLS = 1024
PACKED = tuple(n for n in SHARDED if n != "conv_dw")
PACK_ROW_GRAIN = PACK_GRAIN // PACK_COLS


def _rows(shape):
    n = math.prod(shape)
    assert n % (16 * PACK_COLS) == 0, shape
    return n // PACK_COLS


def _pack_rows(pieces):
    rows = sum(p.shape[0] for p in pieces)
    pad = -rows % PACK_ROW_GRAIN
    if pad:
        pieces = pieces + [jnp.zeros((pad, PACK_COLS), pieces[0].dtype)]
    return jnp.concatenate(pieces, axis=0)


def _unpack_rows(packed, shard_shapes):
    out, at = {}, 0
    for n in PACKED:
        r = _rows(shard_shapes[n])
        out[n] = packed[at:at + r].reshape(shard_shapes[n])
        at += r
    return out


def gather_shards(name, own):
    n_chunks = 4
    rows = own.shape[1] // n_chunks
    assert own.shape[1] % (16 * n_chunks) == 0

    def body(own_ref, out_ref, send_sems, recv_sems, local_sem):
        x, y, c = _coords()
        sib = (x, y, 1 - c)
        mine = pltpu.make_async_copy(own_ref, out_ref.at[0], local_sem)
        mine.start()
        chips = [(2, (1 - x, y)), (1, (x, 1 - y)), (3, (1 - x, 1 - y))]

        def copy(k, j, src, m, half, to):
            sl = pl.ds(j * rows, rows)
            return pltpu.make_async_remote_copy(src.at[sl], out_ref.at[m, half, sl], send_sems.at[k * n_chunks + j],
                                                recv_sems.at[k * n_chunks + j], device_id=to, device_id_type=MESH)

        first = [[copy(k, j, own_ref.at[c], m, c, (px, py, c)) for j in range(n_chunks)]
                 for k, (m, (px, py)) in enumerate(chips)]
        for j in range(n_chunks):
            for k in range(3):
                first[k][j].start()
        passed = []
        for j in range(n_chunks):
            for k, (m, _) in enumerate(chips):
                first[k][j].wait_recv()
                cp = copy(3 + k, j, out_ref.at[m, c], m, c, sib)
                cp.start()
                passed.append(cp)
        for j in range(n_chunks):
            for k, (m, _) in enumerate(chips):
                copy(3 + k, j, out_ref.at[m, 1 - c], m, 1 - c, sib).wait_recv()
        for cp in [cp for per_chip in first for cp in per_chip] + passed:
            cp.wait_send()
        mine.wait()

    return pl.pallas_call(
        body, name=name, in_specs=[_ANY], out_specs=_ANY, out_shape=jax.ShapeDtypeStruct((4,) + own.shape, own.dtype),
        scratch_shapes=[pltpu.SemaphoreType.DMA((6 * n_chunks,)), pltpu.SemaphoreType.DMA((6 * n_chunks,)),
                        pltpu.SemaphoreType.DMA],
        compiler_params=_COMM)(own)


def pack_layer_shard(w, l):
    return _pack_rows([w[n][l].astype(bf16).reshape(-1, PACK_COLS) for n in PACKED]).reshape(2, -1, PACK_COLS)


def unpack_layer(by_mask, w):
    x, y, _ = _coords()
    me = 2 * x + y
    shapes = {n: w[n].shape[1:] for n in PACKED}
    pieces = [_unpack_rows(lax.dynamic_index_in_dim(by_mask, jnp.bitwise_xor(s, me), axis=0, keepdims=False)
                           .reshape(-1, PACK_COLS), shapes) for s in range(4)]
    return {n: jnp.concatenate([pieces[s][n] for s in range(4)], axis=SHARD_AXIS[n] - 1) for n in PACKED}


def gather_conv_dw(dw):
    rows = math.prod(dw.shape[:-1])
    every = all8_gather("ag_conv_dw", jnp.pad(dw.reshape(rows, -1), ((0, -rows % 8), (0, 0))))
    return jnp.concatenate([every[2 * s, :rows].reshape(dw.shape) for s in range(4)], axis=SHARD_AXIS["conv_dw"])


def _add_streams(name, ins, selectors, out_dtypes, rows, T=256):
    n_streams = max([a.shape[sel.index("s")] for a, sel in zip(ins, selectors) if "s" in sel] + [1])

    def spec(sel):
        def index(s, i, pf):
            lead = tuple(s if e == "s" else (pf[e[1]] if isinstance(e, tuple) else e) for e in sel)
            return lead + (i, 0)
        return pl.BlockSpec((None,) * len(sel) + (T, PACK_COLS), index)

    def body(pf_ref, *refs):
        acc = refs[0][...].astype(f32)
        for r in refs[1:len(ins)]:
            acc = acc + r[...].astype(f32)
        for o in refs[len(ins):]:
            o[...] = acc.astype(o.dtype)

    def run(pf):
        grid_spec = pltpu.PrefetchScalarGridSpec(
            num_scalar_prefetch=1, grid=(n_streams, rows // T), in_specs=[spec(sel) for sel in selectors],
            out_specs=[spec(("s",)) for _ in out_dtypes])
        return pl.pallas_call(
            body, name=name, grid_spec=grid_spec,
            out_shape=[jax.ShapeDtypeStruct((n_streams, rows, PACK_COLS), dt) for dt in out_dtypes],
            compiler_params=_params(("parallel", "parallel")))(pf, *ins)
    return run


class LayerReduce:
    def __init__(self, tag, grads):
        x, y, c = _coords()
        self.tag, self.c = tag, c
        streams = []
        for s in range(4):
            pieces = []
            for n in PACKED:
                ax = SHARD_AXIS[n] - 1
                width = grads[n].shape[ax] // 4
                pieces.append(lax.slice_in_dim(grads[n], s * width, (s + 1) * width, axis=ax).reshape(-1, PACK_COLS))
            streams.append(_pack_rows(pieces))
        self.rows = streams[0].shape[0] // 2
        self.G = jnp.concatenate(streams, axis=0).reshape(4, 2, self.rows, PACK_COLS)
        self.pf = jnp.stack([c, 2 * x + y]).astype(jnp.int32)

    def sibling_src(self):
        return lax.dynamic_index_in_dim(self.G, 1 - self.c, axis=1, keepdims=False).astype(bf16)

    def after_sibling(self, from_sib):
        self.chip_sum, chip_sum_b = _add_streams(f"{self.tag}_add_sibling", [self.G, from_sib],
                                                 [("s", ("pf", 0)), ("s",)], [f32, bf16], self.rows)(self.pf)
        return chip_sum_b

    def after_scatter(self, got):
        (half,) = _add_streams(f"{self.tag}_add_chips", [self.chip_sum, got, got, got],
                               [(("pf", 1),), (0,), (1,), (2,)], [f32], self.rows)(self.pf)
        self.half = half[0]

    def alone(self):
        chip_sum_b = self.after_sibling(sibling_exchange(f"{self.tag}_sibling", self.sibling_src()))
        self.after_scatter(chip_exchange(f"{self.tag}_chips", chip_sum_b, scatter=True))


def finish_reduce(reducers, w):
    c = reducers[0].c
    mine = jnp.concatenate([r.half for r in reducers], axis=0)
    other = sibling_exchange("rs_final", mine)
    lower, upper = jnp.where(c == 0, mine, other), jnp.where(c == 0, other, mine)
    shapes = {n: w[n].shape[1:] for n in PACKED}
    per_layer, at = [], 0
    for r in reducers:
        shard = jnp.concatenate([lower[at:at + r.rows], upper[at:at + r.rows]], axis=0)
        per_layer.append(_unpack_rows(shard, shapes))
        at += r.rows
    return {n: jnp.stack([p[n] for p in per_layer]) for n in PACKED}


def _small_pack(t, names=SMALL):
    flat = jnp.concatenate([t[n].reshape(-1) for n in names])
    total = -(-flat.shape[0] // SMALL_PAD) * SMALL_PAD
    return jnp.pad(flat, (0, total - flat.shape[0])).reshape(-1, 128)


def _small_unpack(a, shapes, names=SMALL):
    flat, out, at = a.reshape(-1), {}, 0
    for n in names:
        k = math.prod(shapes[n])
        out[n] = flat[at:at + k].reshape(shapes[n])
        at += k
    return out


def adamw(name, w, g, m, v):
    shape = w.shape
    two = lambda a: a.reshape(-1, shape[-1])
    rows = two(w).shape[0]
    T = 256 if rows % 256 == 0 else rows
    outs = rowwise(name, fn_adamw, [_row(two(w)), _row(two(g)), _row(two(m)), _row(two(v))], [],
                   [_out(shape[-1], f32)] * 3, T=T)
    return [o.reshape(shape) for o in outs]


def kernel(x, mem, positions, ffn1_norm, ffn1_w_in, ffn1_w_out, mix_norm, w_in, conv_dw, conv_b, conv_ln_g, conv_ln_b, sb_q_hnorm, sb_k_hnorm, mla_q_norm, mla_w_uq, mla_kv_norm, mla_w_ukv, mla_q_hnorm, mla_k_hnorm, mem_norm, mem_w_kv, mem_q_hnorm, mem_k_hnorm, w_branch, w_out, ffn2_norm, ffn2_w_in, ffn2_w_out, loss_target, m_ffn1_norm, m_ffn1_w_in, m_ffn1_w_out, m_mix_norm, m_w_in, m_conv_dw, m_conv_b, m_conv_ln_g, m_conv_ln_b, m_sb_q_hnorm, m_sb_k_hnorm, m_mla_q_norm, m_mla_w_uq, m_mla_kv_norm, m_mla_w_ukv, m_mla_q_hnorm, m_mla_k_hnorm, m_mem_norm, m_mem_w_kv, m_mem_q_hnorm, m_mem_k_hnorm, m_w_branch, m_w_out, m_ffn2_norm, m_ffn2_w_in, m_ffn2_w_out, v_ffn1_norm, v_ffn1_w_in, v_ffn1_w_out, v_mix_norm, v_w_in, v_conv_dw, v_conv_b, v_conv_ln_g, v_conv_ln_b, v_sb_q_hnorm, v_sb_k_hnorm, v_mla_q_norm, v_mla_w_uq, v_mla_kv_norm, v_mla_w_ukv, v_mla_q_hnorm, v_mla_k_hnorm, v_mem_norm, v_mem_w_kv, v_mem_q_hnorm, v_mem_k_hnorm, v_w_branch, v_w_out, v_ffn2_norm, v_ffn2_w_in, v_ffn2_w_out):
    w = dict(zip(WEIGHTS, (ffn1_norm, ffn1_w_in, ffn1_w_out, mix_norm, w_in, conv_dw, conv_b, conv_ln_g, conv_ln_b, sb_q_hnorm, sb_k_hnorm, mla_q_norm, mla_w_uq, mla_kv_norm, mla_w_ukv, mla_q_hnorm, mla_k_hnorm, mem_norm, mem_w_kv, mem_q_hnorm, mem_k_hnorm, w_branch, w_out, ffn2_norm, ffn2_w_in, ffn2_w_out)))
    m = dict(zip(WEIGHTS, (m_ffn1_norm, m_ffn1_w_in, m_ffn1_w_out, m_mix_norm, m_w_in, m_conv_dw, m_conv_b, m_conv_ln_g, m_conv_ln_b, m_sb_q_hnorm, m_sb_k_hnorm, m_mla_q_norm, m_mla_w_uq, m_mla_kv_norm, m_mla_w_ukv, m_mla_q_hnorm, m_mla_k_hnorm, m_mem_norm, m_mem_w_kv, m_mem_q_hnorm, m_mem_k_hnorm, m_w_branch, m_w_out, m_ffn2_norm, m_ffn2_w_in, m_ffn2_w_out)))
    v = dict(zip(WEIGHTS, (v_ffn1_norm, v_ffn1_w_in, v_ffn1_w_out, v_mix_norm, v_w_in, v_conv_dw, v_conv_b, v_conv_ln_g, v_conv_ln_b, v_sb_q_hnorm, v_sb_k_hnorm, v_mla_q_norm, v_mla_w_uq, v_mla_kv_norm, v_mla_w_ukv, v_mla_q_hnorm, v_mla_k_hnorm, v_mem_norm, v_mem_w_kv, v_mem_q_hnorm, v_mem_k_hnorm, v_w_branch, v_w_out, v_ffn2_norm, v_ffn2_w_in, v_ffn2_w_out)))
    S = x.shape[1]
    packed = [pack_layer_shard(w, l) for l in range(DEPTH)]
    conv_dw_full = gather_conv_dw(w["conv_dw"])
    gathered = {0: gather_shards("ag_shards_l0", packed[0])}

    def weights_of(l):
        W = unpack_layer(gathered[l], w)
        W["conv_dw"] = conv_dw_full[l]
        W.update({n: w[n][l] for n in SMALL})
        return W

    def deliver(l, landed):
        if landed is not None:
            gathered[l + 1] = landed

    reducers = {}

    def reducer_of(l, layer_grads):
        reducers[l] = LayerReduce(f"rs_l{l}", layer_grads)
        return reducers[l]

    loss_blk, dx, g = local_step(x[0], mem[0], positions.reshape(S, 1), loss_target[0], weights_of,
                                 lambda l: packed[l + 1] if l + 1 < DEPTH else None, deliver, reducer_of)
    loss = lax.psum(loss_blk[0, 0], ("x", "y", "c"))
    grads = finish_reduce([reducers[l] for l in range(DEPTH)], w)
    small_shapes = {n: w[n].shape for n in SMALL}
    reduced = SMALL + ("conv_dw",)
    reduced_shapes = dict(small_shapes, conv_dw=conv_dw_full.shape)
    g_all = {n: jnp.stack([gl[n] for gl in g]) for n in reduced}
    g_all = sum8("small_sum", all8_gather("small_gather", _small_pack(g_all, reduced)))
    g_all = _small_unpack(g_all, reduced_shapes, reduced)
    width = w["conv_dw"].shape[-1]
    x_pos, y_pos, _ = _coords()
    grads["conv_dw"] = lax.dynamic_slice_in_dim(g_all.pop("conv_dw"), (2 * x_pos + y_pos) * width, width, axis=2)
    grads.update(g_all)
    g_small = _small_pack(grads)
    delta, new_m, new_v = {}, {}, {}
    for n in SHARDED:
        delta[n], new_m[n], new_v[n] = adamw(f"adamw_{n}", w[n], grads[n], m[n], v[n])
    d_s, m_s, v_s = adamw("adamw_small", _small_pack(w), g_small, _small_pack(m), _small_pack(v))
    for t, packed in ((delta, d_s), (new_m, m_s), (new_v, v_s)):
        t.update(_small_unpack(packed, small_shapes))
    return (loss, dx.reshape(x.shape), *[grads[n] for n in WEIGHTS], *[delta[n] for n in WEIGHTS],
            *[new_m[n] for n in WEIGHTS], *[new_v[n] for n in WEIGHTS])
```

```python
import math

import jax
import jax.numpy as jnp
from jax import lax
from jax.experimental import pallas as pl
from jax.experimental.pallas import tpu as pltpu

f32, bf16 = jnp.float32, jnp.bfloat16

D_MODEL = 1024
DEPTH = 4
CHUNK = 64
FFN_HIDDEN = 2048
CONV_CH = 512
CONV_WIDTH = 31
HEADS = 4
HEAD_DIM = 128
MLA_NOPE = 128
MLA_ROPE = 64
MLA_QK = MLA_NOPE + MLA_ROPE
MLA_PAD = 256
MLA_Q_LORA = 256
MLA_KV_LORA = 256
N_BRANCH = 4
BRANCH_WIDTH = 512
ROPE_BASE = 10000.0
EPS = 1e-6
NEG_INF = -1e30
IN_WIDTH = 7744
U_WIDTH = 8192
_U_SEGS = (("gates", 3648, 4096, 0), ("conv", 0, 1024, 4096), ("sb", 1024, 1536, 5120), ("qlat", 2560, 256, 6656),
           ("kvlat", 2816, 256, 6912), ("memq", 3136, 512, 7168), ("krope", 3072, 64, 7680))
U_PAD_FROM = 7744

ADAM_LR, ADAM_B1, ADAM_B2, ADAM_EPS, ADAM_WD, ADAM_STEP = 0.001, 0.9, 0.999, 1e-08, 0.01, 10

VMEM_LIMIT = 48 * 1024 * 1024
MESH = pl.DeviceIdType.MESH

SHARDED = ("ffn1_w_in", "ffn1_w_out", "w_in", "conv_dw", "mla_w_uq", "mla_w_ukv", "mem_w_kv", "w_branch", "w_out",
           "ffn2_w_in", "ffn2_w_out")
SHARD_AXIS = {"ffn1_w_in": 2, "ffn1_w_out": 1, "w_in": 2, "conv_dw": 2, "mla_w_uq": 2, "mla_w_ukv": 2, "mem_w_kv": 1,
              "w_branch": 3, "w_out": 1, "ffn2_w_in": 2, "ffn2_w_out": 1}
SMALL = ("ffn1_norm", "mix_norm", "conv_b", "conv_ln_g", "conv_ln_b", "sb_q_hnorm", "sb_k_hnorm", "mla_q_norm",
         "mla_kv_norm", "mla_q_hnorm", "mla_k_hnorm", "mem_norm", "mem_q_hnorm", "mem_k_hnorm", "ffn2_norm")
WEIGHTS = ("ffn1_norm", "ffn1_w_in", "ffn1_w_out", "mix_norm", "w_in", "conv_dw", "conv_b", "conv_ln_g", "conv_ln_b",
           "sb_q_hnorm", "sb_k_hnorm", "mla_q_norm", "mla_w_uq", "mla_kv_norm", "mla_w_ukv", "mla_q_hnorm",
           "mla_k_hnorm", "mem_norm", "mem_w_kv", "mem_q_hnorm", "mem_k_hnorm", "w_branch", "w_out", "ffn2_norm",
           "ffn2_w_in", "ffn2_w_out")
PACK_GRAIN = 2 * 256 * 1024
SMALL_PAD = 8 * 128


def _params(sem, vmem=VMEM_LIMIT):
    return pltpu.CompilerParams(dimension_semantics=sem, vmem_limit_bytes=vmem)


def _pick(n, pref):
    for t in pref:
        if n % t == 0:
            return t
    return n


def mm(name, a, b, form, *, out_dtype=f32, alpha=1.0, res=None, tm=None, tn=None, tk=None):
    if form == "nn":
        (M, K), (K2, N) = a.shape, b.shape
    elif form == "nt":
        (M, K), (N, K2) = a.shape, b.shape
    else:
        (K, M), (K2, N) = a.shape, b.shape
    assert K == K2, (name, a.shape, b.shape)
    tm = tm or _pick(M, (1024, 512, 256, 128))
    tn = tn or _pick(N, (1024, 512, 256, 128))
    tk = tk or _pick(K, (2048, 1024, 512, 256))
    nk = K // tk
    if form == "nn":
        a_spec = pl.BlockSpec((tm, tk), lambda i, j, k: (i, k))
        b_spec = pl.BlockSpec((tk, tn), lambda i, j, k: (k, j))
        dims = (((1,), (0,)), ((), ()))
    elif form == "nt":
        a_spec = pl.BlockSpec((tm, tk), lambda i, j, k: (i, k))
        b_spec = pl.BlockSpec((tn, tk), lambda i, j, k: (j, k))
        dims = (((1,), (1,)), ((), ()))
    else:
        a_spec = pl.BlockSpec((tk, tm), lambda i, j, k: (k, i))
        b_spec = pl.BlockSpec((tk, tn), lambda i, j, k: (k, j))
        dims = (((0,), (0,)), ((), ()))
    o_spec = pl.BlockSpec((tm, tn), lambda i, j, k: (i, j))
    has_res = res is not None

    def body(a_ref, b_ref, *rest):
        if has_res:
            r_ref, o_ref, acc_ref = rest
        else:
            o_ref, acc_ref = rest
        k = pl.program_id(2)
        part = lax.dot_general(a_ref[...].astype(bf16), b_ref[...].astype(bf16), dims, preferred_element_type=f32)

        def finish(acc):
            r = acc if alpha == 1.0 else acc * alpha
            if has_res:
                r = r_ref[...].astype(f32) + r
            o_ref[...] = r.astype(o_ref.dtype)

        if nk == 1:
            finish(part)
        else:
            @pl.when(k == 0)
            def _():
                acc_ref[...] = part

            @pl.when(k > 0)
            def _():
                acc_ref[...] += part

            @pl.when(k == nk - 1)
            def _():
                finish(acc_ref[...])

    ins = [a, b] + ([res] if has_res else [])
    in_specs = [a_spec, b_spec] + ([o_spec] if has_res else [])
    return pl.pallas_call(
        body, name=name, grid=(M // tm, N // tn, nk), in_specs=in_specs, out_specs=o_spec,
        out_shape=jax.ShapeDtypeStruct((M, N), out_dtype), scratch_shapes=[pltpu.VMEM((tm, tn), f32)],
        compiler_params=_params(("parallel", "parallel", "arbitrary")))(*ins)


class _Span:
    def __init__(self, w, off, st, H):
        self.w = w
        if st == 0 or H == 1:
            self.width, self.index, self.base, self.step = w, off, 0, 0
        else:
            self.width, self.index = st * H * w, off // (st * H)
            self.base, self.step = (off % (st * H)) * w, st * w

    def spec(self, T):
        return pl.BlockSpec((T, self.width), lambda i: (i, self.index))

    def lanes(self, h):
        return slice(self.base + h * self.step, self.base + h * self.step + self.w)


def _full_spec(p):
    return pl.BlockSpec(p.shape, lambda i: (0,) * p.ndim)


def rowwise(name, fn, rows, params, outs, *, T=512, H=1):
    S = rows[0][0].shape[0]
    T = min(T, S)
    n_r, n_p = len(rows), len(params)
    in_spans = [_Span(w, off, st, H) for (_, w, off, st) in rows]
    out_spans = [_Span(w, off, st, H) for (_, _, w, off, st) in outs]

    def body(*refs):
        p_vals = [r[...] for r in refs[n_r:n_r + n_p]]
        for h in range(H):
            vals = [r[:, sp.lanes(h)] for r, sp in zip(refs[:n_r], in_spans)]
            res = fn(*vals, *p_vals)
            for o_ref, sp, r in zip(refs[n_r + n_p:], out_spans, res):
                o_ref[:, sp.lanes(h)] = r.astype(o_ref.dtype)

    res = pl.pallas_call(
        body, name=name, grid=(S // T,),
        in_specs=[sp.spec(T) for sp in in_spans] + [_full_spec(p) for p in params],
        out_specs=[sp.spec(T) for sp in out_spans],
        out_shape=[jax.ShapeDtypeStruct((S, tw), dt) for (tw, dt, _, _, _) in outs],
        compiler_params=_params(("parallel",)))(*[r[0] for r in rows], *params)
    return res


def rowwise_bwd(name, fn, rows, params, douts, drows, *, T=512, H=1, nondiff=(), merge=None):
    S = rows[0][0].shape[0]
    T = min(T, S)
    n_r, n_p, n_d = len(rows), len(params), len(douts)
    diff_idx = [k for k in range(n_r) if k not in nondiff]
    merge = merge or [(j,) for j in range(len(diff_idx))]
    shared = [H > 1 and rows[diff_idx[pos[0]]][3] == 0 for pos in merge]
    n_o = len(merge)
    assert n_o == len(drows)
    in_spans = [_Span(w, off, st, H) for (_, w, off, st) in rows]
    d_spans = [_Span(w, off, st, H) for (_, w, off, st) in douts]
    out_spans = [_Span(w, off, st, H) for (_, _, w, off, st) in drows]

    def body(*refs):
        i = pl.program_id(0)
        p_vals = [r[...].astype(f32) for r in refs[n_r:n_r + n_p]]
        out_refs = refs[n_r + n_p + n_d:]
        shared_sum = [None] * n_o
        p_sum = [None] * n_p
        for h in range(H):
            row_vals = [r[:, sp.lanes(h)].astype(f32) for r, sp in zip(refs[:n_r], in_spans)]
            d_vals = [r[:, sp.lanes(h)].astype(f32) for r, sp in zip(refs[n_r + n_p:n_r + n_p + n_d], d_spans)]

            def f(*args):
                full = list(row_vals)
                for k, v in zip(diff_idx, args[:len(diff_idx)]):
                    full[k] = v
                return tuple(fn(*full, *args[len(diff_idx):]))

            _, vjp = jax.vjp(f, *[row_vals[k] for k in diff_idx], *p_vals)
            cts = vjp(tuple(d_vals))
            row_cts = [cts[pos[0]] if len(pos) == 1 else jnp.concatenate([cts[j] for j in pos], axis=-1) for pos in merge]
            for j, (o_ref, sp, ct) in enumerate(zip(out_refs[:n_o], out_spans, row_cts)):
                if shared[j]:
                    shared_sum[j] = ct if h == 0 else shared_sum[j] + ct
                else:
                    o_ref[:, sp.lanes(h)] = ct.astype(o_ref.dtype)
            for j, ct in enumerate(cts[len(diff_idx):]):
                p_sum[j] = ct if h == 0 else p_sum[j] + ct
        for j, o_ref in enumerate(out_refs[:n_o]):
            if shared[j]:
                o_ref[...] = shared_sum[j].astype(o_ref.dtype)
        for o_ref, ct in zip(out_refs[n_o:], p_sum):
            @pl.when(i == 0)
            def _():
                o_ref[...] = ct

            @pl.when(i > 0)
            def _():
                o_ref[...] += ct

    res = pl.pallas_call(
        body, name=name, grid=(S // T,),
        in_specs=[sp.spec(T) for sp in in_spans] + [_full_spec(p) for p in params] + [sp.spec(T) for sp in d_spans],
        out_specs=[sp.spec(T) for sp in out_spans] + [_full_spec(p) for p in params],
        out_shape=[jax.ShapeDtypeStruct((S, tw), dt) for (tw, dt, _, _, _) in drows]
        + [jax.ShapeDtypeStruct(p.shape, f32) for p in params],
        compiler_params=_params(("arbitrary",)))(*[r[0] for r in rows], *params, *[d[0] for d in douts])
    return res[:n_o], res[n_o:]


def _rms(x, g, n=None):
    x = x.astype(f32)
    n = n or x.shape[-1]
    return x * lax.rsqrt(jnp.sum(x * x, axis=-1, keepdims=True) * (1.0 / n) + EPS) * g.astype(f32)


def _sigmoid(x):
    return 1.0 / (1.0 + jnp.exp(-x))


def _silu(x):
    return x * _sigmoid(x)


def fn_rms(x, g):
    return (_rms(x, g),)


def fn_rms_res(x, g):
    return (x.astype(f32), _rms(x, g))


def fn_swiglu(gate, up):
    return (_silu(gate.astype(f32)) * up.astype(f32),)


def fn_sb_prep(q, k, gq, gk):
    return (_rms(q, gq), _rms(k, gk))


def fn_ln_silu(y, b, g, beta):
    y = y.astype(f32) + b
    mu = jnp.mean(y, axis=-1, keepdims=True)
    var = jnp.mean(jnp.square(y - mu), axis=-1, keepdims=True)
    return (_silu((y - mu) * lax.rsqrt(var + EPS) * g + beta),)


def fn_merge(g0, g1, g2, g3, p0, p1, p2, p3):
    out = _sigmoid(g0.astype(f32)) * p0.astype(f32)
    for g, p in ((g1, p1), (g2, p2), (g3, p3)):
        out = out + _sigmoid(g.astype(f32)) * p.astype(f32)
    return (out,)


def _rot_fwd(x):
    z = jnp.zeros_like(x[:, :MLA_NOPE])
    h = MLA_ROPE // 2
    return jnp.concatenate([z, -x[:, MLA_NOPE + h:MLA_QK], x[:, MLA_NOPE:MLA_NOPE + h], z[:, :MLA_PAD - MLA_QK]], axis=-1)


def _rot_bwd(g):
    z = jnp.zeros_like(g[:, :MLA_NOPE])
    h = MLA_ROPE // 2
    return jnp.concatenate([z, g[:, MLA_NOPE + h:MLA_QK], -g[:, MLA_NOPE:MLA_NOPE + h], z[:, :MLA_PAD - MLA_QK]], axis=-1)


@jax.custom_vjp
def _rope(x, c, s):
    return x * c + _rot_fwd(x) * s


def _rope_f(x, c, s):
    return _rope(x, c, s), (c, s)


def _rope_b(res, g):
    c, s = res
    return g * c + _rot_bwd(g * s), jnp.zeros_like(c), jnp.zeros_like(s)


_rope.defvjp(_rope_f, _rope_b)


def fn_mla_q(q, c, s, gain):
    return (_rope(_rms(q, gain, MLA_QK), c, s),)


def fn_mla_k(kn, kr, c, s, gain):
    k = jnp.concatenate([kn.astype(f32), kr.astype(f32)], axis=-1)
    return (_rope(_rms(k, gain, MLA_QK), c, s),)


def fn_mla_k_v(kn, kr, v, c, s, gain):
    return (fn_mla_k(kn, kr, c, s, gain)[0], v.astype(f32))


def fn_mem_k_v(k, v, gain):
    return (_rms(k, gain), v.astype(f32))


def fn_adamw(w, g, m, v):
    m = ADAM_B1 * m + (1.0 - ADAM_B1) * g
    v = ADAM_B2 * v + (1.0 - ADAM_B2) * jnp.square(g)
    m_hat = m / (1.0 - ADAM_B1 ** ADAM_STEP)
    v_hat = v / (1.0 - ADAM_B2 ** ADAM_STEP)
    delta = -ADAM_LR * (m_hat / (jnp.sqrt(v_hat) + ADAM_EPS) + ADAM_WD * w)
    return delta, m, v


def _head_spec(rows, w, off, st):
    return pl.BlockSpec((rows, w), lambda h, i: (0, off + st * h))


def _qblk_spec(B, w, off, st):
    return pl.BlockSpec((B, w), lambda h, i: (i, off + st * h))


def _chunk_mask(tq, tk, d):
    r = lax.broadcasted_iota(jnp.int32, (tq, tk), 0) // CHUNK
    c = (d * tk + lax.broadcasted_iota(jnp.int32, (tq, tk), 1)) // CHUNK
    return c <= r


def _strict_mask(tq, tk, d):
    r = lax.broadcasted_iota(jnp.int32, (tq, tk), 0)
    c = d * tk + lax.broadcasted_iota(jnp.int32, (tq, tk), 1)
    return c < r


_NT = (((1,), (1,)), ((), ()))
_TN = (((0,), (0,)), ((), ()))


def _tiles(Sq, Sk, mask, tq, tk):
    tq = min(tq, Sq)
    if mask is None:
        return tq, Sk, 0
    tk = min(tk, tq)
    assert Sq == Sk and tq % tk == 0 and tk % CHUNK == 0
    return tq, tk, tq // tk


def _gather_in_steps(own_ref, out_ref, send_sems, recv_sems, local_sem, h, i, last_h, last_i):
    x, y, c = lax.axis_index("x"), lax.axis_index("y"), lax.axis_index("c")
    sib = (x, y, 1 - c)
    chips = [(2, (1 - x, y)), (1, (x, 1 - y)), (3, (1 - x, 1 - y))]

    def copy(k, src, m, half, to):
        return pltpu.make_async_remote_copy(src, out_ref.at[m, half], send_sems.at[k], recv_sems.at[k],
                                            device_id=to, device_id_type=MESH)

    mine = pltpu.make_async_copy(own_ref, out_ref.at[0], local_sem)
    first = [copy(k, own_ref.at[c], m, c, (px, py, c)) for k, (m, (px, py)) in enumerate(chips)]
    passed = [copy(3 + k, out_ref.at[m, c], m, c, sib) for k, (m, _) in enumerate(chips)]
    landed = [copy(3 + k, out_ref.at[m, 1 - c], m, 1 - c, sib) for k, (m, _) in enumerate(chips)]

    @pl.when(jnp.logical_and(h == 0, i == 0))
    def _():
        mine.start()
        for cp in first:
            cp.start()

    @pl.when(jnp.logical_and(h == last_h, i == 0))
    def _():
        for k in range(3):
            first[k].wait_recv()
            passed[k].start()

    @pl.when(jnp.logical_and(h == last_h, i == last_i))
    def _():
        for cp in landed:
            cp.wait_recv()
        for cp in first + passed:
            cp.wait_send()
        mine.wait()


def attn_fwd(name, q, k, v, *, scale, mask, tq=1024, tk=512, ride=None):
    Sq, Sk = q[0].shape[0], k[0].shape[0]
    tq, tk, nd = _tiles(Sq, Sk, mask, tq, tk)
    n_q = Sq // tq

    def body(q_ref, k_ref, v_ref, *rest):
        if ride is None:
            o_ref, lse_ref = rest
        else:
            own_ref, o_ref, lse_ref, land_ref, send_sems, recv_sems, local_sem = rest
            _gather_in_steps(own_ref, land_ref, send_sems, recv_sems, local_sem, pl.program_id(0), pl.program_id(1),
                             HEADS - 1, n_q - 1)
        i = pl.program_id(1)
        qv = q_ref[...]

        def block(off, carry, d):
            m, l, acc = carry
            kb, vb = k_ref[pl.ds(off, tk), :].astype(bf16), v_ref[pl.ds(off, tk), :].astype(bf16)
            s = lax.dot_general(qv, kb, _NT, preferred_element_type=f32) * scale
            if d is not None:
                s = jnp.where(_chunk_mask(tq, tk, d), s, NEG_INF)
            m_new = jnp.maximum(m, jnp.max(s, axis=-1, keepdims=True))
            p = jnp.exp(s - m_new)
            corr = jnp.exp(m - m_new)
            l = l * corr + jnp.sum(p, axis=-1, keepdims=True)
            acc = acc * corr + jnp.dot(p.astype(bf16), vb, preferred_element_type=f32)
            return m_new, l, acc

        carry = (jnp.full((tq, 1), NEG_INF, f32), jnp.zeros((tq, 1), f32), jnp.zeros((tq, HEAD_DIM), f32))
        if nd:
            carry = lax.fori_loop(0, i * nd, lambda j, c: block(pl.multiple_of(j * tk, tk), c, None), carry)
            for d in range(nd):
                carry = block(pl.multiple_of(i * tq + d * tk, tk), carry, d)
        else:
            carry = block(0, carry, None)
        m, l, acc = carry
        o_ref[...] = (acc / l).astype(o_ref.dtype)
        lse_ref[...] = jnp.broadcast_to(m + jnp.log(l), (tq, HEAD_DIM))

    in_specs = [_qblk_spec(tq, *q[1:]), _head_spec(Sk, *k[1:]), _head_spec(Sk, *v[1:])]
    out_specs = [_qblk_spec(tq, HEAD_DIM, 0, 1), _qblk_spec(tq, HEAD_DIM, 0, 1)]
    out_shape = [jax.ShapeDtypeStruct((Sq, HEADS * HEAD_DIM), f32), jax.ShapeDtypeStruct((Sq, HEADS * HEAD_DIM), f32)]
    if ride is None:
        return pl.pallas_call(body, name=name, grid=(HEADS, n_q), in_specs=in_specs, out_specs=out_specs,
                              out_shape=out_shape, compiler_params=_params(("parallel", "arbitrary")))(q[0], k[0], v[0])
    any_space = pl.BlockSpec(memory_space=pl.ANY)
    return pl.pallas_call(
        body, name=name, grid=(HEADS, n_q), in_specs=in_specs + [any_space], out_specs=out_specs + [any_space],
        out_shape=out_shape + [jax.ShapeDtypeStruct((4,) + ride.shape, ride.dtype)],
        scratch_shapes=[pltpu.SemaphoreType.DMA((6,)), pltpu.SemaphoreType.DMA((6,)), pltpu.SemaphoreType.DMA],
        compiler_params=pltpu.CompilerParams(dimension_semantics=("arbitrary", "arbitrary"), vmem_limit_bytes=VMEM_LIMIT,
                                             has_side_effects=True))(q[0], k[0], v[0], ride)


def _exchange_in_steps(kind, src_ref, out_ref, send_sems, recv_sems, is_first, is_last):
    x, y, c = lax.axis_index("x"), lax.axis_index("y"), lax.axis_index("c")
    if kind == "sibling":
        copies = [pltpu.make_async_remote_copy(src_ref, out_ref, send_sems.at[0], recv_sems.at[0],
                                               device_id=(x, y, 1 - c), device_id_type=MESH)]
    else:
        copies = [pltpu.make_async_remote_copy(src_ref.at[2 * px + py], out_ref.at[k], send_sems.at[k], recv_sems.at[k],
                                               device_id=(px, py, c), device_id_type=MESH)
                  for k, (px, py) in enumerate([(1 - x, y), (x, 1 - y), (1 - x, 1 - y)])]

    @pl.when(is_first)
    def _():
        for cp in copies:
            cp.start()

    @pl.when(is_last)
    def _():
        for cp in copies:
            cp.wait()


def _ride_call(body, name, grid, in_specs, out_specs, out_shape, operands, ride, vmem=VMEM_LIMIT):
    if ride is None:
        return pl.pallas_call(body, name=name, grid=grid, in_specs=in_specs, out_specs=out_specs, out_shape=out_shape,
                              compiler_params=_params(("arbitrary",) * len(grid), vmem))(*operands)
    kind, src = ride
    n = 1 if kind == "sibling" else 3
    landing = jax.ShapeDtypeStruct(src.shape if kind == "sibling" else (3,) + src.shape[1:], src.dtype)
    any_space = pl.BlockSpec(memory_space=pl.ANY)
    return pl.pallas_call(
        body, name=name, grid=grid, in_specs=in_specs + [any_space], out_specs=out_specs + [any_space],
        out_shape=out_shape + [landing],
        scratch_shapes=[pltpu.SemaphoreType.DMA((n,)), pltpu.SemaphoreType.DMA((n,))],
        compiler_params=pltpu.CompilerParams(dimension_semantics=("arbitrary",) * len(grid), vmem_limit_bytes=vmem,
                                             has_side_effects=True))(*operands, src)


def attn_bwd(name, q, k, v, o, do, lse, *, scale, mask, tq=1024, tk=512, ride=None):
    Sq, Sk = q[0].shape[0], k[0].shape[0]
    tq, tk, nd = _tiles(Sq, Sk, mask, tq, tk)
    dq_w = q[1]
    n_q = Sq // tq

    def body(q_ref, k_ref, v_ref, o_ref, do_ref, lse_ref, *rest):
        if ride is None:
            dq_ref, dk_ref, dv_ref = rest
        else:
            src_ref, dq_ref, dk_ref, dv_ref, land_ref, send_sems, recv_sems = rest
            h, i = pl.program_id(0), pl.program_id(1)
            _exchange_in_steps(ride[0], src_ref, land_ref, send_sems, recv_sems, jnp.logical_and(h == 0, i == 0),
                               jnp.logical_and(h == HEADS - 1, i == n_q - 1))
        i = pl.program_id(1)

        @pl.when(i == 0)
        def _():
            dk_ref[...] = jnp.zeros_like(dk_ref)
            dv_ref[...] = jnp.zeros_like(dv_ref)

        qv, dov = q_ref[...], do_ref[...].astype(bf16)
        delta = jnp.sum(do_ref[...].astype(f32) * o_ref[...].astype(f32), axis=-1, keepdims=True)
        lse_v = lse_ref[:, :1]

        def block(off, dq_acc, d):
            kb, vb = k_ref[pl.ds(off, tk), :].astype(bf16), v_ref[pl.ds(off, tk), :].astype(bf16)
            s = lax.dot_general(qv, kb, _NT, preferred_element_type=f32) * scale
            if d is not None:
                s = jnp.where(_chunk_mask(tq, tk, d), s, NEG_INF)
            p = jnp.exp(s - lse_v)
            dv_ref[pl.ds(off, tk), :] += lax.dot_general(p.astype(bf16), dov, _TN, preferred_element_type=f32)
            dp = lax.dot_general(dov, vb, _NT, preferred_element_type=f32)
            ds = (p * (dp - delta) * scale).astype(bf16)
            dk_ref[pl.ds(off, tk), :] += lax.dot_general(ds, qv, _TN, preferred_element_type=f32)
            return dq_acc + jnp.dot(ds, kb, preferred_element_type=f32)

        acc = jnp.zeros((tq, dq_w), f32)
        if nd:
            acc = lax.fori_loop(0, i * nd, lambda j, c: block(pl.multiple_of(j * tk, tk), c, None), acc)
            for d in range(nd):
                acc = block(pl.multiple_of(i * tq + d * tk, tk), acc, d)
        else:
            acc = block(0, acc, None)
        dq_ref[...] = acc.astype(dq_ref.dtype)

    hd = _qblk_spec(tq, HEAD_DIM, 0, 1)
    return _ride_call(
        body, name, (HEADS, n_q),
        [_qblk_spec(tq, *q[1:]), _head_spec(Sk, *k[1:]), _head_spec(Sk, *v[1:]), hd, hd, hd],
        [_qblk_spec(tq, dq_w, 0, 1), _head_spec(Sk, dq_w, 0, 1), _head_spec(Sk, HEAD_DIM, 0, 1)],
        [jax.ShapeDtypeStruct((Sq, HEADS * dq_w), f32), jax.ShapeDtypeStruct((Sk, HEADS * dq_w), f32),
         jax.ShapeDtypeStruct((Sk, HEADS * HEAD_DIM), f32)],
        (q[0], k[0], v[0], o, do, lse), ride, 56 * 1024 * 1024)


SB_LOG_ZERO = -104.0


def _tri(B, rel):
    r = lax.broadcasted_iota(jnp.int32, (B, B), 0)
    c = lax.broadcasted_iota(jnp.int32, (B, B), 1)
    return rel(r, c).astype(bf16)


def _sb_scores(qv, kb, scale):
    z = lax.dot_general(qv, kb, _NT, preferred_element_type=f32) * scale
    e = jnp.exp(-jnp.abs(z))
    log_keep = -(jnp.maximum(z, 0.0) + jnp.log(1.0 + e))
    return z, e, log_keep


def _split_dot(x, m):
    hi = x.astype(bf16)
    lo = (x - hi.astype(f32)).astype(bf16)
    return jnp.dot(hi, m, preferred_element_type=f32) + jnp.dot(lo, m, preferred_element_type=f32)


def sb_fwd(name, q, k, v, *, tq=512, tk=256):
    S = q[0].shape[0]
    tq, tk, nd = _tiles(S, S, "strict", tq, tk)
    scale = HEAD_DIM ** -0.5
    m_ex = _tri(tk, lambda j, s: j > s)

    def body(q_ref, k_ref, v_ref, mex_ref, o_ref, tot_ref, cnt_ref):
        i = pl.program_id(1)
        qv, mex = q_ref[...], mex_ref[...]

        def block(off, carry, d):
            later, acc = carry
            kb, vb = k_ref[pl.ds(off, tk), :].astype(bf16), v_ref[pl.ds(off, tk), :].astype(bf16)
            z, _, lk = _sb_scores(qv, kb, scale)
            if d is not None:
                lk = jnp.where(_strict_mask(tq, tk, d), lk, 0.0)
            a = jnp.exp(z + lk + _split_dot(lk, mex) + later)
            if d is not None:
                a = jnp.where(_strict_mask(tq, tk, d), a, 0.0)
            acc = acc + jnp.dot(a.astype(bf16), vb, preferred_element_type=f32)
            return later + jnp.sum(lk, axis=-1, keepdims=True), acc

        carry = (jnp.zeros((tq, 1), f32), jnp.zeros((tq, HEAD_DIM), f32))
        for d in reversed(range(nd)):
            carry = block(pl.multiple_of(i * tq + d * tk, tk), carry, d)
        n_full = i * nd

        def more(state):
            t, later, _ = state
            return jnp.logical_and(t < n_full, jnp.max(later) > SB_LOG_ZERO)

        def step(state):
            t, later, acc = state
            later, acc = block(pl.multiple_of((n_full - 1 - t) * tk, tk), (later, acc), None)
            return t + 1, later, acc

        done, total, acc = lax.while_loop(more, step, (jnp.int32(0),) + carry)
        o_ref[...] = acc.astype(o_ref.dtype)
        tot_ref[...] = jnp.broadcast_to(total, (tq, HEAD_DIM))
        cnt_ref[...] = jnp.full((8, HEAD_DIM), done, f32)

    hd = _qblk_spec(tq, HEAD_DIM, 0, 1)
    return pl.pallas_call(
        body, name=name, grid=(HEADS, S // tq),
        in_specs=[_qblk_spec(tq, *q[1:]), _head_spec(S, *k[1:]), _head_spec(S, *v[1:]),
                  pl.BlockSpec((tk, tk), lambda h, i: (0, 0))],
        out_specs=[hd, hd, _qblk_spec(8, HEAD_DIM, 0, 1)],
        out_shape=[jax.ShapeDtypeStruct((S, HEADS * HEAD_DIM), f32), jax.ShapeDtypeStruct((S, HEADS * HEAD_DIM), f32),
                   jax.ShapeDtypeStruct((8 * (S // tq), HEADS * HEAD_DIM), f32)],
        compiler_params=_params(("parallel", "arbitrary")))(q[0], k[0], v[0], m_ex)


def sb_bwd(name, q, k, v, tot, cnt, do, *, tq=512, tk=256, ride=None):
    S = q[0].shape[0]
    tq, tk, nd = _tiles(S, S, "strict", tq, tk)
    scale = HEAD_DIM ** -0.5
    m_le, m_lt = _tri(tk, lambda j, s: j <= s), _tri(tk, lambda j, s: j < s)

    n_q = S // tq

    def body(q_ref, k_ref, v_ref, tot_ref, cnt_ref, do_ref, mle_ref, mlt_ref, *rest):
        if ride is None:
            dq_ref, dk_ref, dv_ref = rest
        else:
            src_ref, dq_ref, dk_ref, dv_ref, land_ref, send_sems, recv_sems = rest
            h, i = pl.program_id(0), pl.program_id(1)
            _exchange_in_steps(ride[0], src_ref, land_ref, send_sems, recv_sems, jnp.logical_and(h == 0, i == 0),
                               jnp.logical_and(h == HEADS - 1, i == n_q - 1))
        i = pl.program_id(1)

        @pl.when(i == 0)
        def _():
            dk_ref[...] = jnp.zeros_like(dk_ref)
            dv_ref[...] = jnp.zeros_like(dv_ref)

        qv, dov, mle, mlt = q_ref[...], do_ref[...].astype(bf16), mle_ref[...], mlt_ref[...]
        total = tot_ref[:, :1]

        def block(off, carry, d):
            before, g_before, dq_acc = carry
            kb, vb = k_ref[pl.ds(off, tk), :].astype(bf16), v_ref[pl.ds(off, tk), :].astype(bf16)
            z, e, lk = _sb_scores(qv, kb, scale)
            sig = jnp.where(z >= 0, 1.0, e) / (1.0 + e)
            if d is not None:
                lk = jnp.where(_strict_mask(tq, tk, d), lk, 0.0)
            later = (total - before) - _split_dot(lk, mle)
            a = jnp.exp(z + lk + later)
            if d is not None:
                a = jnp.where(_strict_mask(tq, tk, d), a, 0.0)
            g = a * lax.dot_general(dov, vb, _NT, preferred_element_type=f32)
            prefix = g_before + jnp.dot(g.astype(bf16), mlt, preferred_element_type=f32)
            dz = (g * (1.0 - sig) - prefix * sig) * scale
            if d is not None:
                dz = jnp.where(_strict_mask(tq, tk, d), dz, 0.0)
            dzb = dz.astype(bf16)
            dk_ref[pl.ds(off, tk), :] += lax.dot_general(dzb, qv, _TN, preferred_element_type=f32)
            dv_ref[pl.ds(off, tk), :] += lax.dot_general(a.astype(bf16), dov, _TN, preferred_element_type=f32)
            return (before + jnp.sum(lk, axis=-1, keepdims=True), g_before + jnp.sum(g, axis=-1, keepdims=True),
                    dq_acc + jnp.dot(dzb, kb, preferred_element_type=f32))

        zero = jnp.zeros((tq, 1), f32)
        first = i * nd - jnp.max(cnt_ref[...]).astype(jnp.int32)
        carry = lax.fori_loop(first, i * nd, lambda j, c: block(pl.multiple_of(j * tk, tk), c, None),
                              (zero, zero, jnp.zeros((tq, HEAD_DIM), f32)))
        for d in range(nd):
            carry = block(pl.multiple_of(i * tq + d * tk, tk), carry, d)
        dq_ref[...] = carry[2].astype(dq_ref.dtype)

    hd = _qblk_spec(tq, HEAD_DIM, 0, 1)
    tri = pl.BlockSpec((tk, tk), lambda h, i: (0, 0))
    return _ride_call(
        body, name, (HEADS, n_q),
        [_qblk_spec(tq, *q[1:]), _head_spec(S, *k[1:]), _head_spec(S, *v[1:]), hd, _qblk_spec(8, HEAD_DIM, 0, 1),
         hd, tri, tri],
        [hd, _head_spec(S, HEAD_DIM, 0, 1), _head_spec(S, HEAD_DIM, 0, 1)],
        [jax.ShapeDtypeStruct((S, HEADS * HEAD_DIM), f32), jax.ShapeDtypeStruct((S, HEADS * HEAD_DIM), f32),
         jax.ShapeDtypeStruct((S, HEADS * HEAD_DIM), f32)],
        (q[0], k[0], v[0], tot, cnt, do, m_le, m_lt), ride)


CONV_HALO = 32
CONV_ROWS = 64
CONV_A_BLK, CONV_G_BLK = 8, 9


def _glu(a, g):
    return a.astype(f32) * _sigmoid(g.astype(f32))


SUBLANES = 8


def _phase_copies(src_ref, dst_ref, rows):
    src_ref[pl.ds(rows, SUBLANES), :] = jnp.zeros((SUBLANES, src_ref.shape[1]), src_ref.dtype)
    for p in range(SUBLANES):
        dst_ref[p] = src_ref[pl.ds(p, rows), :]


def _window(dst_ref, start, rows):
    return dst_ref[start % SUBLANES, pl.ds(start - start % SUBLANES, rows), :]


def conv_fwd(name, u, dw):
    S = u.shape[0]
    T = min(512, S)
    nT = S // T

    def body(a_ref, g_ref, ap_ref, gp_ref, dw_ref, y_ref, ext_ref, ext8_ref):
        i = pl.program_id(0)
        prev = _glu(ap_ref[T - CONV_HALO:, :], gp_ref[T - CONV_HALO:, :])
        ext_ref[:CONV_HALO, :] = jnp.where(i > 0, prev, 0.0)
        ext_ref[pl.ds(CONV_HALO, T), :] = _glu(a_ref[...], g_ref[...])
        _phase_copies(ext_ref, ext8_ref, T + CONV_HALO)
        rc = min(CONV_ROWS, T)
        for r0 in range(0, T, rc):
            acc = jnp.zeros((rc, CONV_CH), f32)
            for w in range(CONV_WIDTH):
                acc = acc + dw_ref[w:w + 1, :] * _window(ext8_ref, r0 + w + CONV_HALO - (CONV_WIDTH - 1), rc)
            y_ref[pl.ds(r0, rc), :] = acc

    cur = lambda blk: pl.BlockSpec((T, CONV_CH), lambda i: (i, blk))
    prv = lambda blk: pl.BlockSpec((T, CONV_CH), lambda i: (jnp.maximum(i - 1, 0), blk))
    return pl.pallas_call(
        body, name=name, grid=(nT,),
        in_specs=[cur(CONV_A_BLK), cur(CONV_G_BLK), prv(CONV_A_BLK), prv(CONV_G_BLK),
                  pl.BlockSpec(dw.shape, lambda i: (0, 0))],
        out_specs=pl.BlockSpec((T, CONV_CH), lambda i: (i, 0)),
        out_shape=jax.ShapeDtypeStruct((S, CONV_CH), f32),
        scratch_shapes=[pltpu.VMEM((T + CONV_HALO + SUBLANES, CONV_CH), f32),
                        pltpu.VMEM((SUBLANES, T + CONV_HALO, CONV_CH), f32)],
        compiler_params=_params(("arbitrary",)))(u, u, u, u, dw)


def conv_bwd(name, u, dy, dw):
    S = u.shape[0]
    T = min(512, S)
    nT = S // T
    lead = CONV_HALO - (CONV_WIDTH - 1)

    def body(a_ref, g_ref, ap_ref, gp_ref, dy_ref, dyn_ref, dw_ref, du_ref, ddw_ref, ext_ref, dext_ref, ext8_ref, dext8_ref):
        i = pl.program_id(0)
        prev = _glu(ap_ref[T - CONV_HALO:, :], gp_ref[T - CONV_HALO:, :])
        ext_ref[:CONV_HALO, :] = jnp.where(i > 0, prev, 0.0)
        ext_ref[pl.ds(CONV_HALO, T), :] = _glu(a_ref[...], g_ref[...])
        dext_ref[:T, :] = dy_ref[...]
        dext_ref[pl.ds(T, CONV_HALO), :] = jnp.where(i < nT - 1, dyn_ref[:CONV_HALO, :], 0.0)
        _phase_copies(ext_ref, ext8_ref, T + CONV_HALO)
        _phase_copies(dext_ref, dext8_ref, T + CONV_HALO)

        @pl.when(i == 0)
        def _():
            ddw_ref[...] = jnp.zeros_like(ddw_ref)

        rc = min(CONV_ROWS, T)
        for r0 in range(0, T, rc):
            dglu = jnp.zeros((rc, CONV_CH), f32)
            for w in range(CONV_WIDTH):
                dglu = dglu + dw_ref[w:w + 1, :] * _window(dext8_ref, r0 + CONV_WIDTH - 1 - w, rc)
            a, sg = a_ref[pl.ds(r0, rc), :].astype(f32), _sigmoid(g_ref[pl.ds(r0, rc), :].astype(f32))
            du_ref[pl.ds(r0, rc), :CONV_CH] = (dglu * sg).astype(du_ref.dtype)
            du_ref[pl.ds(r0, rc), CONV_CH:] = (dglu * a * sg * (1.0 - sg)).astype(du_ref.dtype)
        rc = min(CONV_ROWS // 2, T)
        for r0 in range(0, T, rc):
            dyv = dy_ref[pl.ds(r0, rc), :]
            for w in range(CONV_WIDTH):
                ddw_ref[w:w + 1, :] += jnp.sum(dyv * _window(ext8_ref, r0 + w + lead, rc), axis=0, keepdims=True)

    cur = lambda blk: pl.BlockSpec((T, CONV_CH), lambda i: (i, blk))
    prv = lambda blk: pl.BlockSpec((T, CONV_CH), lambda i: (jnp.maximum(i - 1, 0), blk))
    return pl.pallas_call(
        body, name=name, grid=(nT,),
        in_specs=[cur(CONV_A_BLK), cur(CONV_G_BLK), prv(CONV_A_BLK), prv(CONV_G_BLK),
                  pl.BlockSpec((T, CONV_CH), lambda i: (i, 0)),
                  pl.BlockSpec((T, CONV_CH), lambda i: (jnp.minimum(i + 1, nT - 1), 0)),
                  pl.BlockSpec(dw.shape, lambda i: (0, 0))],
        out_specs=[pl.BlockSpec((T, 2 * CONV_CH), lambda i: (i, 0)), pl.BlockSpec(dw.shape, lambda i: (0, 0))],
        out_shape=[jax.ShapeDtypeStruct((S, 2 * CONV_CH), bf16), jax.ShapeDtypeStruct(dw.shape, f32)],
        scratch_shapes=[pltpu.VMEM((T + CONV_HALO + SUBLANES, CONV_CH), f32)] * 2
        + [pltpu.VMEM((SUBLANES, T + CONV_HALO, CONV_CH), f32)] * 2,
        compiler_params=_params(("arbitrary",)))(u, u, u, u, dy, dy, dw)


def rope_tables(pos_col):
    S = pos_col.shape[0]
    T = min(512, S)
    inv_freq = ROPE_BASE ** (-jnp.arange(0, MLA_ROPE, 2, dtype=f32) / MLA_ROPE)
    zeros = jnp.zeros((MLA_NOPE,), f32)
    inv_row = jnp.concatenate([zeros, inv_freq, inv_freq, zeros[:MLA_PAD - MLA_QK]]).reshape(1, MLA_PAD)

    def body(p_ref, f_ref, c_ref, s_ref):
        lane = lax.broadcasted_iota(jnp.int32, (T, MLA_PAD), 1)
        ang = p_ref[...].astype(f32) * f_ref[...]
        rot = jnp.logical_and(lane >= MLA_NOPE, lane < MLA_QK)
        c_ref[...] = jnp.where(rot, jnp.cos(ang), jnp.where(lane < MLA_NOPE, 1.0, 0.0))
        s_ref[...] = jnp.where(rot, jnp.sin(ang), 0.0)

    spec = pl.BlockSpec((T, MLA_PAD), lambda i: (i, 0))
    return pl.pallas_call(
        body, name="rope_tables", grid=(S // T,),
        in_specs=[pl.BlockSpec((T, 1), lambda i: (i, 0)), pl.BlockSpec((1, MLA_PAD), lambda i: (0, 0))],
        out_specs=[spec, spec], out_shape=[jax.ShapeDtypeStruct((S, MLA_PAD), f32)] * 2,
        compiler_params=_params(("parallel",)))(pos_col, inv_row)


def loss_head(y, target):
    S, D = y.shape
    T = min(512, S)

    def body(y_ref, t_ref, dy_ref, l_ref):
        i = pl.program_id(0)
        err = y_ref[...] - t_ref[...]
        dy_ref[...] = err * (1.0 / D)
        part = 0.5 * jnp.sum(jnp.sum(err * err, axis=-1, keepdims=True) * (1.0 / D), axis=0, keepdims=True)
        part = jnp.broadcast_to(part, l_ref.shape)

        @pl.when(i == 0)
        def _():
            l_ref[...] = part

        @pl.when(i > 0)
        def _():
            l_ref[...] += part

    spec = pl.BlockSpec((T, D), lambda i: (i, 0))
    return pl.pallas_call(
        body, name="loss_head", grid=(S // T,), in_specs=[spec, spec],
        out_specs=[spec, pl.BlockSpec((8, 128), lambda i: (0, 0))],
        out_shape=[jax.ShapeDtypeStruct((S, D), f32), jax.ShapeDtypeStruct((8, 128), f32)],
        compiler_params=_params(("arbitrary",)))(y, target)


def _row(a, w=None, off=0, st=0):
    return (a, w or a.shape[1], off, st)


def _out(tw, dt, w=None, off=0, st=0):
    return (tw, dt, w or tw, off, st)


ACT = bf16


def ffn_fwd(tag, x, g, w_in, w_out):
    (h,) = rowwise(f"{tag}_rms", fn_rms, [_row(x)], [g], [_out(D_MODEL, bf16)])
    u = mm(f"{tag}_in", h, w_in, "nn", out_dtype=ACT)
    (a,) = rowwise(f"{tag}_swiglu", fn_swiglu, [_row(u, FFN_HIDDEN, 0), _row(u, FFN_HIDDEN, 1)], [],
                   [_out(FFN_HIDDEN, bf16)])
    y = mm(f"{tag}_out", a, w_out, "nn", alpha=0.5, res=x)
    return y, (x, h, u, a)


def ffn_bwd(tag, saved, g, w_in, w_out, dy):
    x, h, u, a = saved
    d_w_out = mm(f"{tag}_dwout", a, dy, "tn", alpha=0.5)
    da = mm(f"{tag}_da", dy, w_out, "nt", alpha=0.5, out_dtype=ACT)
    (du,), _ = rowwise_bwd(f"{tag}_dswiglu", fn_swiglu, [_row(u, FFN_HIDDEN, 0), _row(u, FFN_HIDDEN, 1)], [],
                           [_row(da)], [_out(2 * FFN_HIDDEN, bf16)], merge=[(0, 1)], T=256)
    d_w_in = mm(f"{tag}_dwin", h, du, "tn")
    dh = mm(f"{tag}_dh", du, w_in, "nt", out_dtype=ACT)
    (dx,), (dg,) = rowwise_bwd(f"{tag}_drms", fn_rms_res, [_row(x)], [g], [_row(dy), _row(dh)], [_out(D_MODEL, f32)])
    return dx, dg, d_w_in, d_w_out


def _seg(name):
    for n, _, w, start in _U_SEGS:
        if n == name:
            return start, w
    raise KeyError(name)


def mix_fwd(tag, x, mem_n_in, tabs, p, ride=None):
    cos_t, sin_t = tabs
    (h,) = rowwise(f"{tag}_rms", fn_rms, [_row(x)], [p["mix_norm"]], [_out(D_MODEL, bf16)])
    u = mm(f"{tag}_in", h, p["w_in"], "nn", out_dtype=ACT)
    yc = conv_fwd(f"{tag}_conv", u, p["conv_dw"])
    (br_a,) = rowwise(f"{tag}_lnsilu", fn_ln_silu, [_row(yc)], [p["conv_b"], p["conv_ln_g"], p["conv_ln_b"]],
                      [_out(BRANCH_WIDTH, bf16)])
    sb0 = _seg("sb")[0] // HEAD_DIM
    qs, ks = rowwise(f"{tag}_sbprep", fn_sb_prep, [_row(u, HEAD_DIM, sb0, 1), _row(u, HEAD_DIM, sb0 + HEADS, 1)],
                     [p["sb_q_hnorm"], p["sb_k_hnorm"]],
                     [_out(BRANCH_WIDTH, bf16, HEAD_DIM, 0, 1), _out(BRANCH_WIDTH, bf16, HEAD_DIM, 0, 1)], H=HEADS)
    sb_v = _row(u, HEAD_DIM, sb0 + 2 * HEADS, 1)
    br_b, tot_b, cnt_b = sb_fwd(f"{tag}_sb", _row(qs, HEAD_DIM, 0, 1), _row(ks, HEAD_DIM, 0, 1), sb_v)
    ql_n, kvl_n = rowwise(f"{tag}_latrms", lambda a, b, ga, gb: (_rms(a, ga), _rms(b, gb)),
                          [_row(u, MLA_Q_LORA, _seg("qlat")[0] // MLA_Q_LORA), _row(u, MLA_KV_LORA, _seg("kvlat")[0] // MLA_KV_LORA)],
                          [p["mla_q_norm"], p["mla_kv_norm"]], [_out(MLA_Q_LORA, bf16), _out(MLA_KV_LORA, bf16)])
    qfull = mm(f"{tag}_uq", ql_n, p["mla_w_uq"], "nn", out_dtype=ACT)
    kvfull = mm(f"{tag}_ukv", kvl_n, p["mla_w_ukv"], "nn", out_dtype=ACT)
    kr_row = _row(u, HEAD_DIM, _seg("krope")[0] // HEAD_DIM, 0)
    (qr,) = rowwise(f"{tag}_mlaq", fn_mla_q, [_row(qfull, MLA_PAD, 0, 1), _row(cos_t), _row(sin_t)], [p["mla_q_hnorm"]],
                    [_out(HEADS * MLA_PAD, bf16, MLA_PAD, 0, 1)], H=HEADS)
    (kr,) = rowwise(f"{tag}_mlak", fn_mla_k, [_row(kvfull, HEAD_DIM, 0, 2), kr_row, _row(cos_t), _row(sin_t)],
                    [p["mla_k_hnorm"]], [_out(HEADS * MLA_PAD, bf16, MLA_PAD, 0, 1)], H=HEADS)
    mla_v = _row(kvfull, HEAD_DIM, 1, 2)
    br_c, lse_c, *landed = attn_fwd(f"{tag}_mla", _row(qr, MLA_PAD, 0, 1), _row(kr, MLA_PAD, 0, 1), mla_v,
                                    scale=MLA_QK ** -0.5, mask="chunk", ride=ride)
    (mem_n,) = rowwise(f"{tag}_memrms", fn_rms, [_row(mem_n_in)], [p["mem_norm"]], [_out(D_MODEL, bf16)])
    kvm = mm(f"{tag}_memkv", mem_n, p["mem_w_kv"], "nn", out_dtype=ACT)
    (km,) = rowwise(f"{tag}_memk", fn_rms, [_row(kvm, HEAD_DIM, 0, 1)], [p["mem_k_hnorm"]],
                    [_out(BRANCH_WIDTH, bf16, HEAD_DIM, 0, 1)], H=HEADS)
    mq0 = _seg("memq")[0] // HEAD_DIM
    (qm,) = rowwise(f"{tag}_memq", fn_rms, [_row(u, HEAD_DIM, mq0, 1)], [p["mem_q_hnorm"]],
                    [_out(BRANCH_WIDTH, bf16, HEAD_DIM, 0, 1)], H=HEADS)
    mem_v = _row(kvm, HEAD_DIM, HEADS, 1)
    br_d, lse_d = attn_fwd(f"{tag}_memattn", _row(qm, HEAD_DIM, 0, 1), _row(km, HEAD_DIM, 0, 1), mem_v,
                           scale=HEAD_DIM ** -0.5, mask=None)
    branches = (br_a, br_b, br_c, br_d)
    proj = [mm(f"{tag}_branch{b}", branches[b], p["w_branch"][b], "nn", out_dtype=ACT) for b in range(N_BRANCH)]
    gate_rows = [_row(u, D_MODEL, b) for b in range(N_BRANCH)]
    (merged,) = rowwise(f"{tag}_merge", fn_merge, gate_rows + [_row(t) for t in proj], [], [_out(D_MODEL, bf16)], T=256)
    y = mm(f"{tag}_out", merged, p["w_out"], "nn", res=x)
    saved = dict(x=x, h=h, u=u, yc=yc, qs=qs, ks=ks, ql_n=ql_n, kvl_n=kvl_n, qfull=qfull, kvfull=kvfull, qr=qr, kr=kr,
                 lse_c=lse_c, mem_n=mem_n, kvm=kvm, km=km, qm=qm, lse_d=lse_d, branches=branches, proj=proj,
                 merged=merged, tot_b=tot_b, cnt_b=cnt_b)
    return y, saved, (landed[0] if landed else None)


def mix_bwd(tag, sv, mem_n_in, tabs, p, dy, reducer=None):
    cos_t, sin_t = tabs
    u, S = sv["u"], sv["u"].shape[0]
    g = {}
    g["w_out"] = mm(f"{tag}_dwout", sv["merged"], dy, "tn")
    dmerged = mm(f"{tag}_dmerged", dy, p["w_out"], "nt", out_dtype=ACT)
    gate_rows = [_row(u, D_MODEL, b) for b in range(N_BRANCH)]
    d_merge, _ = rowwise_bwd(f"{tag}_dmerge", fn_merge, gate_rows + [_row(t) for t in sv["proj"]], [], [_row(dmerged)],
                             [_out(N_BRANCH * D_MODEL, bf16)] + [_out(D_MODEL, bf16)] * N_BRANCH, T=256,
                             merge=[tuple(range(N_BRANCH))] + [(N_BRANCH + b,) for b in range(N_BRANCH)])
    d_gates, d_proj = d_merge[:1], d_merge[1:]
    g["w_branch"] = [mm(f"{tag}_dwbranch{b}", sv["branches"][b], d_proj[b], "tn") for b in range(N_BRANCH)]
    d_br = [mm(f"{tag}_dbranch{b}", d_proj[b], p["w_branch"][b], "nt", out_dtype=ACT) for b in range(N_BRANCH)]
    (dyc,), (g["conv_b"], g["conv_ln_g"], g["conv_ln_b"]) = rowwise_bwd(
        f"{tag}_dlnsilu", fn_ln_silu, [_row(sv["yc"])], [p["conv_b"], p["conv_ln_g"], p["conv_ln_b"]], [_row(d_br[0])],
        [_out(BRANCH_WIDTH, f32)])
    du_conv, g["conv_dw"] = conv_bwd(f"{tag}_dconv", u, dyc, p["conv_dw"])
    sb0 = _seg("sb")[0] // HEAD_DIM
    sb_v = _row(u, HEAD_DIM, sb0 + 2 * HEADS, 1)
    dqs, dks, dv_sb, *landed = sb_bwd(f"{tag}_dsb", _row(sv["qs"], HEAD_DIM, 0, 1), _row(sv["ks"], HEAD_DIM, 0, 1), sb_v,
                                      sv["tot_b"], sv["cnt_b"], d_br[1],
                                      ride=("sibling", reducer.sibling_src()) if reducer else None)
    scatter_src = reducer.after_sibling(landed[0]) if reducer else None
    (du_sbq, du_sbk), (g["sb_q_hnorm"], g["sb_k_hnorm"]) = rowwise_bwd(
        f"{tag}_dsbprep", fn_sb_prep, [_row(u, HEAD_DIM, sb0, 1), _row(u, HEAD_DIM, sb0 + HEADS, 1)],
        [p["sb_q_hnorm"], p["sb_k_hnorm"]], [_row(dqs, HEAD_DIM, 0, 1), _row(dks, HEAD_DIM, 0, 1)],
        [_out(BRANCH_WIDTH, bf16, HEAD_DIM, 0, 1), _out(BRANCH_WIDTH, bf16, HEAD_DIM, 0, 1)], H=HEADS)
    mla_v = _row(sv["kvfull"], HEAD_DIM, 1, 2)
    dqr, dkr, dv_mla, *landed = attn_bwd(f"{tag}_dmla", _row(sv["qr"], MLA_PAD, 0, 1), _row(sv["kr"], MLA_PAD, 0, 1), mla_v,
                                         sv["branches"][2], d_br[2], sv["lse_c"], scale=MLA_QK ** -0.5, mask="chunk",
                                         ride=("scatter", scatter_src) if reducer else None)
    if reducer:
        reducer.after_scatter(landed[0])
    (dqfull,), (g["mla_q_hnorm"],) = rowwise_bwd(
        f"{tag}_dmlaq", fn_mla_q, [_row(sv["qfull"], MLA_PAD, 0, 1), _row(cos_t), _row(sin_t)], [p["mla_q_hnorm"]],
        [_row(dqr, MLA_PAD, 0, 1)], [_out(HEADS * MLA_PAD, bf16, MLA_PAD, 0, 1)], H=HEADS, nondiff=(1, 2))
    kr_row = _row(u, HEAD_DIM, _seg("krope")[0] // HEAD_DIM, 0)
    (dkn, du_krope, dvp), (g["mla_k_hnorm"],) = rowwise_bwd(
        f"{tag}_dmlak", fn_mla_k_v, [_row(sv["kvfull"], HEAD_DIM, 0, 2), kr_row, mla_v, _row(cos_t), _row(sin_t)],
        [p["mla_k_hnorm"]], [_row(dkr, MLA_PAD, 0, 1), _row(dv_mla, HEAD_DIM, 0, 1)],
        [_out(BRANCH_WIDTH, bf16, HEAD_DIM, 0, 1), _out(HEAD_DIM, f32), _out(BRANCH_WIDTH, bf16, HEAD_DIM, 0, 1)],
        H=HEADS, nondiff=(3, 4))
    dkvfull = _interleave(f"{tag}_dkvfull", dkn, dvp)
    g["mla_w_uq"] = mm(f"{tag}_dwuq", sv["ql_n"], dqfull, "tn")
    g["mla_w_ukv"] = mm(f"{tag}_dwukv", sv["kvl_n"], dkvfull, "tn")
    dql_n = mm(f"{tag}_dqln", dqfull, p["mla_w_uq"], "nt", out_dtype=ACT)
    dkvl_n = mm(f"{tag}_dkvln", dkvfull, p["mla_w_ukv"], "nt", out_dtype=ACT)
    (du_lat,), (g["mla_q_norm"], g["mla_kv_norm"]) = rowwise_bwd(
        f"{tag}_dlatrms", lambda a, b, ga, gb: (_rms(a, ga), _rms(b, gb)),
        [_row(u, MLA_Q_LORA, _seg("qlat")[0] // MLA_Q_LORA), _row(u, MLA_KV_LORA, _seg("kvlat")[0] // MLA_KV_LORA)],
        [p["mla_q_norm"], p["mla_kv_norm"]], [_row(dql_n), _row(dkvl_n)], [_out(MLA_Q_LORA + MLA_KV_LORA, bf16)],
        merge=[(0, 1)])
    mem_v = _row(sv["kvm"], HEAD_DIM, HEADS, 1)
    dqm, dkm, dvm = attn_bwd(f"{tag}_dmemattn", _row(sv["qm"], HEAD_DIM, 0, 1), _row(sv["km"], HEAD_DIM, 0, 1), mem_v,
                             sv["branches"][3], d_br[3], sv["lse_d"], scale=HEAD_DIM ** -0.5, mask=None)
    mq0 = _seg("memq")[0] // HEAD_DIM
    (du_memq,), (g["mem_q_hnorm"],) = rowwise_bwd(
        f"{tag}_dmemq", fn_rms, [_row(u, HEAD_DIM, mq0, 1)], [p["mem_q_hnorm"]], [_row(dqm, HEAD_DIM, 0, 1)],
        [_out(BRANCH_WIDTH, bf16, HEAD_DIM, 0, 1)], H=HEADS)
    (dkvm_k, dkvm_v), (g["mem_k_hnorm"],) = rowwise_bwd(
        f"{tag}_dmemk", fn_mem_k_v, [_row(sv["kvm"], HEAD_DIM, 0, 1), mem_v], [p["mem_k_hnorm"]],
        [_row(dkm, HEAD_DIM, 0, 1), _row(dvm, HEAD_DIM, 0, 1)],
        [_out(BRANCH_WIDTH, bf16, HEAD_DIM, 0, 1), _out(BRANCH_WIDTH, bf16, HEAD_DIM, 0, 1)], H=HEADS)
    dkvm = jnp.concatenate([dkvm_k, dkvm_v], axis=1)
    g["mem_w_kv"] = mm(f"{tag}_dwmemkv", sv["mem_n"], dkvm, "tn")
    dmem_n = mm(f"{tag}_dmemn", dkvm, p["mem_w_kv"], "nt", out_dtype=ACT)
    _, (g["mem_norm"],) = rowwise_bwd(f"{tag}_dmemrms", fn_rms, [_row(mem_n_in)], [p["mem_norm"]], [_row(dmem_n)],
                                      [_out(D_MODEL, bf16)])
    du_krope_b = du_krope.astype(bf16)
    du = jnp.concatenate(list(d_gates) + [du_conv, du_sbq, du_sbk, dv_sb.astype(bf16), du_lat, du_memq,
                                          du_krope_b, jnp.zeros((S, U_WIDTH - _seg("krope")[0] - HEAD_DIM), bf16)], axis=1)
    g["w_in"] = mm(f"{tag}_dwin", sv["h"], du, "tn")
    dh = mm(f"{tag}_dh", du, p["w_in"], "nt", out_dtype=ACT)
    (dx,), (g["mix_norm"],) = rowwise_bwd(f"{tag}_drms", fn_rms_res, [_row(sv["x"])], [p["mix_norm"]],
                                          [_row(dy), _row(dh)], [_out(D_MODEL, f32)])
    return dx, g


def _interleave(name, a, b):
    S, W = a.shape
    T = min(512, S)

    def body(a_ref, b_ref, o_ref):
        o_ref[:, :HEAD_DIM] = a_ref[...]
        o_ref[:, HEAD_DIM:] = b_ref[...]

    blk = pl.BlockSpec((T, HEAD_DIM), lambda i, h: (i, h))
    return pl.pallas_call(
        body, name=name, grid=(S // T, W // HEAD_DIM), in_specs=[blk, blk],
        out_specs=pl.BlockSpec((T, 2 * HEAD_DIM), lambda i, h: (i, h)),
        out_shape=jax.ShapeDtypeStruct((S, 2 * W), a.dtype), compiler_params=_params(("parallel", "parallel")))(a, b)


def _u_layout(w):
    parts, at = [], 0
    for _, src, width, start in _U_SEGS:
        assert start == at
        parts.append(w[..., src:src + width])
        at += width
    parts.append(jnp.zeros(w.shape[:-1] + (U_WIDTH - at,), w.dtype))
    return jnp.concatenate(parts, axis=-1)


def _u_layout_inv(g):
    order = sorted(_U_SEGS, key=lambda s: s[1])
    return jnp.concatenate([g[..., start:start + width] for _, _, width, start in order], axis=-1)


def _pad_heads(w, n=MLA_QK, to=MLA_PAD):
    w = w.reshape(w.shape[:-1] + (HEADS, n))
    w = jnp.pad(w, [(0, 0)] * (w.ndim - 1) + [(0, to - n)])
    return w.reshape(w.shape[:-2] + (HEADS * to,))


def _unpad_heads(g, n=MLA_QK, to=MLA_PAD):
    g = g.reshape(g.shape[:-1] + (HEADS, to))[..., :n]
    return g.reshape(g.shape[:-2] + (HEADS * n,))


def layer_params(W):
    row = lambda name: W[name].reshape(1, -1).astype(f32)
    p = {n: row(n) for n in SMALL if n != "mla_q_hnorm" and n != "mla_k_hnorm"}
    for n in ("mla_q_hnorm", "mla_k_hnorm"):
        p[n] = jnp.pad(row(n), ((0, 0), (0, MLA_PAD - MLA_QK)))
    for n in ("ffn1_w_in", "ffn1_w_out", "ffn2_w_in", "ffn2_w_out", "mla_w_ukv", "mem_w_kv", "w_out"):
        p[n] = W[n]
    p["w_branch"] = [W["w_branch"][b] for b in range(N_BRANCH)]
    p["w_in"] = _u_layout(W["w_in"])
    p["mla_w_uq"] = _pad_heads(W["mla_w_uq"])
    p["conv_dw"] = jnp.pad(W["conv_dw"].astype(f32), ((0, 1), (0, 0)))
    return p


def layer_grads_to_original(g):
    out = dict(g)
    out["w_in"] = _u_layout_inv(g["w_in"])
    out["mla_w_uq"] = _unpad_heads(g["mla_w_uq"])
    out["conv_dw"] = g["conv_dw"][:CONV_WIDTH]
    out["w_branch"] = jnp.stack(g["w_branch"])
    for n in ("mla_q_hnorm", "mla_k_hnorm"):
        out[n] = g[n][:, :MLA_QK]
    return {n: (out[n].reshape(-1) if n in SMALL else out[n]) for n in out}


def local_step(x, mem, pos_col, target, weights_of, ride_of=lambda l: None, deliver=lambda l, landed: None,
               reducer_of=lambda l, grads: None):
    tabs = rope_tables(pos_col)
    params, saved = [], []
    for l in range(DEPTH):
        p = layer_params(weights_of(l))
        params.append(p)
        x, s1 = ffn_fwd(f"l{l}_ffn1", x, p["ffn1_norm"], p["ffn1_w_in"], p["ffn1_w_out"])
        x, s2, landed = mix_fwd(f"l{l}_mix", x, mem, tabs, p, ride=ride_of(l))
        deliver(l, landed)
        x, s3 = ffn_fwd(f"l{l}_ffn2", x, p["ffn2_norm"], p["ffn2_w_in"], p["ffn2_w_out"])
        saved.append((s1, s2, s3))
    dx, loss_blk = loss_head(x, target)
    grads = [None] * DEPTH
    reducer = None
    for l in reversed(range(DEPTH)):
        p, (s1, s2, s3) = params[l], saved[l]
        dx, g_n2, g_in2, g_out2 = ffn_bwd(f"l{l}_ffn2", s3, p["ffn2_norm"], p["ffn2_w_in"], p["ffn2_w_out"], dx)
        dx, g = mix_bwd(f"l{l}_mix", s2, mem, tabs, p, dx, reducer)
        dx, g_n1, g_in1, g_out1 = ffn_bwd(f"l{l}_ffn1", s1, p["ffn1_norm"], p["ffn1_w_in"], p["ffn1_w_out"], dx)
        g.update(ffn1_norm=g_n1, ffn1_w_in=g_in1, ffn1_w_out=g_out1, ffn2_norm=g_n2, ffn2_w_in=g_in2, ffn2_w_out=g_out2)
        grads[l] = layer_grads_to_original(g)
        reducer = reducer_of(l, grads[l])
    if reducer:
        reducer.alone()
    return loss_blk, dx, grads


_ANY = pl.BlockSpec(memory_space=pl.ANY)
_COMM = pltpu.CompilerParams(has_side_effects=True)


def _coords():
    return lax.axis_index("x"), lax.axis_index("y"), lax.axis_index("c")


def chip_exchange(name, src, scatter):
    shape = src.shape[1:] if scatter else src.shape

    def body(src_ref, out_ref, send_sems, recv_sems):
        x, y, c = _coords()
        copies = []
        for k, (px, py) in enumerate([(1 - x, y), (x, 1 - y), (1 - x, 1 - y)]):
            piece = src_ref.at[2 * px + py] if scatter else src_ref
            cp = pltpu.make_async_remote_copy(piece, out_ref.at[k], send_sems.at[k], recv_sems.at[k],
                                              device_id=(px, py, c), device_id_type=MESH)
            cp.start()
            copies.append(cp)
        for cp in copies:
            cp.wait()

    return pl.pallas_call(
        body, name=name, in_specs=[_ANY], out_specs=_ANY, out_shape=jax.ShapeDtypeStruct((3,) + shape, src.dtype),
        scratch_shapes=[pltpu.SemaphoreType.DMA((3,)), pltpu.SemaphoreType.DMA((3,))], compiler_params=_COMM)(src)


def sibling_exchange(name, src):
    def body(src_ref, out_ref, send_sem, recv_sem):
        x, y, c = _coords()
        cp = pltpu.make_async_remote_copy(src_ref, out_ref, send_sem, recv_sem, device_id=(x, y, 1 - c),
                                          device_id_type=MESH)
        cp.start()
        cp.wait()

    return pl.pallas_call(
        body, name=name, in_specs=[_ANY], out_specs=_ANY, out_shape=jax.ShapeDtypeStruct(src.shape, src.dtype),
        scratch_shapes=[pltpu.SemaphoreType.DMA, pltpu.SemaphoreType.DMA], compiler_params=_COMM)(src)


def all8_gather(name, src):
    def body(src_ref, out_ref, send_sems, recv_sems, local_sem):
        x, y, c = _coords()
        me = 4 * x + 2 * y + c
        mine = pltpu.make_async_copy(src_ref, out_ref.at[me], local_sem)
        mine.start()
        copies = []
        for k in range(1, 8):
            peer = (1 - x if k & 4 else x, 1 - y if k & 2 else y, 1 - c if k & 1 else c)
            cp = pltpu.make_async_remote_copy(src_ref, out_ref.at[me], send_sems.at[k - 1], recv_sems.at[k - 1],
                                              device_id=peer, device_id_type=MESH)
            cp.start()
            copies.append(cp)
        for cp in copies:
            cp.wait()
        mine.wait()

    return pl.pallas_call(
        body, name=name, in_specs=[_ANY], out_specs=_ANY, out_shape=jax.ShapeDtypeStruct((8,) + src.shape, src.dtype),
        scratch_shapes=[pltpu.SemaphoreType.DMA((7,)), pltpu.SemaphoreType.DMA((7,)), pltpu.SemaphoreType.DMA],
        compiler_params=_COMM)(src)


def sum8(name, g):
    def body(g_ref, o_ref):
        acc = g_ref[0]
        for k in range(1, 8):
            acc = acc + g_ref[k]
        o_ref[...] = acc

    return pl.pallas_call(body, name=name, out_shape=jax.ShapeDtypeStruct(g.shape[1:], g.dtype))(g)


PACK_COLS = 1024
PACKED = tuple(n for n in SHARDED if n != "conv_dw")
PACK_ROW_GRAIN = PACK_GRAIN // PACK_COLS


def _rows(shape):
    n = math.prod(shape)
    assert n % (16 * PACK_COLS) == 0, shape
    return n // PACK_COLS


def _pack_rows(pieces):
    rows = sum(p.shape[0] for p in pieces)
    pad = -rows % PACK_ROW_GRAIN
    if pad:
        pieces = pieces + [jnp.zeros((pad, PACK_COLS), pieces[0].dtype)]
    return jnp.concatenate(pieces, axis=0)


def _unpack_rows(packed, shard_shapes):
    out, at = {}, 0
    for n in PACKED:
        r = _rows(shard_shapes[n])
        out[n] = packed[at:at + r].reshape(shard_shapes[n])
        at += r
    return out


def gather_shards(name, own):
    n_chunks = 4
    rows = own.shape[1] // n_chunks
    assert own.shape[1] % (16 * n_chunks) == 0

    def body(own_ref, out_ref, send_sems, recv_sems, local_sem):
        x, y, c = _coords()
        sib = (x, y, 1 - c)
        mine = pltpu.make_async_copy(own_ref, out_ref.at[0], local_sem)
        mine.start()
        chips = [(2, (1 - x, y)), (1, (x, 1 - y)), (3, (1 - x, 1 - y))]

        def copy(k, j, src, m, half, to):
            sl = pl.ds(j * rows, rows)
            return pltpu.make_async_remote_copy(src.at[sl], out_ref.at[m, half, sl], send_sems.at[k * n_chunks + j],
                                                recv_sems.at[k * n_chunks + j], device_id=to, device_id_type=MESH)

        first = [[copy(k, j, own_ref.at[c], m, c, (px, py, c)) for j in range(n_chunks)]
                 for k, (m, (px, py)) in enumerate(chips)]
        for j in range(n_chunks):
            for k in range(3):
                first[k][j].start()
        passed = []
        for j in range(n_chunks):
            for k, (m, _) in enumerate(chips):
                first[k][j].wait_recv()
                cp = copy(3 + k, j, out_ref.at[m, c], m, c, sib)
                cp.start()
                passed.append(cp)
        for j in range(n_chunks):
            for k, (m, _) in enumerate(chips):
                copy(3 + k, j, out_ref.at[m, 1 - c], m, 1 - c, sib).wait_recv()
        for cp in [cp for per_chip in first for cp in per_chip] + passed:
            cp.wait_send()
        mine.wait()

    return pl.pallas_call(
        body, name=name, in_specs=[_ANY], out_specs=_ANY, out_shape=jax.ShapeDtypeStruct((4,) + own.shape, own.dtype),
        scratch_shapes=[pltpu.SemaphoreType.DMA((6 * n_chunks,)), pltpu.SemaphoreType.DMA((6 * n_chunks,)),
                        pltpu.SemaphoreType.DMA],
        compiler_params=_COMM)(own)


def pack_layer_shard(w, l):
    return _pack_rows([w[n][l].astype(bf16).reshape(-1, PACK_COLS) for n in PACKED]).reshape(2, -1, PACK_COLS)


def unpack_layer(by_mask, w):
    x, y, _ = _coords()
    me = 2 * x + y
    shapes = {n: w[n].shape[1:] for n in PACKED}
    pieces = [_unpack_rows(lax.dynamic_index_in_dim(by_mask, jnp.bitwise_xor(s, me), axis=0, keepdims=False)
                           .reshape(-1, PACK_COLS), shapes) for s in range(4)]
    return {n: jnp.concatenate([pieces[s][n] for s in range(4)], axis=SHARD_AXIS[n] - 1) for n in PACKED}


def gather_conv_dw(dw):
    rows = math.prod(dw.shape[:-1])
    every = all8_gather("ag_conv_dw", jnp.pad(dw.reshape(rows, -1), ((0, -rows % 8), (0, 0))))
    return jnp.concatenate([every[2 * s, :rows].reshape(dw.shape) for s in range(4)], axis=SHARD_AXIS["conv_dw"])


def _add_streams(name, ins, selectors, out_dtypes, rows, T=256):
    n_streams = max([a.shape[sel.index("s")] for a, sel in zip(ins, selectors) if "s" in sel] + [1])

    def spec(sel):
        def index(s, i, pf):
            lead = tuple(s if e == "s" else (pf[e[1]] if isinstance(e, tuple) else e) for e in sel)
            return lead + (i, 0)
        return pl.BlockSpec((None,) * len(sel) + (T, PACK_COLS), index)

    def body(pf_ref, *refs):
        acc = refs[0][...].astype(f32)
        for r in refs[1:len(ins)]:
            acc = acc + r[...].astype(f32)
        for o in refs[len(ins):]:
            o[...] = acc.astype(o.dtype)

    def run(pf):
        grid_spec = pltpu.PrefetchScalarGridSpec(
            num_scalar_prefetch=1, grid=(n_streams, rows // T), in_specs=[spec(sel) for sel in selectors],
            out_specs=[spec(("s",)) for _ in out_dtypes])
        return pl.pallas_call(
            body, name=name, grid_spec=grid_spec,
            out_shape=[jax.ShapeDtypeStruct((n_streams, rows, PACK_COLS), dt) for dt in out_dtypes],
            compiler_params=_params(("parallel", "parallel")))(pf, *ins)
    return run


class LayerReduce:
    def __init__(self, tag, grads):
        x, y, c = _coords()
        self.tag, self.c = tag, c
        streams = []
        for s in range(4):
            pieces = []
            for n in PACKED:
                ax = SHARD_AXIS[n] - 1
                width = grads[n].shape[ax] // 4
                pieces.append(lax.slice_in_dim(grads[n], s * width, (s + 1) * width, axis=ax).reshape(-1, PACK_COLS))
            streams.append(_pack_rows(pieces))
        self.rows = streams[0].shape[0] // 2
        self.G = jnp.concatenate(streams, axis=0).reshape(4, 2, self.rows, PACK_COLS)
        self.pf = jnp.stack([c, 2 * x + y]).astype(jnp.int32)

    def sibling_src(self):
        return lax.dynamic_index_in_dim(self.G, 1 - self.c, axis=1, keepdims=False).astype(bf16)

    def after_sibling(self, from_sib):
        self.chip_sum, chip_sum_b = _add_streams(f"{self.tag}_add_sibling", [self.G, from_sib],
                                                 [("s", ("pf", 0)), ("s",)], [f32, bf16], self.rows)(self.pf)
        return chip_sum_b

    def after_scatter(self, got):
        (half,) = _add_streams(f"{self.tag}_add_chips", [self.chip_sum, got, got, got],
                               [(("pf", 1),), (0,), (1,), (2,)], [f32], self.rows)(self.pf)
        self.half = half[0]

    def alone(self):
        chip_sum_b = self.after_sibling(sibling_exchange(f"{self.tag}_sibling", self.sibling_src()))
        self.after_scatter(chip_exchange(f"{self.tag}_chips", chip_sum_b, scatter=True))


def finish_reduce(reducers, w):
    c = reducers[0].c
    mine = jnp.concatenate([r.half for r in reducers], axis=0)
    other = sibling_exchange("rs_final", mine)
    lower, upper = jnp.where(c == 0, mine, other), jnp.where(c == 0, other, mine)
    shapes = {n: w[n].shape[1:] for n in PACKED}
    per_layer, at = [], 0
    for r in reducers:
        shard = jnp.concatenate([lower[at:at + r.rows], upper[at:at + r.rows]], axis=0)
        per_layer.append(_unpack_rows(shard, shapes))
        at += r.rows
    return {n: jnp.stack([p[n] for p in per_layer]) for n in PACKED}


def _small_pack(t, names=SMALL):
    flat = jnp.concatenate([t[n].reshape(-1) for n in names])
    total = -(-flat.shape[0] // SMALL_PAD) * SMALL_PAD
    return jnp.pad(flat, (0, total - flat.shape[0])).reshape(-1, 128)


def _small_unpack(a, shapes, names=SMALL):
    flat, out, at = a.reshape(-1), {}, 0
    for n in names:
        k = math.prod(shapes[n])
        out[n] = flat[at:at + k].reshape(shapes[n])
        at += k
    return out


def adamw(name, w, g, m, v):
    shape = w.shape
    two = lambda a: a.reshape(-1, shape[-1])
    rows = two(w).shape[0]
    T = 256 if rows % 256 == 0 else rows
    outs = rowwise(name, fn_adamw, [_row(two(w)), _row(two(g)), _row(two(m)), _row(two(v))], [],
                   [_out(shape[-1], f32)] * 3, T=T)
    return [o.reshape(shape) for o in outs]


def kernel(x, mem, positions, ffn1_norm, ffn1_w_in, ffn1_w_out, mix_norm, w_in, conv_dw, conv_b, conv_ln_g, conv_ln_b, sb_q_hnorm, sb_k_hnorm, mla_q_norm, mla_w_uq, mla_kv_norm, mla_w_ukv, mla_q_hnorm, mla_k_hnorm, mem_norm, mem_w_kv, mem_q_hnorm, mem_k_hnorm, w_branch, w_out, ffn2_norm, ffn2_w_in, ffn2_w_out, loss_target, m_ffn1_norm, m_ffn1_w_in, m_ffn1_w_out, m_mix_norm, m_w_in, m_conv_dw, m_conv_b, m_conv_ln_g, m_conv_ln_b, m_sb_q_hnorm, m_sb_k_hnorm, m_mla_q_norm, m_mla_w_uq, m_mla_kv_norm, m_mla_w_ukv, m_mla_q_hnorm, m_mla_k_hnorm, m_mem_norm, m_mem_w_kv, m_mem_q_hnorm, m_mem_k_hnorm, m_w_branch, m_w_out, m_ffn2_norm, m_ffn2_w_in, m_ffn2_w_out, v_ffn1_norm, v_ffn1_w_in, v_ffn1_w_out, v_mix_norm, v_w_in, v_conv_dw, v_conv_b, v_conv_ln_g, v_conv_ln_b, v_sb_q_hnorm, v_sb_k_hnorm, v_mla_q_norm, v_mla_w_uq, v_mla_kv_norm, v_mla_w_ukv, v_mla_q_hnorm, v_mla_k_hnorm, v_mem_norm, v_mem_w_kv, v_mem_q_hnorm, v_mem_k_hnorm, v_w_branch, v_w_out, v_ffn2_norm, v_ffn2_w_in, v_ffn2_w_out):
    w = dict(zip(WEIGHTS, (ffn1_norm, ffn1_w_in, ffn1_w_out, mix_norm, w_in, conv_dw, conv_b, conv_ln_g, conv_ln_b, sb_q_hnorm, sb_k_hnorm, mla_q_norm, mla_w_uq, mla_kv_norm, mla_w_ukv, mla_q_hnorm, mla_k_hnorm, mem_norm, mem_w_kv, mem_q_hnorm, mem_k_hnorm, w_branch, w_out, ffn2_norm, ffn2_w_in, ffn2_w_out)))
    m = dict(zip(WEIGHTS, (m_ffn1_norm, m_ffn1_w_in, m_ffn1_w_out, m_mix_norm, m_w_in, m_conv_dw, m_conv_b, m_conv_ln_g, m_conv_ln_b, m_sb_q_hnorm, m_sb_k_hnorm, m_mla_q_norm, m_mla_w_uq, m_mla_kv_norm, m_mla_w_ukv, m_mla_q_hnorm, m_mla_k_hnorm, m_mem_norm, m_mem_w_kv, m_mem_q_hnorm, m_mem_k_hnorm, m_w_branch, m_w_out, m_ffn2_norm, m_ffn2_w_in, m_ffn2_w_out)))
    v = dict(zip(WEIGHTS, (v_ffn1_norm, v_ffn1_w_in, v_ffn1_w_out, v_mix_norm, v_w_in, v_conv_dw, v_conv_b, v_conv_ln_g, v_conv_ln_b, v_sb_q_hnorm, v_sb_k_hnorm, v_mla_q_norm, v_mla_w_uq, v_mla_kv_norm, v_mla_w_ukv, v_mla_q_hnorm, v_mla_k_hnorm, v_mem_norm, v_mem_w_kv, v_mem_q_hnorm, v_mem_k_hnorm, v_w_branch, v_w_out, v_ffn2_norm, v_ffn2_w_in, v_ffn2_w_out)))
    S = x.shape[1]
    packed = [pack_layer_shard(w, l) for l in range(DEPTH)]
    conv_dw_full = gather_conv_dw(w["conv_dw"])
    gathered = {0: gather_shards("ag_shards_l0", packed[0])}

    def weights_of(l):
        W = unpack_layer(gathered[l], w)
        W["conv_dw"] = conv_dw_full[l]
        W.update({n: w[n][l] for n in SMALL})
        return W

    def deliver(l, landed):
        if landed is not None:
            gathered[l + 1] = landed

    reducers = {}

    def reducer_of(l, layer_grads):
        reducers[l] = LayerReduce(f"rs_l{l}", layer_grads)
        return reducers[l]

    loss_blk, dx, g = local_step(x[0], mem[0], positions.reshape(S, 1), loss_target[0], weights_of,
                                 lambda l: packed[l + 1] if l + 1 < DEPTH else None, deliver, reducer_of)
    loss = lax.psum(loss_blk[0, 0], ("x", "y", "c"))
    grads = finish_reduce([reducers[l] for l in range(DEPTH)], w)
    small_shapes = {n: w[n].shape for n in SMALL}
    reduced = SMALL + ("conv_dw",)
    reduced_shapes = dict(small_shapes, conv_dw=conv_dw_full.shape)
    g_all = {n: jnp.stack([gl[n] for gl in g]) for n in reduced}
    g_all = sum8("small_sum", all8_gather("small_gather", _small_pack(g_all, reduced)))
    g_all = _small_unpack(g_all, reduced_shapes, reduced)
    width = w["conv_dw"].shape[-1]
    x_pos, y_pos, _ = _coords()
    grads["conv_dw"] = lax.dynamic_slice_in_dim(g_all.pop("conv_dw"), (2 * x_pos + y_pos) * width, width, axis=2)
    grads.update(g_all)
    g_small = _small_pack(grads)
    delta, new_m, new_v = {}, {}, {}
    for n in SHARDED:
        delta[n], new_m[n], new_v[n] = adamw(f"adamw_{n}", w[n], grads[n], m[n], v[n])
    d_s, m_s, v_s = adamw("adamw_small", _small_pack(w), g_small, _small_pack(m), _small_pack(v))
    for t, packed in ((delta, d_s), (new_m, m_s), (new_v, v_s)):
        t.update(_small_unpack(packed, small_shapes))
    return (loss, dx.reshape(x.shape), *[grads[n] for n in WEIGHTS], *[delta[n] for n in WEIGHTS],
            *[new_m[n] for n in WEIGHTS], *[new_v[n] for n in WEIGHTS])
```

```python
import math

import jax
import jax.numpy as jnp
from jax import lax
from jax.experimental import pallas as pl
from jax.experimental.pallas import tpu as pltpu

f32, bf16 = jnp.float32, jnp.bfloat16

D_MODEL = 1024
DEPTH = 4
CHUNK = 64
FFN_HIDDEN = 2048
CONV_CH = 512
CONV_WIDTH = 31
HEADS = 4
HEAD_DIM = 128
MLA_NOPE = 128
MLA_ROPE = 64
MLA_QK = MLA_NOPE + MLA_ROPE
MLA_PAD = 256
MLA_Q_LORA = 256
MLA_KV_LORA = 256
N_BRANCH = 4
BRANCH_WIDTH = 512
ROPE_BASE = 10000.0
EPS = 1e-6
NEG_INF = -1e30
IN_WIDTH = 7744
U_WIDTH = 8192
_U_SEGS = (("gates", 3648, 4096, 0), ("conv", 0, 1024, 4096), ("sb", 1024, 1536, 5120), ("qlat", 2560, 256, 6656),
           ("kvlat", 2816, 256, 6912), ("memq", 3136, 512, 7168), ("krope", 3072, 64, 7680))
U_PAD_FROM = 7744

ADAM_LR, ADAM_B1, ADAM_B2, ADAM_EPS, ADAM_WD, ADAM_STEP = 0.001, 0.9, 0.999, 1e-08, 0.01, 10

VMEM_LIMIT = 48 * 1024 * 1024
MESH = pl.DeviceIdType.MESH

SHARDED = ("ffn1_w_in", "ffn1_w_out", "w_in", "conv_dw", "mla_w_uq", "mla_w_ukv", "mem_w_kv", "w_branch", "w_out",
           "ffn2_w_in", "ffn2_w_out")
SHARD_AXIS = {"ffn1_w_in": 2, "ffn1_w_out": 1, "w_in": 2, "conv_dw": 2, "mla_w_uq": 2, "mla_w_ukv": 2, "mem_w_kv": 1,
              "w_branch": 3, "w_out": 1, "ffn2_w_in": 2, "ffn2_w_out": 1}
SMALL = ("ffn1_norm", "mix_norm", "conv_b", "conv_ln_g", "conv_ln_b", "sb_q_hnorm", "sb_k_hnorm", "mla_q_norm",
         "mla_kv_norm", "mla_q_hnorm", "mla_k_hnorm", "mem_norm", "mem_q_hnorm", "mem_k_hnorm", "ffn2_norm")
WEIGHTS = ("ffn1_norm", "ffn1_w_in", "ffn1_w_out", "mix_norm", "w_in", "conv_dw", "conv_b", "conv_ln_g", "conv_ln_b",
           "sb_q_hnorm", "sb_k_hnorm", "mla_q_norm", "mla_w_uq", "mla_kv_norm", "mla_w_ukv", "mla_q_hnorm",
           "mla_k_hnorm", "mem_norm", "mem_w_kv", "mem_q_hnorm", "mem_k_hnorm", "w_branch", "w_out", "ffn2_norm",
           "ffn2_w_in", "ffn2_w_out")
PACK_GRAIN = 2 * 256 * 1024
SMALL_PAD = 8 * 128


def _params(sem, vmem=VMEM_LIMIT):
    return pltpu.CompilerParams(dimension_semantics=sem, vmem_limit_bytes=vmem)


def _pick(n, pref):
    for t in pref:
        if n % t == 0:
            return t
    return n


def mm(name, a, b, form, *, out_dtype=f32, alpha=1.0, res=None, tm=None, tn=None, tk=None):
    if form == "nn":
        (M, K), (K2, N) = a.shape, b.shape
    elif form == "nt":
        (M, K), (N, K2) = a.shape, b.shape
    else:
        (K, M), (K2, N) = a.shape, b.shape
    assert K == K2, (name, a.shape, b.shape)
    tm = tm or _pick(M, (1024, 512, 256, 128))
    tn = tn or _pick(N, (1024, 512, 256, 128))
    tk = tk or _pick(K, (2048, 1024, 512, 256))
    nk = K // tk
    if form == "nn":
        a_spec = pl.BlockSpec((tm, tk), lambda i, j, k: (i, k))
        b_spec = pl.BlockSpec((tk, tn), lambda i, j, k: (k, j))
        dims = (((1,), (0,)), ((), ()))
    elif form == "nt":
        a_spec = pl.BlockSpec((tm, tk), lambda i, j, k: (i, k))
        b_spec = pl.BlockSpec((tn, tk), lambda i, j, k: (j, k))
        dims = (((1,), (1,)), ((), ()))
    else:
        a_spec = pl.BlockSpec((tk, tm), lambda i, j, k: (k, i))
        b_spec = pl.BlockSpec((tk, tn), lambda i, j, k: (k, j))
        dims = (((0,), (0,)), ((), ()))
    o_spec = pl.BlockSpec((tm, tn), lambda i, j, k: (i, j))
    has_res = res is not None

    def body(a_ref, b_ref, *rest):
        if has_res:
            r_ref, o_ref, acc_ref = rest
        else:
            o_ref, acc_ref = rest
        k = pl.program_id(2)
        part = lax.dot_general(a_ref[...].astype(bf16), b_ref[...].astype(bf16), dims, preferred_element_type=f32)

        def finish(acc):
            r = acc if alpha == 1.0 else acc * alpha
            if has_res:
                r = r_ref[...].astype(f32) + r
            o_ref[...] = r.astype(o_ref.dtype)

        if nk == 1:
            finish(part)
        else:
            @pl.when(k == 0)
            def _():
                acc_ref[...] = part

            @pl.when(k > 0)
            def _():
                acc_ref[...] += part

            @pl.when(k == nk - 1)
            def _():
                finish(acc_ref[...])

    ins = [a, b] + ([res] if has_res else [])
    in_specs = [a_spec, b_spec] + ([o_spec] if has_res else [])
    return pl.pallas_call(
        body, name=name, grid=(M // tm, N // tn, nk), in_specs=in_specs, out_specs=o_spec,
        out_shape=jax.ShapeDtypeStruct((M, N), out_dtype), scratch_shapes=[pltpu.VMEM((tm, tn), f32)],
        compiler_params=_params(("parallel", "parallel", "arbitrary")))(*ins)


class _Span:
    def __init__(self, w, off, st, H):
        self.w = w
        if st == 0 or H == 1:
            self.width, self.index, self.base, self.step = w, off, 0, 0
        else:
            self.width, self.index = st * H * w, off // (st * H)
            self.base, self.step = (off % (st * H)) * w, st * w

    def spec(self, T):
        return pl.BlockSpec((T, self.width), lambda i: (i, self.index))

    def lanes(self, h):
        return slice(self.base + h * self.step, self.base + h * self.step + self.w)


def _full_spec(p):
    return pl.BlockSpec(p.shape, lambda i: (0,) * p.ndim)


def rowwise(name, fn, rows, params, outs, *, T=512, H=1):
    S = rows[0][0].shape[0]
    T = min(T, S)
    n_r, n_p = len(rows), len(params)
    in_spans = [_Span(w, off, st, H) for (_, w, off, st) in rows]
    out_spans = [_Span(w, off, st, H) for (_, _, w, off, st) in outs]

    def body(*refs):
        p_vals = [r[...] for r in refs[n_r:n_r + n_p]]
        for h in range(H):
            vals = [r[:, sp.lanes(h)] for r, sp in zip(refs[:n_r], in_spans)]
            res = fn(*vals, *p_vals)
            for o_ref, sp, r in zip(refs[n_r + n_p:], out_spans, res):
                o_ref[:, sp.lanes(h)] = r.astype(o_ref.dtype)

    res = pl.pallas_call(
        body, name=name, grid=(S // T,),
        in_specs=[sp.spec(T) for sp in in_spans] + [_full_spec(p) for p in params],
        out_specs=[sp.spec(T) for sp in out_spans],
        out_shape=[jax.ShapeDtypeStruct((S, tw), dt) for (tw, dt, _, _, _) in outs],
        compiler_params=_params(("parallel",)))(*[r[0] for r in rows], *params)
    return res


def rowwise_bwd(name, fn, rows, params, douts, drows, *, T=512, H=1, nondiff=(), merge=None):
    S = rows[0][0].shape[0]
    T = min(T, S)
    n_r, n_p, n_d = len(rows), len(params), len(douts)
    diff_idx = [k for k in range(n_r) if k not in nondiff]
    merge = merge or [(j,) for j in range(len(diff_idx))]
    shared = [H > 1 and rows[diff_idx[pos[0]]][3] == 0 for pos in merge]
    n_o = len(merge)
    assert n_o == len(drows)
    in_spans = [_Span(w, off, st, H) for (_, w, off, st) in rows]
    d_spans = [_Span(w, off, st, H) for (_, w, off, st) in douts]
    out_spans = [_Span(w, off, st, H) for (_, _, w, off, st) in drows]

    def body(*refs):
        i = pl.program_id(0)
        p_vals = [r[...].astype(f32) for r in refs[n_r:n_r + n_p]]
        out_refs = refs[n_r + n_p + n_d:]
        shared_sum = [None] * n_o
        p_sum = [None] * n_p
        for h in range(H):
            row_vals = [r[:, sp.lanes(h)].astype(f32) for r, sp in zip(refs[:n_r], in_spans)]
            d_vals = [r[:, sp.lanes(h)].astype(f32) for r, sp in zip(refs[n_r + n_p:n_r + n_p + n_d], d_spans)]

            def f(*args):
                full = list(row_vals)
                for k, v in zip(diff_idx, args[:len(diff_idx)]):
                    full[k] = v
                return tuple(fn(*full, *args[len(diff_idx):]))

            _, vjp = jax.vjp(f, *[row_vals[k] for k in diff_idx], *p_vals)
            cts = vjp(tuple(d_vals))
            row_cts = [cts[pos[0]] if len(pos) == 1 else jnp.concatenate([cts[j] for j in pos], axis=-1) for pos in merge]
            for j, (o_ref, sp, ct) in enumerate(zip(out_refs[:n_o], out_spans, row_cts)):
                if shared[j]:
                    shared_sum[j] = ct if h == 0 else shared_sum[j] + ct
                else:
                    o_ref[:, sp.lanes(h)] = ct.astype(o_ref.dtype)
            for j, ct in enumerate(cts[len(diff_idx):]):
                p_sum[j] = ct if h == 0 else p_sum[j] + ct
        for j, o_ref in enumerate(out_refs[:n_o]):
            if shared[j]:
                o_ref[...] = shared_sum[j].astype(o_ref.dtype)
        for o_ref, ct in zip(out_refs[n_o:], p_sum):
            @pl.when(i == 0)
            def _():
                o_ref[...] = ct

            @pl.when(i > 0)
            def _():
                o_ref[...] += ct

    res = pl.pallas_call(
        body, name=name, grid=(S // T,),
        in_specs=[sp.spec(T) for sp in in_spans] + [_full_spec(p) for p in params] + [sp.spec(T) for sp in d_spans],
        out_specs=[sp.spec(T) for sp in out_spans] + [_full_spec(p) for p in params],
        out_shape=[jax.ShapeDtypeStruct((S, tw), dt) for (tw, dt, _, _, _) in drows]
        + [jax.ShapeDtypeStruct(p.shape, f32) for p in params],
        compiler_params=_params(("arbitrary",)))(*[r[0] for r in rows], *params, *[d[0] for d in douts])
    return res[:n_o], res[n_o:]


def _rms(x, g, n=None):
    x = x.astype(f32)
    n = n or x.shape[-1]
    return x * lax.rsqrt(jnp.sum(x * x, axis=-1, keepdims=True) * (1.0 / n) + EPS) * g.astype(f32)


def _sigmoid(x):
    return 1.0 / (1.0 + jnp.exp(-x))


def _silu(x):
    return x * _sigmoid(x)


def fn_rms(x, g):
    return (_rms(x, g),)


def fn_rms_res(x, g):
    return (x.astype(f32), _rms(x, g))


def fn_swiglu(gate, up):
    return (_silu(gate.astype(f32)) * up.astype(f32),)


def fn_sb_prep(q, k, gq, gk):
    return (_rms(q, gq), _rms(k, gk))


def fn_ln_silu(y, b, g, beta):
    y = y.astype(f32) + b
    mu = jnp.mean(y, axis=-1, keepdims=True)
    var = jnp.mean(jnp.square(y - mu), axis=-1, keepdims=True)
    return (_silu((y - mu) * lax.rsqrt(var + EPS) * g + beta),)


def fn_merge(g0, g1, g2, g3, p0, p1, p2, p3):
    out = _sigmoid(g0.astype(f32)) * p0.astype(f32)
    for g, p in ((g1, p1), (g2, p2), (g3, p3)):
        out = out + _sigmoid(g.astype(f32)) * p.astype(f32)
    return (out,)


def _rot_fwd(x):
    z = jnp.zeros_like(x[:, :MLA_NOPE])
    h = MLA_ROPE // 2
    return jnp.concatenate([z, -x[:, MLA_NOPE + h:MLA_QK], x[:, MLA_NOPE:MLA_NOPE + h], z[:, :MLA_PAD - MLA_QK]], axis=-1)


def _rot_bwd(g):
    z = jnp.zeros_like(g[:, :MLA_NOPE])
    h = MLA_ROPE // 2
    return jnp.concatenate([z, g[:, MLA_NOPE + h:MLA_QK], -g[:, MLA_NOPE:MLA_NOPE + h], z[:, :MLA_PAD - MLA_QK]], axis=-1)


@jax.custom_vjp
def _rope(x, c, s):
    return x * c + _rot_fwd(x) * s


def _rope_f(x, c, s):
    return _rope(x, c, s), (c, s)


def _rope_b(res, g):
    c, s = res
    return g * c + _rot_bwd(g * s), jnp.zeros_like(c), jnp.zeros_like(s)


_rope.defvjp(_rope_f, _rope_b)


def fn_mla_q(q, c, s, gain):
    return (_rope(_rms(q, gain, MLA_QK), c, s),)


def fn_mla_k(kn, kr, c, s, gain):
    k = jnp.concatenate([kn.astype(f32), kr.astype(f32)], axis=-1)
    return (_rope(_rms(k, gain, MLA_QK), c, s),)


def fn_mla_k_v(kn, kr, v, c, s, gain):
    return (fn_mla_k(kn, kr, c, s, gain)[0], v.astype(f32))


def fn_mem_k_v(k, v, gain):
    return (_rms(k, gain), v.astype(f32))


def fn_adamw(w, g, m, v):
    m = ADAM_B1 * m + (1.0 - ADAM_B1) * g
    v = ADAM_B2 * v + (1.0 - ADAM_B2) * jnp.square(g)
    m_hat = m / (1.0 - ADAM_B1 ** ADAM_STEP)
    v_hat = v / (1.0 - ADAM_B2 ** ADAM_STEP)
    delta = -ADAM_LR * (m_hat / (jnp.sqrt(v_hat) + ADAM_EPS) + ADAM_WD * w)
    return delta, m, v


def _head_spec(rows, w, off, st):
    return pl.BlockSpec((rows, w), lambda h, i: (0, off + st * h))


def _qblk_spec(B, w, off, st):
    return pl.BlockSpec((B, w), lambda h, i: (i, off + st * h))


def _chunk_mask(tq, tk, d):
    r = lax.broadcasted_iota(jnp.int32, (tq, tk), 0) // CHUNK
    c = (d * tk + lax.broadcasted_iota(jnp.int32, (tq, tk), 1)) // CHUNK
    return c <= r


def _strict_mask(tq, tk, d):
    r = lax.broadcasted_iota(jnp.int32, (tq, tk), 0)
    c = d * tk + lax.broadcasted_iota(jnp.int32, (tq, tk), 1)
    return c < r


_NT = (((1,), (1,)), ((), ()))
_TN = (((0,), (0,)), ((), ()))


def _tiles(Sq, Sk, mask, tq, tk):
    tq = min(tq, Sq)
    if mask is None:
        return tq, Sk, 0
    tk = min(tk, tq)
    assert Sq == Sk and tq % tk == 0 and tk % CHUNK == 0
    return tq, tk, tq // tk


def _gather_in_steps(own_ref, out_ref, send_sems, recv_sems, local_sem, h, i, last_h, last_i):
    x, y, c = lax.axis_index("x"), lax.axis_index("y"), lax.axis_index("c")
    sib = (x, y, 1 - c)
    chips = [(2, (1 - x, y)), (1, (x, 1 - y)), (3, (1 - x, 1 - y))]

    def copy(k, src, m, half, to):
        return pltpu.make_async_remote_copy(src, out_ref.at[m, half], send_sems.at[k], recv_sems.at[k],
                                            device_id=to, device_id_type=MESH)

    mine = pltpu.make_async_copy(own_ref, out_ref.at[0], local_sem)
    first = [copy(k, own_ref.at[c], m, c, (px, py, c)) for k, (m, (px, py)) in enumerate(chips)]
    passed = [copy(3 + k, out_ref.at[m, c], m, c, sib) for k, (m, _) in enumerate(chips)]
    landed = [copy(3 + k, out_ref.at[m, 1 - c], m, 1 - c, sib) for k, (m, _) in enumerate(chips)]

    @pl.when(jnp.logical_and(h == 0, i == 0))
    def _():
        mine.start()
        for cp in first:
            cp.start()

    @pl.when(jnp.logical_and(h == last_h, i == 0))
    def _():
        for k in range(3):
            first[k].wait_recv()
            passed[k].start()

    @pl.when(jnp.logical_and(h == last_h, i == last_i))
    def _():
        for cp in landed:
            cp.wait_recv()
        for cp in first + passed:
            cp.wait_send()
        mine.wait()


def attn_fwd(name, q, k, v, *, scale, mask, tq=1024, tk=1024, ride=None):
    Sq, Sk = q[0].shape[0], k[0].shape[0]
    tq, tk, nd = _tiles(Sq, Sk, mask, tq, tk)
    n_q = Sq // tq

    def body(q_ref, k_ref, v_ref, *rest):
        if ride is None:
            o_ref, lse_ref = rest
        else:
            own_ref, o_ref, lse_ref, land_ref, send_sems, recv_sems, local_sem = rest
            _gather_in_steps(own_ref, land_ref, send_sems, recv_sems, local_sem, pl.program_id(0), pl.program_id(1),
                             HEADS - 1, n_q - 1)
        i = pl.program_id(1)
        qv = q_ref[...]

        def block(off, carry, d):
            m, l, acc = carry
            kb, vb = k_ref[pl.ds(off, tk), :].astype(bf16), v_ref[pl.ds(off, tk), :].astype(bf16)
            s = lax.dot_general(qv, kb, _NT, preferred_element_type=f32) * scale
            if d is not None:
                s = jnp.where(_chunk_mask(tq, tk, d), s, NEG_INF)
            m_new = jnp.maximum(m, jnp.max(s, axis=-1, keepdims=True))
            p = jnp.exp(s - m_new)
            corr = jnp.exp(m - m_new)
            l = l * corr + jnp.sum(p, axis=-1, keepdims=True)
            acc = acc * corr + jnp.dot(p.astype(bf16), vb, preferred_element_type=f32)
            return m_new, l, acc

        carry = (jnp.full((tq, 1), NEG_INF, f32), jnp.zeros((tq, 1), f32), jnp.zeros((tq, HEAD_DIM), f32))
        if nd:
            carry = lax.fori_loop(0, i * nd, lambda j, c: block(pl.multiple_of(j * tk, tk), c, None), carry)
            for d in range(nd):
                carry = block(pl.multiple_of(i * tq + d * tk, tk), carry, d)
        else:
            carry = block(0, carry, None)
        m, l, acc = carry
        o_ref[...] = (acc / l).astype(o_ref.dtype)
        lse_ref[...] = jnp.broadcast_to(m + jnp.log(l), (tq, HEAD_DIM))

    in_specs = [_qblk_spec(tq, *q[1:]), _head_spec(Sk, *k[1:]), _head_spec(Sk, *v[1:])]
    out_specs = [_qblk_spec(tq, HEAD_DIM, 0, 1), _qblk_spec(tq, HEAD_DIM, 0, 1)]
    out_shape = [jax.ShapeDtypeStruct((Sq, HEADS * HEAD_DIM), f32), jax.ShapeDtypeStruct((Sq, HEADS * HEAD_DIM), f32)]
    if ride is None:
        return pl.pallas_call(body, name=name, grid=(HEADS, n_q), in_specs=in_specs, out_specs=out_specs,
                              out_shape=out_shape, compiler_params=_params(("parallel", "arbitrary")))(q[0], k[0], v[0])
    any_space = pl.BlockSpec(memory_space=pl.ANY)
    return pl.pallas_call(
        body, name=name, grid=(HEADS, n_q), in_specs=in_specs + [any_space], out_specs=out_specs + [any_space],
        out_shape=out_shape + [jax.ShapeDtypeStruct((4,) + ride.shape, ride.dtype)],
        scratch_shapes=[pltpu.SemaphoreType.DMA((6,)), pltpu.SemaphoreType.DMA((6,)), pltpu.SemaphoreType.DMA],
        compiler_params=pltpu.CompilerParams(dimension_semantics=("arbitrary", "arbitrary"), vmem_limit_bytes=VMEM_LIMIT,
                                             has_side_effects=True))(q[0], k[0], v[0], ride)


def _exchange_in_steps(kind, src_ref, out_ref, send_sems, recv_sems, is_first, is_last):
    x, y, c = lax.axis_index("x"), lax.axis_index("y"), lax.axis_index("c")
    if kind == "sibling":
        copies = [pltpu.make_async_remote_copy(src_ref, out_ref, send_sems.at[0], recv_sems.at[0],
                                               device_id=(x, y, 1 - c), device_id_type=MESH)]
    else:
        copies = [pltpu.make_async_remote_copy(src_ref.at[2 * px + py], out_ref.at[k], send_sems.at[k], recv_sems.at[k],
                                               device_id=(px, py, c), device_id_type=MESH)
                  for k, (px, py) in enumerate([(1 - x, y), (x, 1 - y), (1 - x, 1 - y)])]

    @pl.when(is_first)
    def _():
        for cp in copies:
            cp.start()

    @pl.when(is_last)
    def _():
        for cp in copies:
            cp.wait()


def _ride_call(body, name, grid, in_specs, out_specs, out_shape, operands, ride, vmem=VMEM_LIMIT):
    if ride is None:
        return pl.pallas_call(body, name=name, grid=grid, in_specs=in_specs, out_specs=out_specs, out_shape=out_shape,
                              compiler_params=_params(("arbitrary",) * len(grid), vmem))(*operands)
    kind, src = ride
    n = 1 if kind == "sibling" else 3
    landing = jax.ShapeDtypeStruct(src.shape if kind == "sibling" else (3,) + src.shape[1:], src.dtype)
    any_space = pl.BlockSpec(memory_space=pl.ANY)
    return pl.pallas_call(
        body, name=name, grid=grid, in_specs=in_specs + [any_space], out_specs=out_specs + [any_space],
        out_shape=out_shape + [landing],
        scratch_shapes=[pltpu.SemaphoreType.DMA((n,)), pltpu.SemaphoreType.DMA((n,))],
        compiler_params=pltpu.CompilerParams(dimension_semantics=("arbitrary",) * len(grid), vmem_limit_bytes=vmem,
                                             has_side_effects=True))(*operands, src)


def attn_bwd(name, q, k, v, o, do, lse, *, scale, mask, tq=1024, tk=512, ride=None):
    Sq, Sk = q[0].shape[0], k[0].shape[0]
    tq, tk, nd = _tiles(Sq, Sk, mask, tq, tk)
    dq_w = q[1]
    n_q = Sq // tq

    def body(q_ref, k_ref, v_ref, o_ref, do_ref, lse_ref, *rest):
        if ride is None:
            dq_ref, dk_ref, dv_ref = rest
        else:
            src_ref, dq_ref, dk_ref, dv_ref, land_ref, send_sems, recv_sems = rest
            h, i = pl.program_id(0), pl.program_id(1)
            _exchange_in_steps(ride[0], src_ref, land_ref, send_sems, recv_sems, jnp.logical_and(h == 0, i == 0),
                               jnp.logical_and(h == HEADS - 1, i == n_q - 1))
        i = pl.program_id(1)

        @pl.when(i == 0)
        def _():
            dk_ref[...] = jnp.zeros_like(dk_ref)
            dv_ref[...] = jnp.zeros_like(dv_ref)

        qv, dov = q_ref[...], do_ref[...].astype(bf16)
        delta = jnp.sum(do_ref[...].astype(f32) * o_ref[...].astype(f32), axis=-1, keepdims=True)
        lse_v = lse_ref[:, :1]

        def block(off, dq_acc, d):
            kb, vb = k_ref[pl.ds(off, tk), :].astype(bf16), v_ref[pl.ds(off, tk), :].astype(bf16)
            s = lax.dot_general(qv, kb, _NT, preferred_element_type=f32) * scale
            if d is not None:
                s = jnp.where(_chunk_mask(tq, tk, d), s, NEG_INF)
            p = jnp.exp(s - lse_v)
            dv_ref[pl.ds(off, tk), :] += lax.dot_general(p.astype(bf16), dov, _TN, preferred_element_type=f32)
            dp = lax.dot_general(dov, vb, _NT, preferred_element_type=f32)
            ds = (p * (dp - delta) * scale).astype(bf16)
            dk_ref[pl.ds(off, tk), :] += lax.dot_general(ds, qv, _TN, preferred_element_type=f32)
            return dq_acc + jnp.dot(ds, kb, preferred_element_type=f32)

        acc = jnp.zeros((tq, dq_w), f32)
        if nd:
            acc = lax.fori_loop(0, i * nd, lambda j, c: block(pl.multiple_of(j * tk, tk), c, None), acc)
            for d in range(nd):
                acc = block(pl.multiple_of(i * tq + d * tk, tk), acc, d)
        else:
            acc = block(0, acc, None)
        dq_ref[...] = acc.astype(dq_ref.dtype)

    hd = _qblk_spec(tq, HEAD_DIM, 0, 1)
    return _ride_call(
        body, name, (HEADS, n_q),
        [_qblk_spec(tq, *q[1:]), _head_spec(Sk, *k[1:]), _head_spec(Sk, *v[1:]), hd, hd, hd],
        [_qblk_spec(tq, dq_w, 0, 1), _head_spec(Sk, dq_w, 0, 1), _head_spec(Sk, HEAD_DIM, 0, 1)],
        [jax.ShapeDtypeStruct((Sq, HEADS * dq_w), f32), jax.ShapeDtypeStruct((Sk, HEADS * dq_w), f32),
         jax.ShapeDtypeStruct((Sk, HEADS * HEAD_DIM), f32)],
        (q[0], k[0], v[0], o, do, lse), ride, 56 * 1024 * 1024)


SB_LOG_ZERO = -104.0


def _tri(B, rel):
    r = lax.broadcasted_iota(jnp.int32, (B, B), 0)
    c = lax.broadcasted_iota(jnp.int32, (B, B), 1)
    return rel(r, c).astype(bf16)


def _sb_scores(qv, kb, scale):
    z = lax.dot_general(qv, kb, _NT, preferred_element_type=f32) * scale
    e = jnp.exp(-jnp.abs(z))
    log_keep = -(jnp.maximum(z, 0.0) + jnp.log(1.0 + e))
    return z, e, log_keep


def _split_dot(x, m):
    hi = x.astype(bf16)
    lo = (x - hi.astype(f32)).astype(bf16)
    return jnp.dot(hi, m, preferred_element_type=f32) + jnp.dot(lo, m, preferred_element_type=f32)


def sb_fwd(name, q, k, v, *, tq=512, tk=256):
    S = q[0].shape[0]
    tq, tk, nd = _tiles(S, S, "strict", tq, tk)
    scale = HEAD_DIM ** -0.5
    m_ex = _tri(tk, lambda j, s: j > s)

    def body(q_ref, k_ref, v_ref, mex_ref, o_ref, tot_ref, cnt_ref):
        i = pl.program_id(1)
        qv, mex = q_ref[...], mex_ref[...]

        def block(off, carry, d):
            later, acc = carry
            kb, vb = k_ref[pl.ds(off, tk), :].astype(bf16), v_ref[pl.ds(off, tk), :].astype(bf16)
            z, _, lk = _sb_scores(qv, kb, scale)
            if d is not None:
                lk = jnp.where(_strict_mask(tq, tk, d), lk, 0.0)
            a = jnp.exp(z + lk + _split_dot(lk, mex) + later)
            if d is not None:
                a = jnp.where(_strict_mask(tq, tk, d), a, 0.0)
            acc = acc + jnp.dot(a.astype(bf16), vb, preferred_element_type=f32)
            return later + jnp.sum(lk, axis=-1, keepdims=True), acc

        carry = (jnp.zeros((tq, 1), f32), jnp.zeros((tq, HEAD_DIM), f32))
        for d in reversed(range(nd)):
            carry = block(pl.multiple_of(i * tq + d * tk, tk), carry, d)
        n_full = i * nd

        def more(state):
            t, later, _ = state
            return jnp.logical_and(t < n_full, jnp.max(later) > SB_LOG_ZERO)

        def step(state):
            t, later, acc = state
            later, acc = block(pl.multiple_of((n_full - 1 - t) * tk, tk), (later, acc), None)
            return t + 1, later, acc

        done, total, acc = lax.while_loop(more, step, (jnp.int32(0),) + carry)
        o_ref[...] = acc.astype(o_ref.dtype)
        tot_ref[...] = jnp.broadcast_to(total, (tq, HEAD_DIM))
        cnt_ref[...] = jnp.full((8, HEAD_DIM), done, f32)

    hd = _qblk_spec(tq, HEAD_DIM, 0, 1)
    return pl.pallas_call(
        body, name=name, grid=(HEADS, S // tq),
        in_specs=[_qblk_spec(tq, *q[1:]), _head_spec(S, *k[1:]), _head_spec(S, *v[1:]),
                  pl.BlockSpec((tk, tk), lambda h, i: (0, 0))],
        out_specs=[hd, hd, _qblk_spec(8, HEAD_DIM, 0, 1)],
        out_shape=[jax.ShapeDtypeStruct((S, HEADS * HEAD_DIM), f32), jax.ShapeDtypeStruct((S, HEADS * HEAD_DIM), f32),
                   jax.ShapeDtypeStruct((8 * (S // tq), HEADS * HEAD_DIM), f32)],
        compiler_params=_params(("parallel", "arbitrary")))(q[0], k[0], v[0], m_ex)


def sb_bwd(name, q, k, v, tot, cnt, do, *, tq=512, tk=256, ride=None):
    S = q[0].shape[0]
    tq, tk, nd = _tiles(S, S, "strict", tq, tk)
    scale = HEAD_DIM ** -0.5
    m_le, m_lt = _tri(tk, lambda j, s: j <= s), _tri(tk, lambda j, s: j < s)

    n_q = S // tq

    def body(q_ref, k_ref, v_ref, tot_ref, cnt_ref, do_ref, mle_ref, mlt_ref, *rest):
        if ride is None:
            dq_ref, dk_ref, dv_ref = rest
        else:
            src_ref, dq_ref, dk_ref, dv_ref, land_ref, send_sems, recv_sems = rest
            h, i = pl.program_id(0), pl.program_id(1)
            _exchange_in_steps(ride[0], src_ref, land_ref, send_sems, recv_sems, jnp.logical_and(h == 0, i == 0),
                               jnp.logical_and(h == HEADS - 1, i == n_q - 1))
        i = pl.program_id(1)

        @pl.when(i == 0)
        def _():
            dk_ref[...] = jnp.zeros_like(dk_ref)
            dv_ref[...] = jnp.zeros_like(dv_ref)

        qv, dov, mle, mlt = q_ref[...], do_ref[...].astype(bf16), mle_ref[...], mlt_ref[...]
        total = tot_ref[:, :1]

        def block(off, carry, d):
            before, g_before, dq_acc = carry
            kb, vb = k_ref[pl.ds(off, tk), :].astype(bf16), v_ref[pl.ds(off, tk), :].astype(bf16)
            z, e, lk = _sb_scores(qv, kb, scale)
            sig = jnp.where(z >= 0, 1.0, e) / (1.0 + e)
            if d is not None:
                lk = jnp.where(_strict_mask(tq, tk, d), lk, 0.0)
            later = (total - before) - _split_dot(lk, mle)
            a = jnp.exp(z + lk + later)
            if d is not None:
                a = jnp.where(_strict_mask(tq, tk, d), a, 0.0)
            g = a * lax.dot_general(dov, vb, _NT, preferred_element_type=f32)
            prefix = g_before + jnp.dot(g.astype(bf16), mlt, preferred_element_type=f32)
            dz = (g * (1.0 - sig) - prefix * sig) * scale
            if d is not None:
                dz = jnp.where(_strict_mask(tq, tk, d), dz, 0.0)
            dzb = dz.astype(bf16)
            dk_ref[pl.ds(off, tk), :] += lax.dot_general(dzb, qv, _TN, preferred_element_type=f32)
            dv_ref[pl.ds(off, tk), :] += lax.dot_general(a.astype(bf16), dov, _TN, preferred_element_type=f32)
            return (before + jnp.sum(lk, axis=-1, keepdims=True), g_before + jnp.sum(g, axis=-1, keepdims=True),
                    dq_acc + jnp.dot(dzb, kb, preferred_element_type=f32))

        zero = jnp.zeros((tq, 1), f32)
        first = i * nd - jnp.max(cnt_ref[...]).astype(jnp.int32)
        carry = lax.fori_loop(first, i * nd, lambda j, c: block(pl.multiple_of(j * tk, tk), c, None),
                              (zero, zero, jnp.zeros((tq, HEAD_DIM), f32)))
        for d in range(nd):
            carry = block(pl.multiple_of(i * tq + d * tk, tk), carry, d)
        dq_ref[...] = carry[2].astype(dq_ref.dtype)

    hd = _qblk_spec(tq, HEAD_DIM, 0, 1)
    tri = pl.BlockSpec((tk, tk), lambda h, i: (0, 0))
    return _ride_call(
        body, name, (HEADS, n_q),
        [_qblk_spec(tq, *q[1:]), _head_spec(S, *k[1:]), _head_spec(S, *v[1:]), hd, _qblk_spec(8, HEAD_DIM, 0, 1),
         hd, tri, tri],
        [hd, _head_spec(S, HEAD_DIM, 0, 1), _head_spec(S, HEAD_DIM, 0, 1)],
        [jax.ShapeDtypeStruct((S, HEADS * HEAD_DIM), f32), jax.ShapeDtypeStruct((S, HEADS * HEAD_DIM), f32),
         jax.ShapeDtypeStruct((S, HEADS * HEAD_DIM), f32)],
        (q[0], k[0], v[0], tot, cnt, do, m_le, m_lt), ride)


CONV_HALO = 32
CONV_ROWS = 64
CONV_A_BLK, CONV_G_BLK = 8, 9


def _glu(a, g):
    return a.astype(f32) * _sigmoid(g.astype(f32))


SUBLANES = 8


def _phase_copies(src_ref, dst_ref, rows):
    src_ref[pl.ds(rows, SUBLANES), :] = jnp.zeros((SUBLANES, src_ref.shape[1]), src_ref.dtype)
    for p in range(SUBLANES):
        dst_ref[p] = src_ref[pl.ds(p, rows), :]


def _window(dst_ref, start, rows):
    return dst_ref[start % SUBLANES, pl.ds(start - start % SUBLANES, rows), :]


def conv_fwd(name, u, dw):
    S = u.shape[0]
    T = min(512, S)
    nT = S // T

    def body(a_ref, g_ref, ap_ref, gp_ref, dw_ref, y_ref, ext_ref, ext8_ref):
        i = pl.program_id(0)
        prev = _glu(ap_ref[T - CONV_HALO:, :], gp_ref[T - CONV_HALO:, :])
        ext_ref[:CONV_HALO, :] = jnp.where(i > 0, prev, 0.0)
        ext_ref[pl.ds(CONV_HALO, T), :] = _glu(a_ref[...], g_ref[...])
        _phase_copies(ext_ref, ext8_ref, T + CONV_HALO)
        rc = min(CONV_ROWS, T)
        for r0 in range(0, T, rc):
            acc = jnp.zeros((rc, CONV_CH), f32)
            for w in range(CONV_WIDTH):
                acc = acc + dw_ref[w:w + 1, :] * _window(ext8_ref, r0 + w + CONV_HALO - (CONV_WIDTH - 1), rc)
            y_ref[pl.ds(r0, rc), :] = acc

    cur = lambda blk: pl.BlockSpec((T, CONV_CH), lambda i: (i, blk))
    prv = lambda blk: pl.BlockSpec((T, CONV_CH), lambda i: (jnp.maximum(i - 1, 0), blk))
    return pl.pallas_call(
        body, name=name, grid=(nT,),
        in_specs=[cur(CONV_A_BLK), cur(CONV_G_BLK), prv(CONV_A_BLK), prv(CONV_G_BLK),
                  pl.BlockSpec(dw.shape, lambda i: (0, 0))],
        out_specs=pl.BlockSpec((T, CONV_CH), lambda i: (i, 0)),
        out_shape=jax.ShapeDtypeStruct((S, CONV_CH), f32),
        scratch_shapes=[pltpu.VMEM((T + CONV_HALO + SUBLANES, CONV_CH), f32),
                        pltpu.VMEM((SUBLANES, T + CONV_HALO, CONV_CH), f32)],
        compiler_params=_params(("arbitrary",)))(u, u, u, u, dw)


def conv_bwd(name, u, dy, dw):
    S = u.shape[0]
    T = min(512, S)
    nT = S // T
    lead = CONV_HALO - (CONV_WIDTH - 1)

    def body(a_ref, g_ref, ap_ref, gp_ref, dy_ref, dyn_ref, dw_ref, du_ref, ddw_ref, ext_ref, dext_ref, ext8_ref, dext8_ref):
        i = pl.program_id(0)
        prev = _glu(ap_ref[T - CONV_HALO:, :], gp_ref[T - CONV_HALO:, :])
        ext_ref[:CONV_HALO, :] = jnp.where(i > 0, prev, 0.0)
        ext_ref[pl.ds(CONV_HALO, T), :] = _glu(a_ref[...], g_ref[...])
        dext_ref[:T, :] = dy_ref[...]
        dext_ref[pl.ds(T, CONV_HALO), :] = jnp.where(i < nT - 1, dyn_ref[:CONV_HALO, :], 0.0)
        _phase_copies(ext_ref, ext8_ref, T + CONV_HALO)
        _phase_copies(dext_ref, dext8_ref, T + CONV_HALO)

        @pl.when(i == 0)
        def _():
            ddw_ref[...] = jnp.zeros_like(ddw_ref)

        rc = min(CONV_ROWS, T)
        for r0 in range(0, T, rc):
            dglu = jnp.zeros((rc, CONV_CH), f32)
            for w in range(CONV_WIDTH):
                dglu = dglu + dw_ref[w:w + 1, :] * _window(dext8_ref, r0 + CONV_WIDTH - 1 - w, rc)
            a, sg = a_ref[pl.ds(r0, rc), :].astype(f32), _sigmoid(g_ref[pl.ds(r0, rc), :].astype(f32))
            du_ref[pl.ds(r0, rc), :CONV_CH] = (dglu * sg).astype(du_ref.dtype)
            du_ref[pl.ds(r0, rc), CONV_CH:] = (dglu * a * sg * (1.0 - sg)).astype(du_ref.dtype)
        rc = min(CONV_ROWS // 2, T)
        for r0 in range(0, T, rc):
            dyv = dy_ref[pl.ds(r0, rc), :]
            for w in range(CONV_WIDTH):
                ddw_ref[w:w + 1, :] += jnp.sum(dyv * _window(ext8_ref, r0 + w + lead, rc), axis=0, keepdims=True)

    cur = lambda blk: pl.BlockSpec((T, CONV_CH), lambda i: (i, blk))
    prv = lambda blk: pl.BlockSpec((T, CONV_CH), lambda i: (jnp.maximum(i - 1, 0), blk))
    return pl.pallas_call(
        body, name=name, grid=(nT,),
        in_specs=[cur(CONV_A_BLK), cur(CONV_G_BLK), prv(CONV_A_BLK), prv(CONV_G_BLK),
                  pl.BlockSpec((T, CONV_CH), lambda i: (i, 0)),
                  pl.BlockSpec((T, CONV_CH), lambda i: (jnp.minimum(i + 1, nT - 1), 0)),
                  pl.BlockSpec(dw.shape, lambda i: (0, 0))],
        out_specs=[pl.BlockSpec((T, 2 * CONV_CH), lambda i: (i, 0)), pl.BlockSpec(dw.shape, lambda i: (0, 0))],
        out_shape=[jax.ShapeDtypeStruct((S, 2 * CONV_CH), bf16), jax.ShapeDtypeStruct(dw.shape, f32)],
        scratch_shapes=[pltpu.VMEM((T + CONV_HALO + SUBLANES, CONV_CH), f32)] * 2
        + [pltpu.VMEM((SUBLANES, T + CONV_HALO, CONV_CH), f32)] * 2,
        compiler_params=_params(("arbitrary",)))(u, u, u, u, dy, dy, dw)


def rope_tables(pos_col):
    S = pos_col.shape[0]
    T = min(512, S)
    inv_freq = ROPE_BASE ** (-jnp.arange(0, MLA_ROPE, 2, dtype=f32) / MLA_ROPE)
    zeros = jnp.zeros((MLA_NOPE,), f32)
    inv_row = jnp.concatenate([zeros, inv_freq, inv_freq, zeros[:MLA_PAD - MLA_QK]]).reshape(1, MLA_PAD)

    def body(p_ref, f_ref, c_ref, s_ref):
        lane = lax.broadcasted_iota(jnp.int32, (T, MLA_PAD), 1)
        ang = p_ref[...].astype(f32) * f_ref[...]
        rot = jnp.logical_and(lane >= MLA_NOPE, lane < MLA_QK)
        c_ref[...] = jnp.where(rot, jnp.cos(ang), jnp.where(lane < MLA_NOPE, 1.0, 0.0))
        s_ref[...] = jnp.where(rot, jnp.sin(ang), 0.0)

    spec = pl.BlockSpec((T, MLA_PAD), lambda i: (i, 0))
    return pl.pallas_call(
        body, name="rope_tables", grid=(S // T,),
        in_specs=[pl.BlockSpec((T, 1), lambda i: (i, 0)), pl.BlockSpec((1, MLA_PAD), lambda i: (0, 0))],
        out_specs=[spec, spec], out_shape=[jax.ShapeDtypeStruct((S, MLA_PAD), f32)] * 2,
        compiler_params=_params(("parallel",)))(pos_col, inv_row)


def loss_head(y, target):
    S, D = y.shape
    T = min(512, S)

    def body(y_ref, t_ref, dy_ref, l_ref):
        i = pl.program_id(0)
        err = y_ref[...] - t_ref[...]
        dy_ref[...] = err * (1.0 / D)
        part = 0.5 * jnp.sum(jnp.sum(err * err, axis=-1, keepdims=True) * (1.0 / D), axis=0, keepdims=True)
        part = jnp.broadcast_to(part, l_ref.shape)

        @pl.when(i == 0)
        def _():
            l_ref[...] = part

        @pl.when(i > 0)
        def _():
            l_ref[...] += part

    spec = pl.BlockSpec((T, D), lambda i: (i, 0))
    return pl.pallas_call(
        body, name="loss_head", grid=(S // T,), in_specs=[spec, spec],
        out_specs=[spec, pl.BlockSpec((8, 128), lambda i: (0, 0))],
        out_shape=[jax.ShapeDtypeStruct((S, D), f32), jax.ShapeDtypeStruct((8, 128), f32)],
        compiler_params=_params(("arbitrary",)))(y, target)


def _row(a, w=None, off=0, st=0):
    return (a, w or a.shape[1], off, st)


def _out(tw, dt, w=None, off=0, st=0):
    return (tw, dt, w or tw, off, st)


ACT = bf16


def ffn_fwd(tag, x, g, w_in, w_out):
    (h,) = rowwise(f"{tag}_rms", fn_rms, [_row(x)], [g], [_out(D_MODEL, bf16)])
    u = mm(f"{tag}_in", h, w_in, "nn", out_dtype=ACT)
    (a,) = rowwise(f"{tag}_swiglu", fn_swiglu, [_row(u, FFN_HIDDEN, 0), _row(u, FFN_HIDDEN, 1)], [],
                   [_out(FFN_HIDDEN, bf16)])
    y = mm(f"{tag}_out", a, w_out, "nn", alpha=0.5, res=x)
    return y, (x, h, u, a)


def ffn_bwd(tag, saved, g, w_in, w_out, dy):
    x, h, u, a = saved
    d_w_out = mm(f"{tag}_dwout", a, dy, "tn", alpha=0.5)
    da = mm(f"{tag}_da", dy, w_out, "nt", alpha=0.5, out_dtype=ACT)
    (du,), _ = rowwise_bwd(f"{tag}_dswiglu", fn_swiglu, [_row(u, FFN_HIDDEN, 0), _row(u, FFN_HIDDEN, 1)], [],
                           [_row(da)], [_out(2 * FFN_HIDDEN, bf16)], merge=[(0, 1)], T=256)
    d_w_in = mm(f"{tag}_dwin", h, du, "tn")
    dh = mm(f"{tag}_dh", du, w_in, "nt", out_dtype=ACT)
    (dx,), (dg,) = rowwise_bwd(f"{tag}_drms", fn_rms_res, [_row(x)], [g], [_row(dy), _row(dh)], [_out(D_MODEL, f32)])
    return dx, dg, d_w_in, d_w_out


def _seg(name):
    for n, _, w, start in _U_SEGS:
        if n == name:
            return start, w
    raise KeyError(name)


def mix_fwd(tag, x, mem_n_in, tabs, p, ride=None):
    cos_t, sin_t = tabs
    (h,) = rowwise(f"{tag}_rms", fn_rms, [_row(x)], [p["mix_norm"]], [_out(D_MODEL, bf16)])
    u = mm(f"{tag}_in", h, p["w_in"], "nn", out_dtype=ACT)
    yc = conv_fwd(f"{tag}_conv", u, p["conv_dw"])
    (br_a,) = rowwise(f"{tag}_lnsilu", fn_ln_silu, [_row(yc)], [p["conv_b"], p["conv_ln_g"], p["conv_ln_b"]],
                      [_out(BRANCH_WIDTH, bf16)])
    sb0 = _seg("sb")[0] // HEAD_DIM
    qs, ks = rowwise(f"{tag}_sbprep", fn_sb_prep, [_row(u, HEAD_DIM, sb0, 1), _row(u, HEAD_DIM, sb0 + HEADS, 1)],
                     [p["sb_q_hnorm"], p["sb_k_hnorm"]],
                     [_out(BRANCH_WIDTH, bf16, HEAD_DIM, 0, 1), _out(BRANCH_WIDTH, bf16, HEAD_DIM, 0, 1)], H=HEADS)
    sb_v = _row(u, HEAD_DIM, sb0 + 2 * HEADS, 1)
    br_b, tot_b, cnt_b = sb_fwd(f"{tag}_sb", _row(qs, HEAD_DIM, 0, 1), _row(ks, HEAD_DIM, 0, 1), sb_v)
    ql_n, kvl_n = rowwise(f"{tag}_latrms", lambda a, b, ga, gb: (_rms(a, ga), _rms(b, gb)),
                          [_row(u, MLA_Q_LORA, _seg("qlat")[0] // MLA_Q_LORA), _row(u, MLA_KV_LORA, _seg("kvlat")[0] // MLA_KV_LORA)],
                          [p["mla_q_norm"], p["mla_kv_norm"]], [_out(MLA_Q_LORA, bf16), _out(MLA_KV_LORA, bf16)])
    qfull = mm(f"{tag}_uq", ql_n, p["mla_w_uq"], "nn", out_dtype=ACT)
    kvfull = mm(f"{tag}_ukv", kvl_n, p["mla_w_ukv"], "nn", out_dtype=ACT)
    kr_row = _row(u, HEAD_DIM, _seg("krope")[0] // HEAD_DIM, 0)
    (qr,) = rowwise(f"{tag}_mlaq", fn_mla_q, [_row(qfull, MLA_PAD, 0, 1), _row(cos_t), _row(sin_t)], [p["mla_q_hnorm"]],
                    [_out(HEADS * MLA_PAD, bf16, MLA_PAD, 0, 1)], H=HEADS)
    (kr,) = rowwise(f"{tag}_mlak", fn_mla_k, [_row(kvfull, HEAD_DIM, 0, 2), kr_row, _row(cos_t), _row(sin_t)],
                    [p["mla_k_hnorm"]], [_out(HEADS * MLA_PAD, bf16, MLA_PAD, 0, 1)], H=HEADS)
    mla_v = _row(kvfull, HEAD_DIM, 1, 2)
    br_c, lse_c, *landed = attn_fwd(f"{tag}_mla", _row(qr, MLA_PAD, 0, 1), _row(kr, MLA_PAD, 0, 1), mla_v,
                                    scale=MLA_QK ** -0.5, mask="chunk", ride=ride)
    (mem_n,) = rowwise(f"{tag}_memrms", fn_rms, [_row(mem_n_in)], [p["mem_norm"]], [_out(D_MODEL, bf16)])
    kvm = mm(f"{tag}_memkv", mem_n, p["mem_w_kv"], "nn", out_dtype=ACT)
    (km,) = rowwise(f"{tag}_memk", fn_rms, [_row(kvm, HEAD_DIM, 0, 1)], [p["mem_k_hnorm"]],
                    [_out(BRANCH_WIDTH, bf16, HEAD_DIM, 0, 1)], H=HEADS)
    mq0 = _seg("memq")[0] // HEAD_DIM
    (qm,) = rowwise(f"{tag}_memq", fn_rms, [_row(u, HEAD_DIM, mq0, 1)], [p["mem_q_hnorm"]],
                    [_out(BRANCH_WIDTH, bf16, HEAD_DIM, 0, 1)], H=HEADS)
    mem_v = _row(kvm, HEAD_DIM, HEADS, 1)
    br_d, lse_d = attn_fwd(f"{tag}_memattn", _row(qm, HEAD_DIM, 0, 1), _row(km, HEAD_DIM, 0, 1), mem_v,
                           scale=HEAD_DIM ** -0.5, mask=None)
    branches = (br_a, br_b, br_c, br_d)
    proj = [mm(f"{tag}_branch{b}", branches[b], p["w_branch"][b], "nn", out_dtype=ACT) for b in range(N_BRANCH)]
    gate_rows = [_row(u, D_MODEL, b) for b in range(N_BRANCH)]
    (merged,) = rowwise(f"{tag}_merge", fn_merge, gate_rows + [_row(t) for t in proj], [], [_out(D_MODEL, bf16)], T=256)
    y = mm(f"{tag}_out", merged, p["w_out"], "nn", res=x)
    saved = dict(x=x, h=h, u=u, yc=yc, qs=qs, ks=ks, ql_n=ql_n, kvl_n=kvl_n, qfull=qfull, kvfull=kvfull, qr=qr, kr=kr,
                 lse_c=lse_c, mem_n=mem_n, kvm=kvm, km=km, qm=qm, lse_d=lse_d, branches=branches, proj=proj,
                 merged=merged, tot_b=tot_b, cnt_b=cnt_b)
    return y, saved, (landed[0] if landed else None)


def mix_bwd(tag, sv, mem_n_in, tabs, p, dy, reducer=None):
    cos_t, sin_t = tabs
    u, S = sv["u"], sv["u"].shape[0]
    g = {}
    g["w_out"] = mm(f"{tag}_dwout", sv["merged"], dy, "tn")
    dmerged = mm(f"{tag}_dmerged", dy, p["w_out"], "nt", out_dtype=ACT)
    gate_rows = [_row(u, D_MODEL, b) for b in range(N_BRANCH)]
    d_merge, _ = rowwise_bwd(f"{tag}_dmerge", fn_merge, gate_rows + [_row(t) for t in sv["proj"]], [], [_row(dmerged)],
                             [_out(N_BRANCH * D_MODEL, bf16)] + [_out(D_MODEL, bf16)] * N_BRANCH, T=256,
                             merge=[tuple(range(N_BRANCH))] + [(N_BRANCH + b,) for b in range(N_BRANCH)])
    d_gates, d_proj = d_merge[:1], d_merge[1:]
    g["w_branch"] = [mm(f"{tag}_dwbranch{b}", sv["branches"][b], d_proj[b], "tn") for b in range(N_BRANCH)]
    d_br = [mm(f"{tag}_dbranch{b}", d_proj[b], p["w_branch"][b], "nt", out_dtype=ACT) for b in range(N_BRANCH)]
    (dyc,), (g["conv_b"], g["conv_ln_g"], g["conv_ln_b"]) = rowwise_bwd(
        f"{tag}_dlnsilu", fn_ln_silu, [_row(sv["yc"])], [p["conv_b"], p["conv_ln_g"], p["conv_ln_b"]], [_row(d_br[0])],
        [_out(BRANCH_WIDTH, f32)])
    du_conv, g["conv_dw"] = conv_bwd(f"{tag}_dconv", u, dyc, p["conv_dw"])
    sb0 = _seg("sb")[0] // HEAD_DIM
    sb_v = _row(u, HEAD_DIM, sb0 + 2 * HEADS, 1)
    dqs, dks, dv_sb, *landed = sb_bwd(f"{tag}_dsb", _row(sv["qs"], HEAD_DIM, 0, 1), _row(sv["ks"], HEAD_DIM, 0, 1), sb_v,
                                      sv["tot_b"], sv["cnt_b"], d_br[1],
                                      ride=("sibling", reducer.sibling_src()) if reducer else None)
    scatter_src = reducer.after_sibling(landed[0]) if reducer else None
    (du_sbq, du_sbk), (g["sb_q_hnorm"], g["sb_k_hnorm"]) = rowwise_bwd(
        f"{tag}_dsbprep", fn_sb_prep, [_row(u, HEAD_DIM, sb0, 1), _row(u, HEAD_DIM, sb0 + HEADS, 1)],
        [p["sb_q_hnorm"], p["sb_k_hnorm"]], [_row(dqs, HEAD_DIM, 0, 1), _row(dks, HEAD_DIM, 0, 1)],
        [_out(BRANCH_WIDTH, bf16, HEAD_DIM, 0, 1), _out(BRANCH_WIDTH, bf16, HEAD_DIM, 0, 1)], H=HEADS)
    mla_v = _row(sv["kvfull"], HEAD_DIM, 1, 2)
    dqr, dkr, dv_mla, *landed = attn_bwd(f"{tag}_dmla", _row(sv["qr"], MLA_PAD, 0, 1), _row(sv["kr"], MLA_PAD, 0, 1), mla_v,
                                         sv["branches"][2], d_br[2], sv["lse_c"], scale=MLA_QK ** -0.5, mask="chunk",
                                         ride=("scatter", scatter_src) if reducer else None)
    if reducer:
        reducer.after_scatter(landed[0])
    (dqfull,), (g["mla_q_hnorm"],) = rowwise_bwd(
        f"{tag}_dmlaq", fn_mla_q, [_row(sv["qfull"], MLA_PAD, 0, 1), _row(cos_t), _row(sin_t)], [p["mla_q_hnorm"]],
        [_row(dqr, MLA_PAD, 0, 1)], [_out(HEADS * MLA_PAD, bf16, MLA_PAD, 0, 1)], H=HEADS, nondiff=(1, 2))
    kr_row = _row(u, HEAD_DIM, _seg("krope")[0] // HEAD_DIM, 0)
    (dkn, du_krope, dvp), (g["mla_k_hnorm"],) = rowwise_bwd(
        f"{tag}_dmlak", fn_mla_k_v, [_row(sv["kvfull"], HEAD_DIM, 0, 2), kr_row, mla_v, _row(cos_t), _row(sin_t)],
        [p["mla_k_hnorm"]], [_row(dkr, MLA_PAD, 0, 1), _row(dv_mla, HEAD_DIM, 0, 1)],
        [_out(BRANCH_WIDTH, bf16, HEAD_DIM, 0, 1), _out(HEAD_DIM, f32), _out(BRANCH_WIDTH, bf16, HEAD_DIM, 0, 1)],
        H=HEADS, nondiff=(3, 4))
    dkvfull = _interleave(f"{tag}_dkvfull", dkn, dvp)
    g["mla_w_uq"] = mm(f"{tag}_dwuq", sv["ql_n"], dqfull, "tn")
    g["mla_w_ukv"] = mm(f"{tag}_dwukv", sv["kvl_n"], dkvfull, "tn")
    dql_n = mm(f"{tag}_dqln", dqfull, p["mla_w_uq"], "nt", out_dtype=ACT)
    dkvl_n = mm(f"{tag}_dkvln", dkvfull, p["mla_w_ukv"], "nt", out_dtype=ACT)
    (du_lat,), (g["mla_q_norm"], g["mla_kv_norm"]) = rowwise_bwd(
        f"{tag}_dlatrms", lambda a, b, ga, gb: (_rms(a, ga), _rms(b, gb)),
        [_row(u, MLA_Q_LORA, _seg("qlat")[0] // MLA_Q_LORA), _row(u, MLA_KV_LORA, _seg("kvlat")[0] // MLA_KV_LORA)],
        [p["mla_q_norm"], p["mla_kv_norm"]], [_row(dql_n), _row(dkvl_n)], [_out(MLA_Q_LORA + MLA_KV_LORA, bf16)],
        merge=[(0, 1)])
    mem_v = _row(sv["kvm"], HEAD_DIM, HEADS, 1)
    dqm, dkm, dvm = attn_bwd(f"{tag}_dmemattn", _row(sv["qm"], HEAD_DIM, 0, 1), _row(sv["km"], HEAD_DIM, 0, 1), mem_v,
                             sv["branches"][3], d_br[3], sv["lse_d"], scale=HEAD_DIM ** -0.5, mask=None)
    mq0 = _seg("memq")[0] // HEAD_DIM
    (du_memq,), (g["mem_q_hnorm"],) = rowwise_bwd(
        f"{tag}_dmemq", fn_rms, [_row(u, HEAD_DIM, mq0, 1)], [p["mem_q_hnorm"]], [_row(dqm, HEAD_DIM, 0, 1)],
        [_out(BRANCH_WIDTH, bf16, HEAD_DIM, 0, 1)], H=HEADS)
    (dkvm_k, dkvm_v), (g["mem_k_hnorm"],) = rowwise_bwd(
        f"{tag}_dmemk", fn_mem_k_v, [_row(sv["kvm"], HEAD_DIM, 0, 1), mem_v], [p["mem_k_hnorm"]],
        [_row(dkm, HEAD_DIM, 0, 1), _row(dvm, HEAD_DIM, 0, 1)],
        [_out(BRANCH_WIDTH, bf16, HEAD_DIM, 0, 1), _out(BRANCH_WIDTH, bf16, HEAD_DIM, 0, 1)], H=HEADS)
    dkvm = jnp.concatenate([dkvm_k, dkvm_v], axis=1)
    g["mem_w_kv"] = mm(f"{tag}_dwmemkv", sv["mem_n"], dkvm, "tn")
    dmem_n = mm(f"{tag}_dmemn", dkvm, p["mem_w_kv"], "nt", out_dtype=ACT)
    _, (g["mem_norm"],) = rowwise_bwd(f"{tag}_dmemrms", fn_rms, [_row(mem_n_in)], [p["mem_norm"]], [_row(dmem_n)],
                                      [_out(D_MODEL, bf16)])
    du_krope_b = du_krope.astype(bf16)
    du = jnp.concatenate(list(d_gates) + [du_conv, du_sbq, du_sbk, dv_sb.astype(bf16), du_lat, du_memq,
                                          du_krope_b, jnp.zeros((S, U_WIDTH - _seg("krope")[0] - HEAD_DIM), bf16)], axis=1)
    g["w_in"] = mm(f"{tag}_dwin", sv["h"], du, "tn")
    dh = mm(f"{tag}_dh", du, p["w_in"], "nt", out_dtype=ACT)
    (dx,), (g["mix_norm"],) = rowwise_bwd(f"{tag}_drms", fn_rms_res, [_row(sv["x"])], [p["mix_norm"]],
                                          [_row(dy), _row(dh)], [_out(D_MODEL, f32)])
    return dx, g


def _interleave(name, a, b):
    S, W = a.shape
    T = min(512, S)

    def body(a_ref, b_ref, o_ref):
        o_ref[:, :HEAD_DIM] = a_ref[...]
        o_ref[:, HEAD_DIM:] = b_ref[...]

    blk = pl.BlockSpec((T, HEAD_DIM), lambda i, h: (i, h))
    return pl.pallas_call(
        body, name=name, grid=(S // T, W // HEAD_DIM), in_specs=[blk, blk],
        out_specs=pl.BlockSpec((T, 2 * HEAD_DIM), lambda i, h: (i, h)),
        out_shape=jax.ShapeDtypeStruct((S, 2 * W), a.dtype), compiler_params=_params(("parallel", "parallel")))(a, b)


def _u_layout(w):
    parts, at = [], 0
    for _, src, width, start in _U_SEGS:
        assert start == at
        parts.append(w[..., src:src + width])
        at += width
    parts.append(jnp.zeros(w.shape[:-1] + (U_WIDTH - at,), w.dtype))
    return jnp.concatenate(parts, axis=-1)


def _u_layout_inv(g):
    order = sorted(_U_SEGS, key=lambda s: s[1])
    return jnp.concatenate([g[..., start:start + width] for _, _, width, start in order], axis=-1)


def _pad_heads(w, n=MLA_QK, to=MLA_PAD):
    w = w.reshape(w.shape[:-1] + (HEADS, n))
    w = jnp.pad(w, [(0, 0)] * (w.ndim - 1) + [(0, to - n)])
    return w.reshape(w.shape[:-2] + (HEADS * to,))


def _unpad_heads(g, n=MLA_QK, to=MLA_PAD):
    g = g.reshape(g.shape[:-1] + (HEADS, to))[..., :n]
    return g.reshape(g.shape[:-2] + (HEADS * n,))


def layer_params(W):
    row = lambda name: W[name].reshape(1, -1).astype(f32)
    p = {n: row(n) for n in SMALL if n != "mla_q_hnorm" and n != "mla_k_hnorm"}
    for n in ("mla_q_hnorm", "mla_k_hnorm"):
        p[n] = jnp.pad(row(n), ((0, 0), (0, MLA_PAD - MLA_QK)))
    for n in ("ffn1_w_in", "ffn1_w_out", "ffn2_w_in", "ffn2_w_out", "mla_w_ukv", "mem_w_kv", "w_out"):
        p[n] = W[n]
    p["w_branch"] = [W["w_branch"][b] for b in range(N_BRANCH)]
    p["w_in"] = _u_layout(W["w_in"])
    p["mla_w_uq"] = _pad_heads(W["mla_w_uq"])
    p["conv_dw"] = jnp.pad(W["conv_dw"].astype(f32), ((0, 1), (0, 0)))
    return p


def layer_grads_to_original(g):
    out = dict(g)
    out["w_in"] = _u_layout_inv(g["w_in"])
    out["mla_w_uq"] = _unpad_heads(g["mla_w_uq"])
    out["conv_dw"] = g["conv_dw"][:CONV_WIDTH]
    out["w_branch"] = jnp.stack(g["w_branch"])
    for n in ("mla_q_hnorm", "mla_k_hnorm"):
        out[n] = g[n][:, :MLA_QK]
    return {n: (out[n].reshape(-1) if n in SMALL else out[n]) for n in out}


def local_step(x, mem, pos_col, target, weights_of, ride_of=lambda l: None, deliver=lambda l, landed: None,
               reducer_of=lambda l, grads: None):
    tabs = rope_tables(pos_col)
    params, saved = [], []
    for l in range(DEPTH):
        p = layer_params(weights_of(l))
        params.append(p)
        x, s1 = ffn_fwd(f"l{l}_ffn1", x, p["ffn1_norm"], p["ffn1_w_in"], p["ffn1_w_out"])
        x, s2, landed = mix_fwd(f"l{l}_mix", x, mem, tabs, p, ride=ride_of(l))
        deliver(l, landed)
        x, s3 = ffn_fwd(f"l{l}_ffn2", x, p["ffn2_norm"], p["ffn2_w_in"], p["ffn2_w_out"])
        saved.append((s1, s2, s3))
    dx, loss_blk = loss_head(x, target)
    grads = [None] * DEPTH
    reducer = None
    for l in reversed(range(DEPTH)):
        p, (s1, s2, s3) = params[l], saved[l]
        dx, g_n2, g_in2, g_out2 = ffn_bwd(f"l{l}_ffn2", s3, p["ffn2_norm"], p["ffn2_w_in"], p["ffn2_w_out"], dx)
        dx, g = mix_bwd(f"l{l}_mix", s2, mem, tabs, p, dx, reducer)
        dx, g_n1, g_in1, g_out1 = ffn_bwd(f"l{l}_ffn1", s1, p["ffn1_norm"], p["ffn1_w_in"], p["ffn1_w_out"], dx)
        g.update(ffn1_norm=g_n1, ffn1_w_in=g_in1, ffn1_w_out=g_out1, ffn2_norm=g_n2, ffn2_w_in=g_in2, ffn2_w_out=g_out2)
        grads[l] = layer_grads_to_original(g)
        reducer = reducer_of(l, grads[l])
    if reducer:
        reducer.alone()
    return loss_blk, dx, grads


_ANY = pl.BlockSpec(memory_space=pl.ANY)
_COMM = pltpu.CompilerParams(has_side_effects=True)


def _coords():
    return lax.axis_index("x"), lax.axis_index("y"), lax.axis_index("c")


def chip_exchange(name, src, scatter):
    shape = src.shape[1:] if scatter else src.shape

    def body(src_ref, out_ref, send_sems, recv_sems):
        x, y, c = _coords()
        copies = []
        for k, (px, py) in enumerate([(1 - x, y), (x, 1 - y), (1 - x, 1 - y)]):
            piece = src_ref.at[2 * px + py] if scatter else src_ref
            cp = pltpu.make_async_remote_copy(piece, out_ref.at[k], send_sems.at[k], recv_sems.at[k],
                                              device_id=(px, py, c), device_id_type=MESH)
            cp.start()
            copies.append(cp)
        for cp in copies:
            cp.wait()

    return pl.pallas_call(
        body, name=name, in_specs=[_ANY], out_specs=_ANY, out_shape=jax.ShapeDtypeStruct((3,) + shape, src.dtype),
        scratch_shapes=[pltpu.SemaphoreType.DMA((3,)), pltpu.SemaphoreType.DMA((3,))], compiler_params=_COMM)(src)


def sibling_exchange(name, src):
    def body(src_ref, out_ref, send_sem, recv_sem):
        x, y, c = _coords()
        cp = pltpu.make_async_remote_copy(src_ref, out_ref, send_sem, recv_sem, device_id=(x, y, 1 - c),
                                          device_id_type=MESH)
        cp.start()
        cp.wait()

    return pl.pallas_call(
        body, name=name, in_specs=[_ANY], out_specs=_ANY, out_shape=jax.ShapeDtypeStruct(src.shape, src.dtype),
        scratch_shapes=[pltpu.SemaphoreType.DMA, pltpu.SemaphoreType.DMA], compiler_params=_COMM)(src)


def all8_gather(name, src):
    def body(src_ref, out_ref, send_sems, recv_sems, local_sem):
        x, y, c = _coords()
        me = 4 * x + 2 * y + c
        mine = pltpu.make_async_copy(src_ref, out_ref.at[me], local_sem)
        mine.start()
        copies = []
        for k in range(1, 8):
            peer = (1 - x if k & 4 else x, 1 - y if k & 2 else y, 1 - c if k & 1 else c)
            cp = pltpu.make_async_remote_copy(src_ref, out_ref.at[me], send_sems.at[k - 1], recv_sems.at[k - 1],
                                              device_id=peer, device_id_type=MESH)
            cp.start()
            copies.append(cp)
        for cp in copies:
            cp.wait()
        mine.wait()

    return pl.pallas_call(
        body, name=name, in_specs=[_ANY], out_specs=_ANY, out_shape=jax.ShapeDtypeStruct((8,) + src.shape, src.dtype),
        scratch_shapes=[pltpu.SemaphoreType.DMA((7,)), pltpu.SemaphoreType.DMA((7,)), pltpu.SemaphoreType.DMA],
        compiler_params=_COMM)(src)


def sum8(name, g):
    def body(g_ref, o_ref):
        acc = g_ref[0]
        for k in range(1, 8):
            acc = acc + g_ref[k]
        o_ref[...] = acc

    return pl.pallas_call(body, name=name, out_shape=jax.ShapeDtypeStruct(g.shape[1:], g.dtype))(g)


PACK_COLS = 1024
PACKED = tuple(n for n in SHARDED if n != "conv_dw")
PACK_ROW_GRAIN = PACK_GRAIN // PACK_COLS


def _rows(shape):
    n = math.prod(shape)
    assert n % (16 * PACK_COLS) == 0, shape
    return n // PACK_COLS


def _pack_rows(pieces):
    rows = sum(p.shape[0] for p in pieces)
    pad = -rows % PACK_ROW_GRAIN
    if pad:
        pieces = pieces + [jnp.zeros((pad, PACK_COLS), pieces[0].dtype)]
    return jnp.concatenate(pieces, axis=0)


def _unpack_rows(packed, shard_shapes):
    out, at = {}, 0
    for n in PACKED:
        r = _rows(shard_shapes[n])
        out[n] = packed[at:at + r].reshape(shard_shapes[n])
        at += r
    return out


def gather_shards(name, own):
    n_chunks = 4
    rows = own.shape[1] // n_chunks
    assert own.shape[1] % (16 * n_chunks) == 0

    def body(own_ref, out_ref, send_sems, recv_sems, local_sem):
        x, y, c = _coords()
        sib = (x, y, 1 - c)
        mine = pltpu.make_async_copy(own_ref, out_ref.at[0], local_sem)
        mine.start()
        chips = [(2, (1 - x, y)), (1, (x, 1 - y)), (3, (1 - x, 1 - y))]

        def copy(k, j, src, m, half, to):
            sl = pl.ds(j * rows, rows)
            return pltpu.make_async_remote_copy(src.at[sl], out_ref.at[m, half, sl], send_sems.at[k * n_chunks + j],
                                                recv_sems.at[k * n_chunks + j], device_id=to, device_id_type=MESH)

        first = [[copy(k, j, own_ref.at[c], m, c, (px, py, c)) for j in range(n_chunks)]
                 for k, (m, (px, py)) in enumerate(chips)]
        for j in range(n_chunks):
            for k in range(3):
                first[k][j].start()
        passed = []
        for j in range(n_chunks):
            for k, (m, _) in enumerate(chips):
                first[k][j].wait_recv()
                cp = copy(3 + k, j, out_ref.at[m, c], m, c, sib)
                cp.start()
                passed.append(cp)
        for j in range(n_chunks):
            for k, (m, _) in enumerate(chips):
                copy(3 + k, j, out_ref.at[m, 1 - c], m, 1 - c, sib).wait_recv()
        for cp in [cp for per_chip in first for cp in per_chip] + passed:
            cp.wait_send()
        mine.wait()

    return pl.pallas_call(
        body, name=name, in_specs=[_ANY], out_specs=_ANY, out_shape=jax.ShapeDtypeStruct((4,) + own.shape, own.dtype),
        scratch_shapes=[pltpu.SemaphoreType.DMA((6 * n_chunks,)), pltpu.SemaphoreType.DMA((6 * n_chunks,)),
                        pltpu.SemaphoreType.DMA],
        compiler_params=_COMM)(own)


def pack_layer_shard(w, l):
    return _pack_rows([w[n][l].astype(bf16).reshape(-1, PACK_COLS) for n in PACKED]).reshape(2, -1, PACK_COLS)


def unpack_layer(by_mask, w):
    x, y, _ = _coords()
    me = 2 * x + y
    shapes = {n: w[n].shape[1:] for n in PACKED}
    pieces = [_unpack_rows(lax.dynamic_index_in_dim(by_mask, jnp.bitwise_xor(s, me), axis=0, keepdims=False)
                           .reshape(-1, PACK_COLS), shapes) for s in range(4)]
    return {n: jnp.concatenate([pieces[s][n] for s in range(4)], axis=SHARD_AXIS[n] - 1) for n in PACKED}


def gather_conv_dw(dw):
    rows = math.prod(dw.shape[:-1])
    every = all8_gather("ag_conv_dw", jnp.pad(dw.reshape(rows, -1), ((0, -rows % 8), (0, 0))))
    return jnp.concatenate([every[2 * s, :rows].reshape(dw.shape) for s in range(4)], axis=SHARD_AXIS["conv_dw"])


def _add_streams(name, ins, selectors, out_dtypes, rows, T=256):
    n_streams = max([a.shape[sel.index("s")] for a, sel in zip(ins, selectors) if "s" in sel] + [1])

    def spec(sel):
        def index(s, i, pf):
            lead = tuple(s if e == "s" else (pf[e[1]] if isinstance(e, tuple) else e) for e in sel)
            return lead + (i, 0)
        return pl.BlockSpec((None,) * len(sel) + (T, PACK_COLS), index)

    def body(pf_ref, *refs):
        acc = refs[0][...].astype(f32)
        for r in refs[1:len(ins)]:
            acc = acc + r[...].astype(f32)
        for o in refs[len(ins):]:
            o[...] = acc.astype(o.dtype)

    def run(pf):
        grid_spec = pltpu.PrefetchScalarGridSpec(
            num_scalar_prefetch=1, grid=(n_streams, rows // T), in_specs=[spec(sel) for sel in selectors],
            out_specs=[spec(("s",)) for _ in out_dtypes])
        return pl.pallas_call(
            body, name=name, grid_spec=grid_spec,
            out_shape=[jax.ShapeDtypeStruct((n_streams, rows, PACK_COLS), dt) for dt in out_dtypes],
            compiler_params=_params(("parallel", "parallel")))(pf, *ins)
    return run


class LayerReduce:
    def __init__(self, tag, grads):
        x, y, c = _coords()
        self.tag, self.c = tag, c
        streams = []
        for s in range(4):
            pieces = []
            for n in PACKED:
                ax = SHARD_AXIS[n] - 1
                width = grads[n].shape[ax] // 4
                pieces.append(lax.slice_in_dim(grads[n], s * width, (s + 1) * width, axis=ax).reshape(-1, PACK_COLS))
            streams.append(_pack_rows(pieces))
        self.rows = streams[0].shape[0] // 2
        self.G = jnp.concatenate(streams, axis=0).reshape(4, 2, self.rows, PACK_COLS)
        self.pf = jnp.stack([c, 2 * x + y]).astype(jnp.int32)

    def sibling_src(self):
        return lax.dynamic_index_in_dim(self.G, 1 - self.c, axis=1, keepdims=False).astype(bf16)

    def after_sibling(self, from_sib):
        self.chip_sum, chip_sum_b = _add_streams(f"{self.tag}_add_sibling", [self.G, from_sib],
                                                 [("s", ("pf", 0)), ("s",)], [f32, bf16], self.rows)(self.pf)
        return chip_sum_b

    def after_scatter(self, got):
        (half,) = _add_streams(f"{self.tag}_add_chips", [self.chip_sum, got, got, got],
                               [(("pf", 1),), (0,), (1,), (2,)], [f32], self.rows)(self.pf)
        self.half = half[0]

    def alone(self):
        chip_sum_b = self.after_sibling(sibling_exchange(f"{self.tag}_sibling", self.sibling_src()))
        self.after_scatter(chip_exchange(f"{self.tag}_chips", chip_sum_b, scatter=True))


def finish_reduce(reducers, w):
    c = reducers[0].c
    mine = jnp.concatenate([r.half for r in reducers], axis=0)
    other = sibling_exchange("rs_final", mine)
    lower, upper = jnp.where(c == 0, mine, other), jnp.where(c == 0, other, mine)
    shapes = {n: w[n].shape[1:] for n in PACKED}
    per_layer, at = [], 0
    for r in reducers:
        shard = jnp.concatenate([lower[at:at + r.rows], upper[at:at + r.rows]], axis=0)
        per_layer.append(_unpack_rows(shard, shapes))
        at += r.rows
    return {n: jnp.stack([p[n] for p in per_layer]) for n in PACKED}


def _small_pack(t, names=SMALL):
    flat = jnp.concatenate([t[n].reshape(-1) for n in names])
    total = -(-flat.shape[0] // SMALL_PAD) * SMALL_PAD
    return jnp.pad(flat, (0, total - flat.shape[0])).reshape(-1, 128)


def _small_unpack(a, shapes, names=SMALL):
    flat, out, at = a.reshape(-1), {}, 0
    for n in names:
        k = math.prod(shapes[n])
        out[n] = flat[at:at + k].reshape(shapes[n])
        at += k
    return out


def adamw(name, w, g, m, v):
    shape = w.shape
    two = lambda a: a.reshape(-1, shape[-1])
    rows = two(w).shape[0]
    T = 256 if rows % 256 == 0 else rows
    outs = rowwise(name, fn_adamw, [_row(two(w)), _row(two(g)), _row(two(m)), _row(two(v))], [],
                   [_out(shape[-1], f32)] * 3, T=T)
    return [o.reshape(shape) for o in outs]


def kernel(x, mem, positions, ffn1_norm, ffn1_w_in, ffn1_w_out, mix_norm, w_in, conv_dw, conv_b, conv_ln_g, conv_ln_b, sb_q_hnorm, sb_k_hnorm, mla_q_norm, mla_w_uq, mla_kv_norm, mla_w_ukv, mla_q_hnorm, mla_k_hnorm, mem_norm, mem_w_kv, mem_q_hnorm, mem_k_hnorm, w_branch, w_out, ffn2_norm, ffn2_w_in, ffn2_w_out, loss_target, m_ffn1_norm, m_ffn1_w_in, m_ffn1_w_out, m_mix_norm, m_w_in, m_conv_dw, m_conv_b, m_conv_ln_g, m_conv_ln_b, m_sb_q_hnorm, m_sb_k_hnorm, m_mla_q_norm, m_mla_w_uq, m_mla_kv_norm, m_mla_w_ukv, m_mla_q_hnorm, m_mla_k_hnorm, m_mem_norm, m_mem_w_kv, m_mem_q_hnorm, m_mem_k_hnorm, m_w_branch, m_w_out, m_ffn2_norm, m_ffn2_w_in, m_ffn2_w_out, v_ffn1_norm, v_ffn1_w_in, v_ffn1_w_out, v_mix_norm, v_w_in, v_conv_dw, v_conv_b, v_conv_ln_g, v_conv_ln_b, v_sb_q_hnorm, v_sb_k_hnorm, v_mla_q_norm, v_mla_w_uq, v_mla_kv_norm, v_mla_w_ukv, v_mla_q_hnorm, v_mla_k_hnorm, v_mem_norm, v_mem_w_kv, v_mem_q_hnorm, v_mem_k_hnorm, v_w_branch, v_w_out, v_ffn2_norm, v_ffn2_w_in, v_ffn2_w_out):
    w = dict(zip(WEIGHTS, (ffn1_norm, ffn1_w_in, ffn1_w_out, mix_norm, w_in, conv_dw, conv_b, conv_ln_g, conv_ln_b, sb_q_hnorm, sb_k_hnorm, mla_q_norm, mla_w_uq, mla_kv_norm, mla_w_ukv, mla_q_hnorm, mla_k_hnorm, mem_norm, mem_w_kv, mem_q_hnorm, mem_k_hnorm, w_branch, w_out, ffn2_norm, ffn2_w_in, ffn2_w_out)))
    m = dict(zip(WEIGHTS, (m_ffn1_norm, m_ffn1_w_in, m_ffn1_w_out, m_mix_norm, m_w_in, m_conv_dw, m_conv_b, m_conv_ln_g, m_conv_ln_b, m_sb_q_hnorm, m_sb_k_hnorm, m_mla_q_norm, m_mla_w_uq, m_mla_kv_norm, m_mla_w_ukv, m_mla_q_hnorm, m_mla_k_hnorm, m_mem_norm, m_mem_w_kv, m_mem_q_hnorm, m_mem_k_hnorm, m_w_branch, m_w_out, m_ffn2_norm, m_ffn2_w_in, m_ffn2_w_out)))
    v = dict(zip(WEIGHTS, (v_ffn1_norm, v_ffn1_w_in, v_ffn1_w_out, v_mix_norm, v_w_in, v_conv_dw, v_conv_b, v_conv_ln_g, v_conv_ln_b, v_sb_q_hnorm, v_sb_k_hnorm, v_mla_q_norm, v_mla_w_uq, v_mla_kv_norm, v_mla_w_ukv, v_mla_q_hnorm, v_mla_k_hnorm, v_mem_norm, v_mem_w_kv, v_mem_q_hnorm, v_mem_k_hnorm, v_w_branch, v_w_out, v_ffn2_norm, v_ffn2_w_in, v_ffn2_w_out)))
    S = x.shape[1]
    packed = [pack_layer_shard(w, l) for l in range(DEPTH)]
    conv_dw_full = gather_conv_dw(w["conv_dw"])
    gathered = {0: gather_shards("ag_shards_l0", packed[0])}

    def weights_of(l):
        W = unpack_layer(gathered[l], w)
        W["conv_dw"] = conv_dw_full[l]
        W.update({n: w[n][l] for n in SMALL})
        return W

    def deliver(l, landed):
        if landed is not None:
            gathered[l + 1] = landed

    reducers = {}

    def reducer_of(l, layer_grads):
        reducers[l] = LayerReduce(f"rs_l{l}", layer_grads)
        return reducers[l]

    loss_blk, dx, g = local_step(x[0], mem[0], positions.reshape(S, 1), loss_target[0], weights_of,
                                 lambda l: packed[l + 1] if l + 1 < DEPTH else None, deliver, reducer_of)
    loss = lax.psum(loss_blk[0, 0], ("x", "y", "c"))
    grads = finish_reduce([reducers[l] for l in range(DEPTH)], w)
    small_shapes = {n: w[n].shape for n in SMALL}
    reduced = SMALL + ("conv_dw",)
    reduced_shapes = dict(small_shapes, conv_dw=conv_dw_full.shape)
    g_all = {n: jnp.stack([gl[n] for gl in g]) for n in reduced}
    g_all = sum8("small_sum", all8_gather("small_gather", _small_pack(g_all, reduced)))
    g_all = _small_unpack(g_all, reduced_shapes, reduced)
    width = w["conv_dw"].shape[-1]
    x_pos, y_pos, _ = _coords()
    grads["conv_dw"] = lax.dynamic_slice_in_dim(g_all.pop("conv_dw"), (2 * x_pos + y_pos) * width, width, axis=2)
    grads.update(g_all)
    g_small = _small_pack(grads)
    delta, new_m, new_v = {}, {}, {}
    for n in SHARDED:
        delta[n], new_m[n], new_v[n] = adamw(f"adamw_{n}", w[n], grads[n], m[n], v[n])
    d_s, m_s, v_s = adamw("adamw_small", _small_pack(w), g_small, _small_pack(m), _small_pack(v))
    for t, packed in ((delta, d_s), (new_m, m_s), (new_v, v_s)):
        t.update(_small_unpack(packed, small_shapes))
    return (loss, dx.reshape(x.shape), *[grads[n] for n in WEIGHTS], *[delta[n] for n in WEIGHTS],
            *[new_m[n] for n in WEIGHTS], *[new_v[n] for n in WEIGHTS])
```
